```python
import jax, jax.numpy as jnp
from jax import lax
import numpy as np

D_MODEL = 2048
BATCH = 2
SEQ = 4096
DEPTH = 1
DEC_BATCH = 8
DEC_SEQ = 4096
PAST_LEN = 128

GRID_W = 64
D_MIX = D_MODEL
D_ATT = D_MIX // 2
D_MLSTM = D_MIX // 2
ATT_HEAD_DIM = 128
ATT_HEADS = D_ATT // ATT_HEAD_DIM
ATT_KV_HEADS = 2
ATT_GROUP = ATT_HEADS // ATT_KV_HEADS
Q_BLOCK = 128
ROPE_THETA = 10000.0
ROPE_PAIRS_PER_AXIS = ATT_HEAD_DIM // 4
ML_HEADS = 4
ML_DV = D_MLSTM // ML_HEADS
ML_DQK = ML_DV // 2
ML_CHUNK = 128
N_GROUPS = 4
EXPERTS_PER_GROUP = 4
N_EXPERTS = N_GROUPS * EXPERTS_PER_GROUP
TOP_K = 2
D_FF_EXPERT = D_MODEL // 2
MOE_BLOCK = 128
EPS = 1e-6
C_AQ = ATT_HEADS * ATT_HEAD_DIM
C_AK = ATT_KV_HEADS * ATT_HEAD_DIM
C_AV = ATT_KV_HEADS * ATT_HEAD_DIM
C_MQ = ML_HEADS * ML_DQK
C_MK = ML_HEADS * ML_DQK
C_MV = ML_HEADS * ML_DV
C_MO = ML_HEADS * ML_DV
C_MG = 4 * ML_HEADS
D_IN_PROJ = C_AQ + C_AK + C_AV + C_MQ + C_MK + C_MV + C_MO + C_MG

kernel_name = 'hymba_attn_mlstm_hier_moe_encoder'


def rmsnorm(x, w):
    xf = x.astype(jnp.float32)
    y = xf * lax.rsqrt(jnp.mean(xf * xf, axis=-1, keepdims=True) + EPS)
    return (y * w.astype(jnp.float32)).astype(x.dtype)


def axial_rope(T):
    rows = T // GRID_W
    row = jnp.repeat(jnp.arange(rows, dtype=jnp.float32), GRID_W)
    col = jnp.tile(jnp.arange(GRID_W, dtype=jnp.float32), rows)
    freqs = ROPE_THETA ** (-jnp.arange(ROPE_PAIRS_PER_AXIS, dtype=jnp.float32) / ROPE_PAIRS_PER_AXIS)
    ang = jnp.stack([row[:, None] * freqs, col[:, None] * freqs], axis=1)
    return jnp.cos(ang), jnp.sin(ang)


def apply_rope(x, cos, sin):
    B, T, H, hd = x.shape
    xr = x.astype(jnp.float32).reshape(B, T, H, 2, 2, ROPE_PAIRS_PER_AXIS)
    x1 = xr[..., 0, :]
    x2 = xr[..., 1, :]
    c = cos[None, :, None]
    s = sin[None, :, None]
    out = jnp.stack([x1 * c - x2 * s, x2 * c + x1 * s], axis=-2)
    return out.reshape(B, T, H, hd).astype(x.dtype)


def block_attention(q, k, v):
    B, T = q.shape[:2]
    nqb = T // Q_BLOCK
    qg = q.reshape(B, nqb, Q_BLOCK, ATT_KV_HEADS, ATT_GROUP, ATT_HEAD_DIM).transpose(1, 0, 3, 4, 2, 5)
    kt = k.transpose(0, 2, 1, 3)
    vt = v.transpose(0, 2, 1, 3)
    scale = ATT_HEAD_DIM ** -0.5

    def one_block(qb):
        s = jnp.einsum('bhgqd,bhkd->bhgqk', qb, kt).astype(jnp.float32) * scale
        p = jax.nn.softmax(s, axis=-1).astype(vt.dtype)
        return jnp.einsum('bhgqk,bhkd->bhgqd', p, vt)

    o = lax.map(one_block, qg)
    return o.transpose(1, 0, 4, 2, 3, 5).reshape(B, T, ATT_HEADS * ATT_HEAD_DIM)


def mlstm_bidirectional(q, k, v, i_pre, f_pre):
    B, T = q.shape[:2]
    P = 2 * B * ML_HEADS
    L = ML_CHUNK
    nc = T // L

    def both_dirs(a):
        a = a.astype(jnp.float32).transpose(0, 2, 1, 3)
        return jnp.stack([a, a[:, :, ::-1]], axis=0).reshape(P, T, a.shape[-1])

    def gate_dirs(g):
        g = g.astype(jnp.float32).transpose(2, 0, 3, 1)
        return jnp.stack([g[0], g[1][..., ::-1]], axis=0).reshape(P, T)

    def to_chunks(a):
        return a.reshape(P, nc, L, *a.shape[2:]).swapaxes(0, 1)

    qc = to_chunks(both_dirs(q) * (ML_DQK ** -0.5))
    kc = to_chunks(both_dirs(k))
    vc = to_chunks(both_dirs(v))
    ic = to_chunks(gate_dirs(i_pre))
    bc = jnp.cumsum(to_chunks(jax.nn.log_sigmoid(gate_dirs(f_pre))), axis=-1)
    lower = jnp.tril(jnp.ones((L, L), dtype=bool))

    def step(carry, xs):
        C, n, m = carry
        qb, kb, vb, ib, bb = xs
        m_inter = bb + m[:, None]
        dmat = jnp.where(lower, bb[:, :, None] - bb[:, None, :] + ib[:, None, :], -jnp.inf)
        m_loc = jnp.maximum(m_inter, jnp.max(dmat, axis=-1))
        a_inter = jnp.exp(m_inter - m_loc)
        s = jnp.einsum('pjd,psd->pjs', qb, kb) * jnp.exp(dmat - m_loc[..., None])
        num = a_inter[..., None] * jnp.einsum('pvd,pjd->pjv', C, qb) + jnp.einsum('pjs,psv->pjv', s, vb)
        den = a_inter * jnp.einsum('pd,pjd->pj', n, qb) + jnp.sum(s, axis=-1)
        h = num / jnp.maximum(jnp.abs(den), jnp.exp(-m_loc))[..., None]
        b_end = bb[:, -1]
        g = b_end[:, None] - bb + ib
        m_new = jnp.maximum(b_end + m, jnp.max(g, axis=-1))
        decay = jnp.exp(b_end + m - m_new)
        w = jnp.exp(g - m_new[:, None])
        C = decay[:, None, None] * C + jnp.einsum('ps,psv,psd->pvd', w, vb, kb)
        n = decay[:, None] * n + jnp.einsum('ps,psd->pd', w, kb)
        return (C, n, m_new), h

    init = (jnp.zeros((P, ML_DV, ML_DQK), jnp.float32),
            jnp.zeros((P, ML_DQK), jnp.float32),
            jnp.zeros((P,), jnp.float32))
    _, h = lax.scan(step, init, (qc, kc, vc, ic, bc))
    h = h.swapaxes(0, 1).reshape(2, B, ML_HEADS, T, ML_DV)
    h = h[0] + h[1][:, :, ::-1]
    return h.transpose(0, 2, 1, 3)


def hierarchical_moe(x, w_gr, b_gr, w_er, b_er, w_gate, w_up, w_down):
    N, D = x.shape
    p_grp = jax.nn.softmax((x @ w_gr).astype(jnp.float32) + b_gr, axis=-1)
    g_idx = jnp.argmax(p_grp, axis=-1)
    p_g = jnp.take_along_axis(p_grp, g_idx[:, None], axis=-1)[:, 0]
    el = ((x @ w_er).astype(jnp.float32) + b_er).reshape(N, N_GROUPS, EXPERTS_PER_GROUP)
    el_g = jnp.take_along_axis(el, g_idx[:, None, None], axis=1)[:, 0]
    top_w, top_i = lax.top_k(jax.nn.softmax(el_g, axis=-1), TOP_K)
    top_w = top_w / jnp.sum(top_w, axis=-1, keepdims=True) * p_g[:, None]
    M = N * TOP_K
    eid = (g_idx[:, None] * EXPERTS_PER_GROUP + top_i).reshape(M).astype(jnp.int32)
    tok = jnp.repeat(jnp.arange(N, dtype=jnp.int32), TOP_K)
    wt = top_w.reshape(M)
    order = jnp.argsort(eid)
    eid_s = eid[order]
    tok_s = tok[order]
    wt_s = wt[order]
    counts = jnp.bincount(eid, length=N_EXPERTS)
    start = jnp.cumsum(counts) - counts
    padded = (counts + MOE_BLOCK - 1) // MOE_BLOCK * MOE_BLOCK
    pend = jnp.cumsum(padded)
    pstart = pend - padded
    dest = pstart[eid_s] + (jnp.arange(M, dtype=jnp.int32) - start[eid_s])
    nb = (M + MOE_BLOCK - 1) // MOE_BLOCK + N_EXPERTS
    m_pad = nb * MOE_BLOCK
    row_tok = jnp.zeros((m_pad,), jnp.int32).at[dest].set(tok_s)
    row_wt = jnp.zeros((m_pad,), jnp.float32).at[dest].set(wt_s)
    blk_e = jnp.minimum(jnp.searchsorted(pend, jnp.arange(nb) * MOE_BLOCK, side='right'), N_EXPERTS - 1)

    def one_block(args):
        toks, wts, e = args
        xb = x[toks]
        h = jax.nn.silu(xb @ w_gate[e]) * (xb @ w_up[e])
        out = h @ w_down[e]
        return out * wts[:, None].astype(out.dtype)

    out = lax.map(one_block, (row_tok.reshape(nb, MOE_BLOCK), row_wt.reshape(nb, MOE_BLOCK), blk_e))
    return jnp.zeros_like(x).at[row_tok].add(out.reshape(m_pad, D).astype(x.dtype))


def encoder_layer(x, c, norm1_w, norm2_w, w_ada, b_ada, w_in, q_norm_w, k_norm_w,
                  b_igate, b_fgate, ml_norm_w, w_out, w_gr, b_gr, w_er, b_er,
                  w_gate, w_up, w_down):
    B, T, D = x.shape
    mod = (jax.nn.silu(c) @ w_ada + b_ada).reshape(B, 6, D)[:, :, None, :]
    shift1, scale1, gate1, shift2, scale2, gate2 = (mod[:, i] for i in range(6))
    h = rmsnorm(x, norm1_w) * (1.0 + scale1) + shift1
    proj = h @ w_in
    cuts = np.cumsum([C_AQ, C_AK, C_AV, C_MQ, C_MK, C_MV, C_MO]).tolist()
    aq, ak, av, mq, mk, mv, mo, mg = jnp.split(proj, cuts, axis=-1)
    cos, sin = axial_rope(T)
    aq = apply_rope(rmsnorm(aq.reshape(B, T, ATT_HEADS, ATT_HEAD_DIM), q_norm_w), cos, sin)
    ak = apply_rope(rmsnorm(ak.reshape(B, T, ATT_KV_HEADS, ATT_HEAD_DIM), k_norm_w), cos, sin)
    av = av.reshape(B, T, ATT_KV_HEADS, ATT_HEAD_DIM)
    att = block_attention(aq, ak, av)
    gates = mg.astype(jnp.float32).reshape(B, T, 2, 2, ML_HEADS)
    i_pre = gates[:, :, 0] + b_igate
    f_pre = gates[:, :, 1] + b_fgate
    ml = mlstm_bidirectional(mq.reshape(B, T, ML_HEADS, ML_DQK), mk.reshape(B, T, ML_HEADS, ML_DQK),
                             mv.reshape(B, T, ML_HEADS, ML_DV), i_pre, f_pre)
    ml = rmsnorm(ml, ml_norm_w.reshape(ML_HEADS, ML_DV)).reshape(B, T, D_MLSTM)
    ml = (ml * jax.nn.sigmoid(mo.astype(jnp.float32))).astype(x.dtype)
    mix = jnp.concatenate([att.astype(x.dtype), ml], axis=-1) @ w_out
    x = x + gate1 * mix
    h2 = rmsnorm(x, norm2_w) * (1.0 + scale2) + shift2
    moe = hierarchical_moe(h2.reshape(B * T, D), w_gr, b_gr, w_er, b_er, w_gate, w_up, w_down)
    return x + gate2 * moe.reshape(B, T, D)


def setup_inputs(seed: int = 0) -> dict:
    key = jax.random.key(seed)
    ks = jax.random.split(key, 24)
    f32 = jnp.float32
    nrm = lambda k, shape, s: jax.random.normal(k, shape, f32) * s
    D = D_MODEL
    return {
        'x_prompt': nrm(ks[0], (BATCH, SEQ, D), 1.0),
        'x_sample': nrm(ks[1], (DEC_BATCH, DEC_SEQ, D), 1.0),
        'c_prompt': nrm(ks[2], (BATCH, D), 1.0),
        'c_sample': nrm(ks[3], (DEC_BATCH, D), 1.0),
        'norm1_w': 1.0 + nrm(ks[4], (DEPTH, D), 0.05),
        'norm2_w': 1.0 + nrm(ks[5], (DEPTH, D), 0.05),
        'w_ada': nrm(ks[6], (DEPTH, D, 6 * D), 0.5 * D ** -0.5),
        'b_ada': nrm(ks[7], (DEPTH, 6 * D), 0.02),
        'w_in': nrm(ks[8], (DEPTH, D, D_IN_PROJ), D ** -0.5),
        'q_norm_w': 1.0 + nrm(ks[9], (DEPTH, ATT_HEAD_DIM), 0.05),
        'k_norm_w': 1.0 + nrm(ks[10], (DEPTH, ATT_HEAD_DIM), 0.05),
        'b_igate': -1.0 + nrm(ks[11], (DEPTH, 2, ML_HEADS), 0.1),
        'b_fgate': jnp.linspace(3.0, 6.0, ML_HEADS, dtype=f32) + nrm(ks[12], (DEPTH, 2, ML_HEADS), 0.1),
        'ml_norm_w': 1.0 + nrm(ks[13], (DEPTH, D_MLSTM), 0.05),
        'w_out': nrm(ks[14], (DEPTH, D_MIX, D), D_MIX ** -0.5),
        'w_gr': nrm(ks[15], (DEPTH, D, N_GROUPS), D ** -0.5),
        'b_gr': nrm(ks[16], (DEPTH, N_GROUPS), 0.01),
        'w_er': nrm(ks[17], (DEPTH, D, N_EXPERTS), D ** -0.5),
        'b_er': nrm(ks[18], (DEPTH, N_EXPERTS), 0.01),
        'w_gate': nrm(ks[19], (DEPTH, N_EXPERTS, D, D_FF_EXPERT), D ** -0.5),
        'w_up': nrm(ks[20], (DEPTH, N_EXPERTS, D, D_FF_EXPERT), D ** -0.5),
        'w_down': nrm(ks[21], (DEPTH, N_EXPERTS, D_FF_EXPERT, D), D_FF_EXPERT ** -0.5),
    }


def reference(x_prompt, x_sample, c_prompt, c_sample, norm1_w, norm2_w, w_ada, b_ada,
              w_in, q_norm_w, k_norm_w, b_igate, b_fgate, ml_norm_w, w_out,
              w_gr, b_gr, w_er, b_er, w_gate, w_up, w_down):
    y_prompt = x_prompt
    y_sample = x_sample
    for l in range(DEPTH):
        lp = (norm1_w[l], norm2_w[l], w_ada[l], b_ada[l], w_in[l], q_norm_w[l], k_norm_w[l],
              b_igate[l], b_fgate[l], ml_norm_w[l], w_out[l], w_gr[l], b_gr[l], w_er[l], b_er[l],
              w_gate[l], w_up[l], w_down[l])
        y_prompt = encoder_layer(y_prompt, c_prompt, *lp)
        y_sample = encoder_layer(y_sample, c_sample, *lp)
    return (y_prompt, y_sample)
```

```python
import functools

import jax
import jax.numpy as jnp
import numpy as np
from jax import lax
from jax.experimental import pallas as pl
from jax.experimental.pallas import tpu as pltpu

F32 = jnp.float32
BF16 = jnp.bfloat16
I32 = jnp.int32

GRID_W = 64
HEAD_DIM = 128
ATT_HEADS = 8
ATT_KV_HEADS = 2
ATT_GROUP = ATT_HEADS // ATT_KV_HEADS
ROPE_THETA = 10000.0
ROPE_PAIRS = HEAD_DIM // 4
ML_HEADS = 4
ML_DV = 256
ML_DQK = 128
ML_CHUNK = 128
N_GROUPS = 4
EXPERTS_PER_GROUP = 4
N_EXPERTS = N_GROUPS * EXPERTS_PER_GROUP
EPS = 1e-6

C_AQ = ATT_HEADS * HEAD_DIM
C_AK = ATT_KV_HEADS * HEAD_DIM
C_AV = ATT_KV_HEADS * HEAD_DIM
C_MQ = ML_HEADS * ML_DQK
C_MK = ML_HEADS * ML_DQK
C_MV = ML_HEADS * ML_DV
C_MO = ML_HEADS * ML_DV
C_MG = 4 * ML_HEADS
C_MAIN = C_AQ + C_AK + C_AV + C_MQ + C_MK + C_MV + C_MO

LANES = 128
ROUTER_ROWS = 32
MIB = 1024 * 1024

TM_PROJ = 256
TQ_ATT = 256
TM_MIX = 256
TL_ROUTE = 512
TD_DISPATCH = 256
TM_MOE = 256
TC_COMBINE = 256


def _params(semantics, vmem_mib):
    return pltpu.CompilerParams(dimension_semantics=semantics, vmem_limit_bytes=vmem_mib * MIB)


def _resident(shape):
    nd = len(shape)
    return pl.BlockSpec(shape, lambda *_: (0,) * nd, pipeline_mode=pl.Buffered(1))


def _ada_kernel(c_ref, w_ref, b_ref, o_ref):
    c = c_ref[...]
    s = (c * jax.nn.sigmoid(c)).astype(BF16)
    o_ref[...] = jnp.dot(s, w_ref[...].astype(BF16), preferred_element_type=F32) + b_ref[...]


def _ada_modulation(c_pad, w_ada, b_ada):
    rows, d = c_pad.shape
    n = w_ada.shape[1]
    tn = 1024
    return pl.pallas_call(
        _ada_kernel,
        grid=(n // tn,),
        in_specs=[pl.BlockSpec((rows, d), lambda j: (0, 0)),
                  pl.BlockSpec((d, tn), lambda j: (0, j)),
                  pl.BlockSpec((1, tn), lambda j: (0, j))],
        out_specs=pl.BlockSpec((rows, tn), lambda j: (0, j)),
        out_shape=jax.ShapeDtypeStruct((rows, n), F32),
        compiler_params=_params(("arbitrary",), 40),
    )(c_pad, w_ada, b_ada.reshape(1, n))


def _rms(x, w):
    return x * lax.rsqrt(jnp.mean(x * x, axis=-1, keepdims=True) + EPS) * w


def _inproj_kernel(x_ref, mod_ref, n1_ref, w_ref, wg_ref, qn_ref, kn_ref, cos_ref, sin_ref,
                   aq_ref, ak_ref, av_ref, mq_ref, mk_ref, mv_ref, mo_ref, mg_ref):
    x = x_ref[0]
    h = _rms(x, n1_ref[...]) * (1.0 + mod_ref[0, 1:2, :]) + mod_ref[0, 0:1, :]
    hb = h.astype(BF16)

    def proj(c0, width):
        return jnp.dot(hb, w_ref[:, c0:c0 + width], preferred_element_type=F32)

    cos = cos_ref[...]
    sin = sin_ref[...]
    lane = lax.broadcasted_iota(I32, (1, HEAD_DIM), 1)
    first = (lane % (2 * ROPE_PAIRS)) < ROPE_PAIRS

    def norm_rope(p, w):
        pn = _rms(p, w)
        partner = jnp.where(first, pltpu.roll(pn, HEAD_DIM - ROPE_PAIRS, 1), pltpu.roll(pn, ROPE_PAIRS, 1))
        return pn * cos + partner * sin

    c0 = 0
    for half in range(2):
        p = proj(c0, C_AQ // 2)
        for hh in range(ATT_HEADS // 2):
            col = half * (C_AQ // 2) + hh * HEAD_DIM
            aq_ref[0, :, col:col + HEAD_DIM] = norm_rope(p[:, hh * HEAD_DIM:(hh + 1) * HEAD_DIM], qn_ref[...]).astype(BF16)
        c0 += C_AQ // 2
    p = proj(c0, C_AK + C_AV)
    for hh in range(ATT_KV_HEADS):
        ak_ref[0, :, hh * HEAD_DIM:(hh + 1) * HEAD_DIM] = norm_rope(p[:, hh * HEAD_DIM:(hh + 1) * HEAD_DIM], kn_ref[...]).astype(BF16)
    av_ref[0] = p[:, C_AK:].astype(BF16)
    c0 += C_AK + C_AV
    mq_ref[0] = (proj(c0, C_MQ) * (ML_DQK ** -0.5)).astype(BF16)
    c0 += C_MQ
    mk_ref[0] = proj(c0, C_MK).astype(BF16)
    c0 += C_MK
    for half in range(2):
        mv_ref[0, :, half * 512:(half + 1) * 512] = proj(c0, 512).astype(BF16)
        c0 += 512
    for half in range(2):
        mo_ref[0, :, half * 512:(half + 1) * 512] = proj(c0, 512)
        c0 += 512
    mg_ref[0] = jnp.dot(hb, wg_ref[...], preferred_element_type=F32)


def _in_projection(x, mod, norm1_w, w_main, w_gates, q_norm_w, k_norm_w, cos_t, sin_t):
    b, t, d = x.shape
    tm = min(TM_PROJ, t)
    row = lambda width: pl.BlockSpec((1, tm, width), lambda bi, i: (bi, i, 0))
    widths = (C_AQ, C_AK, C_AV, C_MQ, C_MK, C_MV, C_MO, LANES)
    dtypes = (BF16, BF16, BF16, BF16, BF16, BF16, F32, F32)
    return pl.pallas_call(
        _inproj_kernel,
        grid=(b, t // tm),
        in_specs=[row(d),
                  pl.BlockSpec((1, 6, d), lambda bi, i: (bi, 0, 0)),
                  _resident((1, d)),
                  _resident((d, C_MAIN)),
                  _resident((d, LANES)),
                  _resident((1, HEAD_DIM)),
                  _resident((1, HEAD_DIM)),
                  pl.BlockSpec((tm, HEAD_DIM), lambda bi, i: (i, 0)),
                  pl.BlockSpec((tm, HEAD_DIM), lambda bi, i: (i, 0))],
        out_specs=[row(w) for w in widths],
        out_shape=[jax.ShapeDtypeStruct((b, t, w), dt) for w, dt in zip(widths, dtypes)],
        compiler_params=_params(("parallel", "arbitrary"), 48),
    )(x, mod, norm1_w, w_main, w_gates, q_norm_w, k_norm_w, cos_t, sin_t)


def _attention_kernel(q_ref, k_ref, v_ref, o_ref):
    k = k_ref[0]
    v = v_ref[0]
    scale = HEAD_DIM ** -0.5
    for g in range(ATT_GROUP):
        q = q_ref[0, :, g * HEAD_DIM:(g + 1) * HEAD_DIM]
        s = lax.dot_general(q, k, (((1,), (1,)), ((), ())), preferred_element_type=F32) * scale
        m = jnp.max(s, axis=-1, keepdims=True)
        p = jnp.exp(s - m)
        l = jnp.sum(p, axis=-1, keepdims=True)
        o = jnp.dot(p.astype(BF16), v, preferred_element_type=F32)
        o_ref[0, :, g * HEAD_DIM:(g + 1) * HEAD_DIM] = (o / l).astype(BF16)


def _attention(aq, ak, av):
    b, t, _ = aq.shape
    tq = min(TQ_ATT, t)
    gw = ATT_GROUP * HEAD_DIM
    return pl.pallas_call(
        _attention_kernel,
        grid=(b, ATT_KV_HEADS, t // tq),
        in_specs=[pl.BlockSpec((1, tq, gw), lambda bi, h, i: (bi, i, h)),
                  pl.BlockSpec((1, t, HEAD_DIM), lambda bi, h, i: (bi, 0, h)),
                  pl.BlockSpec((1, t, HEAD_DIM), lambda bi, h, i: (bi, 0, h))],
        out_specs=pl.BlockSpec((1, tq, gw), lambda bi, h, i: (bi, i, h)),
        out_shape=jax.ShapeDtypeStruct((b, t, C_AQ), BF16),
        compiler_params=_params(("parallel", "parallel", "arbitrary"), 48),
    )(aq, ak, av)


def _log_sigmoid(x):
    return jnp.minimum(x, 0.0) - jnp.log1p(jnp.exp(-jnp.abs(x)))


def _split3(a):
    a1 = a.astype(BF16)
    r1 = a - a1.astype(F32)
    a2 = r1.astype(BF16)
    a3 = (r1 - a2.astype(F32)).astype(BF16)
    return a1, a2, a3


def _lane_pick(a, idx):
    lane = lax.broadcasted_iota(I32, a.shape, 1)
    return jnp.sum(jnp.where(lane == idx, a, 0.0), axis=-1, keepdims=True)


def _mlstm_direction(reverse, q_ref, k_ref, v_ref, gc_ref, gr_ref, bc_ref, br_ref, h_ref,
                     ct_ref, n_ref, m_ref):
    L = ML_CHUNK
    d = 1 if reverse else 0
    row = lax.broadcasted_iota(I32, (L, L), 0)
    col = lax.broadcasted_iota(I32, (L, L), 1)
    lower = (col <= row)
    upper = (col >= row)
    lower_b = lower.astype(BF16)
    upper_b = upper.astype(BF16)
    cum_cols = upper_b if reverse else lower_b
    cum_rows = lower_b if reverse else upper_b
    mask = upper if reverse else lower

    pre_c = gc_ref[0] + bc_ref[...]
    pre_r = gr_ref[0] + br_ref[...]
    b_cols = sum(jnp.dot(cum_cols, part, preferred_element_type=F32) for part in _split3(_log_sigmoid(pre_c)))
    b_rows = sum(jnp.dot(part, cum_rows, preferred_element_type=F32) for part in _split3(_log_sigmoid(pre_r)))
    end = 0 if reverse else L - 1

    for hh in range(ML_HEADS):
        ci = d * ML_HEADS + hh
        cf = 2 * ML_HEADS + d * ML_HEADS + hh
        sidx = d * ML_HEADS + hh
        i_row = pre_r[ci:ci + 1, :]
        b_row = b_rows[cf:cf + 1, :]
        b_col = _lane_pick(b_cols, cf)
        m = m_ref[sidx][0:1, 0:1]
        dmat = jnp.where(mask, b_col - b_row + i_row, -jnp.inf)
        m_inter = b_col + m
        m_loc = jnp.maximum(m_inter, jnp.max(dmat, axis=-1, keepdims=True))
        a_inter = jnp.exp(m_inter - m_loc)
        q = q_ref[0, :, hh * ML_DQK:(hh + 1) * ML_DQK]
        k = k_ref[0, :, hh * ML_DQK:(hh + 1) * ML_DQK]
        v = v_ref[0, :, hh * ML_DV:(hh + 1) * ML_DV]
        qk = lax.dot_general(q, k, (((1,), (1,)), ((), ())), preferred_element_type=F32)
        s = qk * jnp.exp(dmat - m_loc)
        ct = ct_ref[sidx]
        n_row = n_ref[sidx]
        num = a_inter * jnp.dot(q, ct.astype(BF16), preferred_element_type=F32) \
            + jnp.dot(s.astype(BF16), v, preferred_element_type=F32)
        qn = jnp.sum(q.astype(F32) * n_row, axis=-1, keepdims=True)
        den = a_inter * qn + jnp.sum(s, axis=-1, keepdims=True)
        h_ref[0, :, hh * ML_DV:(hh + 1) * ML_DV] = num / jnp.maximum(jnp.abs(den), jnp.exp(-m_loc))

        b_end = _lane_pick(b_row, end)
        g_row = b_end - b_row + i_row
        m_new = jnp.maximum(b_end + m, jnp.max(g_row, axis=-1, keepdims=True))
        decay = jnp.exp(b_end + m - m_new)
        w_row = jnp.exp(g_row - m_new)
        ktw = (k.astype(F32).T * w_row).astype(BF16)
        ct_ref[sidx] = decay * ct + jnp.dot(ktw, v, preferred_element_type=F32)
        wk = jnp.dot(jnp.broadcast_to(w_row.astype(BF16), (8, L)), k, preferred_element_type=F32)
        n_ref[sidx] = decay * n_row + wk[0:1, :]
        m_ref[sidx] = jnp.broadcast_to(m_new, (8, LANES))


def _mlstm_kernel(qf, kf, vf, gcf, grf, qb, kb, vb, gcb, grb, bc_ref, br_ref, hf_ref, hb_ref,
                  ct_ref, n_ref, m_ref):
    @pl.when(pl.program_id(1) == 0)
    def _():
        ct_ref[...] = jnp.zeros_like(ct_ref)
        n_ref[...] = jnp.zeros_like(n_ref)
        m_ref[...] = jnp.zeros_like(m_ref)

    _mlstm_direction(False, qf, kf, vf, gcf, grf, bc_ref, br_ref, hf_ref, ct_ref, n_ref, m_ref)
    _mlstm_direction(True, qb, kb, vb, gcb, grb, bc_ref, br_ref, hb_ref, ct_ref, n_ref, m_ref)


def _mlstm(mq, mk, mv, mg, mg_t, bias_col, bias_row):
    b, t, _ = mq.shape
    L = ML_CHUNK
    nc = t // L
    fw = lambda width: pl.BlockSpec((1, L, width), lambda bi, c: (bi, c, 0))
    bw = lambda width: pl.BlockSpec((1, L, width), lambda bi, c: (bi, nc - 1 - c, 0))
    fw_t = pl.BlockSpec((1, C_MG, L), lambda bi, c: (bi, 0, c))
    bw_t = pl.BlockSpec((1, C_MG, L), lambda bi, c: (bi, 0, nc - 1 - c))
    ns = 2 * ML_HEADS
    return pl.pallas_call(
        _mlstm_kernel,
        grid=(b, nc),
        in_specs=[fw(C_MQ), fw(C_MK), fw(C_MV), fw(LANES), fw_t,
                  bw(C_MQ), bw(C_MK), bw(C_MV), bw(LANES), bw_t,
                  _resident((1, LANES)), _resident((C_MG, L))],
        out_specs=[fw(C_MV), bw(C_MV)],
        out_shape=[jax.ShapeDtypeStruct((b, t, C_MV), F32)] * 2,
        scratch_shapes=[pltpu.VMEM((ns, ML_DQK, ML_DV), F32),
                        pltpu.VMEM((ns, 1, ML_DQK), F32),
                        pltpu.VMEM((ns, 8, LANES), F32)],
        compiler_params=_params(("parallel", "arbitrary"), 32),
    )(mq, mk, mv, mg, mg_t, mq, mk, mv, mg, mg_t, bias_col, bias_row)


def _mix_kernel(x_ref, att_ref, hf_ref, hb_ref, mo_ref, mlw_ref, wo_ref, mod_ref, n2_ref,
                wrh_ref, wrl_ref, br_ref, x1_ref, h2_ref, lg_ref):
    ml = hf_ref[0] + hb_ref[0]
    gate = jax.nn.sigmoid(mo_ref[0])
    mix = jnp.dot(att_ref[0], wo_ref[0:C_AQ, :], preferred_element_type=F32)
    for hh in range(ML_HEADS):
        sl = slice(hh * ML_DV, (hh + 1) * ML_DV)
        seg = (_rms(ml[:, sl], mlw_ref[:, sl]) * gate[:, sl]).astype(BF16)
        mix += jnp.dot(seg, wo_ref[C_AQ + hh * ML_DV:C_AQ + (hh + 1) * ML_DV, :], preferred_element_type=F32)
    x1 = x_ref[0] + mod_ref[0, 2:3, :] * mix
    x1_ref[0] = x1
    h2 = _rms(x1, n2_ref[...]) * (1.0 + mod_ref[0, 4:5, :]) + mod_ref[0, 3:4, :]
    h2_ref[0] = h2
    h_hi = h2.astype(BF16)
    h_lo = (h2 - h_hi.astype(F32)).astype(BF16)
    nt = (((1,), (1,)), ((), ()))
    lg = lax.dot_general(wrh_ref[...], h_hi, nt, preferred_element_type=F32)
    lg += lax.dot_general(wrl_ref[...], h_hi, nt, preferred_element_type=F32)
    lg += lax.dot_general(wrh_ref[...], h_lo, nt, preferred_element_type=F32)
    lg_ref[0] = lg + br_ref[:, 0:1]


def _mix_and_router(x, att, h_fw, h_bw, mo, ml_norm_w, w_out, mod, norm2_w, wr_hi, wr_lo, b_router):
    b, t, d = x.shape
    tm = min(TM_MIX, t)
    row = lambda width: pl.BlockSpec((1, tm, width), lambda bi, i: (bi, i, 0))
    return pl.pallas_call(
        _mix_kernel,
        grid=(b, t // tm),
        in_specs=[row(d), row(C_AQ), row(C_MV), row(C_MV), row(C_MO),
                  _resident((1, C_MV)), _resident((d, d)),
                  pl.BlockSpec((1, 6, d), lambda bi, i: (bi, 0, 0)),
                  _resident((1, d)), _resident((ROUTER_ROWS, d)), _resident((ROUTER_ROWS, d)),
                  _resident((ROUTER_ROWS, LANES))],
        out_specs=[row(d), row(d), pl.BlockSpec((1, ROUTER_ROWS, tm), lambda bi, i: (bi, 0, i))],
        out_shape=[jax.ShapeDtypeStruct((b, t, d), F32), jax.ShapeDtypeStruct((b, t, d), F32),
                   jax.ShapeDtypeStruct((b, ROUTER_ROWS, t), F32)],
        compiler_params=_params(("parallel", "arbitrary"), 48),
    )(x, att, h_fw, h_bw, mo, ml_norm_w, w_out, mod, norm2_w, wr_hi, wr_lo, b_router)


def _first_argmax(rows):
    best = rows[0]
    idx = jnp.zeros_like(best)
    for j in range(1, len(rows)):
        better = rows[j] > best
        best = jnp.where(better, rows[j], best)
        idx = jnp.where(better, float(j), idx)
    return best, idx


def _softmax_rows(rows):
    mx = functools.reduce(jnp.maximum, rows)
    ex = [jnp.exp(r - mx) for r in rows]
    tot = functools.reduce(lambda a, c: a + c, ex)
    return [e / tot for e in ex]


def _route_kernel(lg_ref, idx_ref, wt_ref, cnt_ref, run_ref):
    @pl.when((pl.program_id(0) == 0) & (pl.program_id(1) == 0))
    def _():
        run_ref[...] = jnp.zeros_like(run_ref)

    tl = lg_ref.shape[2]
    lg = lg_ref[0]
    p_grp = _softmax_rows([lg[g:g + 1, :] for g in range(N_GROUPS)])
    p_g, g_idx = _first_argmax(p_grp)
    el = []
    for j in range(EXPERTS_PER_GROUP):
        sel = lg[N_GROUPS + (N_GROUPS - 1) * EXPERTS_PER_GROUP + j:N_GROUPS + (N_GROUPS - 1) * EXPERTS_PER_GROUP + j + 1, :]
        for g in range(N_GROUPS - 2, -1, -1):
            r = N_GROUPS + g * EXPERTS_PER_GROUP + j
            sel = jnp.where(g_idx == float(g), lg[r:r + 1, :], sel)
        el.append(sel)
    pe = _softmax_rows(el)
    w1, i1 = _first_argmax(pe)
    rest = [jnp.where(i1 == float(j), -1.0, pe[j]) for j in range(EXPERTS_PER_GROUP)]
    w2, i2 = _first_argmax(rest)
    tot = w1 + w2
    wt1 = w1 / tot * p_g
    wt2 = w2 / tot * p_g
    e1 = (g_idx * float(EXPERTS_PER_GROUP) + i1).astype(I32)
    e2 = (g_idx * float(EXPERTS_PER_GROUP) + i2).astype(I32)

    erow = lax.broadcasted_iota(I32, (N_EXPERTS, tl), 0)
    hit1 = erow == e1
    hit2 = erow == e2
    onehot = (hit1 | hit2).astype(F32)
    before = lax.broadcasted_iota(I32, (tl, tl), 0) <= lax.broadcasted_iota(I32, (tl, tl), 1)
    incl = jnp.dot(onehot.astype(BF16), before.astype(BF16), preferred_element_type=F32)
    base = run_ref[:, 0:1] + incl - onehot
    r1 = jnp.sum(jnp.where(hit1, base, 0.0), axis=0, keepdims=True).astype(I32)
    r2 = jnp.sum(jnp.where(hit2, base, 0.0), axis=0, keepdims=True).astype(I32)
    total = run_ref[...] + jnp.sum(onehot, axis=-1, keepdims=True)
    run_ref[...] = total
    cnt_ref[...] = total.astype(I32)

    zi = jnp.zeros((4, tl), I32)
    idx_ref[0] = jnp.concatenate([e1, e2, r1, r2, zi], axis=0)
    wt_ref[0] = jnp.concatenate([wt1, wt2, jnp.zeros((6, tl), F32)], axis=0)


def _route(logits_t):
    b, _, t = logits_t.shape
    tl = min(TL_ROUTE, t)
    blk = lambda rows: pl.BlockSpec((1, rows, tl), lambda bi, i: (bi, 0, i))
    return pl.pallas_call(
        _route_kernel,
        grid=(b, t // tl),
        in_specs=[blk(ROUTER_ROWS)],
        out_specs=[blk(8), blk(8), pl.BlockSpec((N_EXPERTS, LANES), lambda bi, i: (0, 0))],
        out_shape=[jax.ShapeDtypeStruct((b, 8, t), I32), jax.ShapeDtypeStruct((b, 8, t), F32),
                   jax.ShapeDtypeStruct((N_EXPERTS, LANES), I32)],
        scratch_shapes=[pltpu.VMEM((N_EXPERTS, LANES), F32)],
        compiler_params=_params(("arbitrary", "arbitrary"), 32),
    )(logits_t)


def _row_copy(src_ref, src_row, dst_ref, dst_row, sem):
    return pltpu.make_async_copy(src_ref.at[pl.ds(src_row, 1), :], dst_ref.at[pl.ds(dst_row, 1), :], sem)


def _dispatch_kernel(pstart_ref, idx_ref, h_ref, zeros_ref, xs_ref, sem):
    del zeros_ref
    td = h_ref.shape[0]

    def copies(t):
        out = []
        for k in range(2):
            dest = pstart_ref[idx_ref[0, k, t]] + idx_ref[0, 2 + k, t]
            out.append(_row_copy(h_ref, t, xs_ref, dest, sem))
        return out

    def start(t, carry):
        for c in copies(t):
            c.start()
        return carry

    def wait(t, carry):
        for c in copies(t):
            c.wait()
        return carry

    lax.fori_loop(0, td, start, 0)
    lax.fori_loop(0, td, wait, 0)


def _dispatch(h2, idx, pstart, m_pad):
    n, d = h2.shape
    b, _, t = idx.shape
    td = min(TD_DISPATCH, t)
    per = t // td
    zeros = jnp.zeros((m_pad, d), F32)
    return pl.pallas_call(
        _dispatch_kernel,
        grid_spec=pltpu.PrefetchScalarGridSpec(
            num_scalar_prefetch=1,
            grid=(n // td,),
            in_specs=[pl.BlockSpec((1, 8, td), lambda i, ps: (i // per, 0, i % per), memory_space=pltpu.SMEM),
                      pl.BlockSpec((td, d), lambda i, ps: (i, 0)),
                      pl.BlockSpec(memory_space=pl.ANY)],
            out_specs=pl.BlockSpec(memory_space=pl.ANY),
            scratch_shapes=[pltpu.SemaphoreType.DMA(())]),
        out_shape=jax.ShapeDtypeStruct((m_pad, d), F32),
        input_output_aliases={3: 0},
        compiler_params=_params(("arbitrary",), 32),
    )(pstart, idx, h2, zeros)


def _moe_kernel(blk_e_ref, nused_ref, x_ref, wg_ref, wu_ref, wd_ref, o_ref):
    r = pl.program_id(0)

    @pl.when(r < nused_ref[0])
    def _():
        xb = x_ref[...].astype(BF16)
        g = jnp.dot(xb, wg_ref[0], preferred_element_type=F32)
        u = jnp.dot(xb, wu_ref[0], preferred_element_type=F32)
        h = (g * jax.nn.sigmoid(g) * u).astype(BF16)
        o_ref[...] = jnp.dot(h, wd_ref[0], preferred_element_type=F32)

    @pl.when(r >= nused_ref[0])
    def _():
        o_ref[...] = jnp.zeros_like(o_ref)


def _moe_experts(xs, blk_e, nused, w_gate, w_up, w_down):
    m_pad, d = xs.shape
    f = w_gate.shape[2]
    tm = TM_MOE
    last = lambda r, nu: jnp.minimum(r, nu[0] - 1)
    return pl.pallas_call(
        _moe_kernel,
        grid_spec=pltpu.PrefetchScalarGridSpec(
            num_scalar_prefetch=2,
            grid=(m_pad // tm,),
            in_specs=[pl.BlockSpec((tm, d), lambda r, be, nu: (last(r, nu), 0)),
                      pl.BlockSpec((1, d, f), lambda r, be, nu: (be[last(r, nu)], 0, 0)),
                      pl.BlockSpec((1, d, f), lambda r, be, nu: (be[last(r, nu)], 0, 0)),
                      pl.BlockSpec((1, f, d), lambda r, be, nu: (be[last(r, nu)], 0, 0))],
            out_specs=pl.BlockSpec((tm, d), lambda r, be, nu: (r, 0))),
        out_shape=jax.ShapeDtypeStruct((m_pad, d), F32),
        compiler_params=_params(("arbitrary",), 56),
    )(blk_e, nused, xs, w_gate, w_up, w_down)


def _combine_kernel(pstart_ref, idx_ref, x1_ref, w_ref, mod_ref, o_ref, y_ref, g0_ref, g1_ref, sem):
    tc = x1_ref.shape[0]
    bufs = (g0_ref, g1_ref)

    def copies(t):
        out = []
        for k in range(2):
            src = pstart_ref[idx_ref[0, k, t]] + idx_ref[0, 2 + k, t]
            out.append(_row_copy(o_ref, src, bufs[k], t, sem))
        return out

    def start(t, carry):
        for c in copies(t):
            c.start()
        return carry

    def wait(t, carry):
        for c in copies(t):
            c.wait()
        return carry

    lax.fori_loop(0, tc, start, 0)
    lax.fori_loop(0, tc, wait, 0)
    moe = g0_ref[...] * w_ref[:, 0:1] + g1_ref[...] * w_ref[:, 1:2]
    y_ref[...] = x1_ref[...] + mod_ref[0, 5:6, :] * moe


def _combine(x1, idx, w_cols, mod, o_rows, pstart):
    n, d = x1.shape
    b, _, t = idx.shape
    tc = min(TC_COMBINE, t)
    per = t // tc
    return pl.pallas_call(
        _combine_kernel,
        grid_spec=pltpu.PrefetchScalarGridSpec(
            num_scalar_prefetch=1,
            grid=(n // tc,),
            in_specs=[pl.BlockSpec((1, 8, tc), lambda i, ps: (i // per, 0, i % per), memory_space=pltpu.SMEM),
                      pl.BlockSpec((tc, d), lambda i, ps: (i, 0)),
                      pl.BlockSpec((tc, 2), lambda i, ps: (i, 0)),
                      pl.BlockSpec((1, 6, d), lambda i, ps: (i // per, 0, 0)),
                      pl.BlockSpec(memory_space=pl.ANY)],
            out_specs=pl.BlockSpec((tc, d), lambda i, ps: (i, 0)),
            scratch_shapes=[pltpu.VMEM((tc, d), F32), pltpu.VMEM((tc, d), F32), pltpu.SemaphoreType.DMA(())]),
        out_shape=jax.ShapeDtypeStruct((n, d), F32),
        compiler_params=_params(("arbitrary",), 32),
    )(pstart, idx, x1, w_cols, mod, o_rows)


def _rope_tables(t):
    rows = t // GRID_W
    row = jnp.repeat(jnp.arange(rows, dtype=F32), GRID_W)
    col = jnp.tile(jnp.arange(GRID_W, dtype=F32), rows)
    freqs = ROPE_THETA ** (-jnp.arange(ROPE_PAIRS, dtype=F32) / ROPE_PAIRS)
    ar = row[:, None] * freqs
    ac = col[:, None] * freqs
    cos_t = jnp.concatenate([jnp.cos(ar), jnp.cos(ar), jnp.cos(ac), jnp.cos(ac)], axis=-1)
    sin_t = jnp.concatenate([-jnp.sin(ar), jnp.sin(ar), -jnp.sin(ac), jnp.sin(ac)], axis=-1)
    return cos_t, sin_t


def _encoder_layer(x, mod, p):
    b, t, d = x.shape
    n = b * t
    cos_t, sin_t = _rope_tables(t)
    aq, ak, av, mq, mk, mv, mo, mg = _in_projection(
        x, mod, p["norm1_w"], p["w_main"], p["w_gates"], p["q_norm_w"], p["k_norm_w"], cos_t, sin_t)
    att = _attention(aq, ak, av)
    mg_t = jnp.swapaxes(mg[:, :, :C_MG], 1, 2)
    h_fw, h_bw = _mlstm(mq, mk, mv, mg, mg_t, p["gate_bias_col"], p["gate_bias_row"])
    x1, h2, logits_t = _mix_and_router(x, att, h_fw, h_bw, mo, p["ml_norm_w"], p["w_out"], mod,
                                       p["norm2_w"], p["wr_hi"], p["wr_lo"], p["b_router"])
    idx, wts, counts = _route(logits_t)

    tm = TM_MOE
    counts = counts[:, 0]
    padded = (counts + tm - 1) // tm * tm
    pend = jnp.cumsum(padded)
    pstart = (pend - padded).astype(I32)
    nb = (2 * n + tm - 1) // tm + N_EXPERTS
    m_pad = nb * tm
    blk_e = jnp.minimum(jnp.searchsorted(pend, jnp.arange(nb, dtype=I32) * tm, side="right"), N_EXPERTS - 1).astype(I32)
    nused = (pend[-1:] // tm).astype(I32)

    xs = _dispatch(h2.reshape(n, d), idx, pstart, m_pad)
    o_rows = _moe_experts(xs, blk_e, nused, p["w_gate"], p["w_up"], p["w_down"])
    w_cols = jnp.swapaxes(wts[:, :2, :], 1, 2).reshape(n, 2)
    y = _combine(x1.reshape(n, d), idx, w_cols, mod, o_rows, pstart)
    return y.reshape(b, t, d)


def kernel(x_prompt, x_sample, c_prompt, c_sample, norm1_w, norm2_w, w_ada, b_ada, w_in, q_norm_w, k_norm_w, b_igate, b_fgate, ml_norm_w, w_out, w_gr, b_gr, w_er, b_er, w_gate, w_up, w_down):
    depth = norm1_w.shape[0]
    d = x_prompt.shape[-1]
    bp = x_prompt.shape[0]
    bs = x_sample.shape[0]
    rows = -(-(bp + bs) // 8) * 8
    y_prompt, y_sample = x_prompt, x_sample
    for l in range(depth):
        c_pad = jnp.zeros((rows, d), F32).at[:bp].set(c_prompt).at[bp:bp + bs].set(c_sample)
        mod = _ada_modulation(c_pad, w_ada[l], b_ada[l]).reshape(rows, 6, d)
        gate_bias = jnp.concatenate([b_igate[l].reshape(-1), b_fgate[l].reshape(-1)])
        w_router = jnp.concatenate([w_gr[l], w_er[l]], axis=1).T
        w_router = jnp.zeros((ROUTER_ROWS, d), F32).at[:N_GROUPS + N_EXPERTS].set(w_router)
        wr_hi = w_router.astype(BF16)
        b_router = jnp.zeros((ROUTER_ROWS,), F32).at[:N_GROUPS + N_EXPERTS].set(jnp.concatenate([b_gr[l], b_er[l]]))
        p = {
            "norm1_w": norm1_w[l].reshape(1, d),
            "norm2_w": norm2_w[l].reshape(1, d),
            "w_main": w_in[l][:, :C_MAIN].astype(BF16),
            "w_gates": jnp.zeros((d, LANES), BF16).at[:, :C_MG].set(w_in[l][:, C_MAIN:].astype(BF16)),
            "q_norm_w": q_norm_w[l].reshape(1, HEAD_DIM),
            "k_norm_w": k_norm_w[l].reshape(1, HEAD_DIM),
            "gate_bias_col": jnp.zeros((1, LANES), F32).at[0, :C_MG].set(gate_bias),
            "gate_bias_row": jnp.broadcast_to(gate_bias[:, None], (C_MG, ML_CHUNK)),
            "ml_norm_w": ml_norm_w[l].reshape(1, C_MV),
            "w_out": w_out[l].astype(BF16),
            "wr_hi": wr_hi,
            "wr_lo": (w_router - wr_hi.astype(F32)).astype(BF16),
            "b_router": jnp.broadcast_to(b_router[:, None], (ROUTER_ROWS, LANES)),
            "w_gate": w_gate[l].astype(BF16),
            "w_up": w_up[l].astype(BF16),
            "w_down": w_down[l].astype(BF16),
        }
        y_prompt = _encoder_layer(y_prompt, mod[:bp], p)
        y_sample = _encoder_layer(y_sample, mod[bp:bp + bs], p)
    return (y_prompt, y_sample)
```

```python
import functools

import jax
import jax.numpy as jnp
import numpy as np
from jax import lax
from jax.experimental import pallas as pl
from jax.experimental.pallas import tpu as pltpu

F32 = jnp.float32
BF16 = jnp.bfloat16
I32 = jnp.int32

GRID_W = 64
HEAD_DIM = 128
ATT_HEADS = 8
ATT_KV_HEADS = 2
ATT_GROUP = ATT_HEADS // ATT_KV_HEADS
ROPE_THETA = 10000.0
ROPE_PAIRS = HEAD_DIM // 4
ML_HEADS = 4
ML_DV = 256
ML_DQK = 128
ML_CHUNK = 128
N_GROUPS = 4
EXPERTS_PER_GROUP = 4
N_EXPERTS = N_GROUPS * EXPERTS_PER_GROUP
EPS = 1e-6
Q_SCALE = HEAD_DIM ** -0.5 * float(np.log2(np.e))

C_AQ = ATT_HEADS * HEAD_DIM
C_AK = ATT_KV_HEADS * HEAD_DIM
C_AV = ATT_KV_HEADS * HEAD_DIM
C_MQ = ML_HEADS * ML_DQK
C_MK = ML_HEADS * ML_DQK
C_MV = ML_HEADS * ML_DV
C_MO = ML_HEADS * ML_DV
C_MG = 4 * ML_HEADS
C_MAIN = C_AQ + C_AK + C_AV + C_MQ + C_MK + C_MV + C_MO

LANES = 128
ROUTER_ROWS = 32
MIB = 1024 * 1024

TM_PROJ = 256
TQ_ATT = 256
TM_MIX = 256
TL_ROUTE = 512
TD_DISPATCH = 256
TM_MOE = 256
TC_COMBINE = 256
ATT_SLOTS = 4
DMA_UNROLL = 8


def _params(semantics, vmem_mib):
    return pltpu.CompilerParams(dimension_semantics=semantics, vmem_limit_bytes=vmem_mib * MIB)


def _resident(shape):
    nd = len(shape)
    return pl.BlockSpec(shape, lambda *_: (0,) * nd, pipeline_mode=pl.Buffered(1))


def _ada_kernel(c_ref, w_ref, b_ref, o_ref):
    c = c_ref[...]
    s = (c * jax.nn.sigmoid(c)).astype(BF16)
    o_ref[...] = jnp.dot(s, w_ref[...].astype(BF16), preferred_element_type=F32) + b_ref[...]


def _ada_modulation(c_pad, w_ada, b_ada):
    rows, d = c_pad.shape
    n = w_ada.shape[1]
    tn = 1024
    return pl.pallas_call(
        _ada_kernel,
        grid=(n // tn,),
        in_specs=[pl.BlockSpec((rows, d), lambda j: (0, 0)),
                  pl.BlockSpec((d, tn), lambda j: (0, j)),
                  pl.BlockSpec((1, tn), lambda j: (0, j))],
        out_specs=pl.BlockSpec((rows, tn), lambda j: (0, j)),
        out_shape=jax.ShapeDtypeStruct((rows, n), F32),
        compiler_params=_params(("arbitrary",), 40),
        name="ada_modulation",
    )(c_pad, w_ada, b_ada.reshape(1, n))


def _rms(x, w):
    return x * lax.rsqrt(jnp.mean(x * x, axis=-1, keepdims=True) + EPS) * w


def _inproj_kernel(x_ref, mod_ref, n1_ref, w_ref, wvt_ref, wg_ref, qn_ref, kn_ref, cos_ref, sin_ref,
                   aq_ref, ak_ref, avt_ref, mq_ref, mk_ref, mv_ref, mo_ref, mg_ref):
    x = x_ref[0]
    h = _rms(x, n1_ref[...]) * (1.0 + mod_ref[0, 1:2, :]) + mod_ref[0, 0:1, :]
    hb = h.astype(BF16)

    def proj(c0, width):
        return jnp.dot(hb, w_ref[:, c0:c0 + width], preferred_element_type=F32)

    cos = cos_ref[...]
    sin = sin_ref[...]
    lane = lax.broadcasted_iota(I32, (1, HEAD_DIM), 1)
    first = (lane % (2 * ROPE_PAIRS)) < ROPE_PAIRS

    def norm_rope(p, w):
        pn = _rms(p, w)
        partner = jnp.where(first, pltpu.roll(pn, HEAD_DIM - ROPE_PAIRS, 1), pltpu.roll(pn, ROPE_PAIRS, 1))
        return pn * cos + partner * sin

    c0 = 0
    for half in range(2):
        p = proj(c0, C_AQ // 2)
        for hh in range(ATT_HEADS // 2):
            col = half * (C_AQ // 2) + hh * HEAD_DIM
            qh = norm_rope(p[:, hh * HEAD_DIM:(hh + 1) * HEAD_DIM], qn_ref[...])
            aq_ref[0, :, col:col + HEAD_DIM] = (qh * Q_SCALE).astype(BF16)
        c0 += C_AQ // 2
    p = proj(c0, C_AK)
    for hh in range(ATT_KV_HEADS):
        ak_ref[0, :, hh * HEAD_DIM:(hh + 1) * HEAD_DIM] = norm_rope(p[:, hh * HEAD_DIM:(hh + 1) * HEAD_DIM], kn_ref[...]).astype(BF16)
    avt_ref[0] = lax.dot_general(wvt_ref[...], hb, (((1,), (1,)), ((), ())), preferred_element_type=F32).astype(BF16)
    c0 += C_AK + C_AV
    mq_ref[0] = (proj(c0, C_MQ) * (ML_DQK ** -0.5)).astype(BF16)
    c0 += C_MQ
    mk_ref[0] = proj(c0, C_MK).astype(BF16)
    c0 += C_MK
    for half in range(2):
        mv_ref[0, :, half * 512:(half + 1) * 512] = proj(c0, 512).astype(BF16)
        c0 += 512
    for half in range(2):
        mo_ref[0, :, half * 512:(half + 1) * 512] = proj(c0, 512)
        c0 += 512
    mg_ref[0] = jnp.dot(hb, wg_ref[...], preferred_element_type=F32)


def _in_projection(x, mod, norm1_w, w_main, w_v_t, w_gates, q_norm_w, k_norm_w, cos_t, sin_t):
    b, t, d = x.shape
    tm = min(TM_PROJ, t)
    row = lambda width: pl.BlockSpec((1, tm, width), lambda bi, i: (bi, i, 0))
    widths = (C_AQ, C_AK, None, C_MQ, C_MK, C_MV, C_MO, LANES)
    dtypes = (BF16, BF16, BF16, BF16, BF16, BF16, F32, F32)
    v_t_spec = pl.BlockSpec((1, C_AV, tm), lambda bi, i: (bi, 0, i))
    return pl.pallas_call(
        _inproj_kernel,
        grid=(b, t // tm),
        in_specs=[row(d),
                  pl.BlockSpec((1, 6, d), lambda bi, i: (bi, 0, 0)),
                  _resident((1, d)),
                  _resident((d, C_MAIN)),
                  _resident((C_AV, d)),
                  _resident((d, LANES)),
                  _resident((1, HEAD_DIM)),
                  _resident((1, HEAD_DIM)),
                  pl.BlockSpec((tm, HEAD_DIM), lambda bi, i: (i, 0)),
                  pl.BlockSpec((tm, HEAD_DIM), lambda bi, i: (i, 0))],
        out_specs=[v_t_spec if w is None else row(w) for w in widths],
        out_shape=[jax.ShapeDtypeStruct((b, C_AV, t) if w is None else (b, t, w), dt) for w, dt in zip(widths, dtypes)],
        compiler_params=_params(("parallel", "arbitrary"), 48),
        name="in_projection",
    )(x, mod, norm1_w, w_main, w_v_t, w_gates, q_norm_w, k_norm_w, cos_t, sin_t)


def _attention_kernel(q_ref, k_ref, vt_ref, o_ref, s_ref, p_ref, l_ref):
    def scores(g):
        kv = g // ATT_GROUP
        q = q_ref[0, :, g * HEAD_DIM:(g + 1) * HEAD_DIM]
        k = k_ref[0, :, kv * HEAD_DIM:(kv + 1) * HEAD_DIM]
        s_ref[g % ATT_SLOTS] = lax.dot_general(k, q, (((1,), (1,)), ((), ())), preferred_element_type=F32)

    def softmax(g):
        s = s_ref[g % ATT_SLOTS]
        m = jnp.max(s, axis=0, keepdims=True)
        p = jnp.exp2(s - m)
        l_ref[g] = jnp.broadcast_to(jnp.sum(p, axis=0, keepdims=True), l_ref.shape[1:])
        p_ref[g % ATT_SLOTS] = p.astype(BF16)

    def values(g):
        kv = g // ATT_GROUP
        vt = vt_ref[0, kv * HEAD_DIM:(kv + 1) * HEAD_DIM, :]
        o_t = jnp.dot(vt, p_ref[g % ATT_SLOTS], preferred_element_type=F32)
        o_ref[0, :, g * HEAD_DIM:(g + 1) * HEAD_DIM] = (o_t / l_ref[g][0:1, :]).T.astype(BF16)

    n = ATT_HEADS
    for step in range(n + 2):
        if step < n:
            scores(step)
        if 1 <= step <= n:
            softmax(step - 1)
        if step >= 2:
            values(step - 2)


def _attention(aq, ak, av_t):
    b, t, _ = aq.shape
    tq = min(TQ_ATT, t)
    return pl.pallas_call(
        _attention_kernel,
        grid=(b, t // tq),
        in_specs=[pl.BlockSpec((1, tq, C_AQ), lambda bi, i: (bi, i, 0)),
                  pl.BlockSpec((1, t, C_AK), lambda bi, i: (bi, 0, 0)),
                  pl.BlockSpec((1, C_AV, t), lambda bi, i: (bi, 0, 0))],
        out_specs=pl.BlockSpec((1, tq, C_AQ), lambda bi, i: (bi, i, 0)),
        out_shape=jax.ShapeDtypeStruct((b, t, C_AQ), BF16),
        scratch_shapes=[pltpu.VMEM((ATT_SLOTS, t, tq), F32), pltpu.VMEM((ATT_SLOTS, t, tq), BF16),
                        pltpu.VMEM((ATT_HEADS, 8, tq), F32)],
        compiler_params=_params(("parallel", "arbitrary"), 48),
        name="attention",
    )(aq, ak, av_t)


def _log_sigmoid(x):
    return jnp.minimum(x, 0.0) - jnp.log1p(jnp.exp(-jnp.abs(x)))


def _split3(a):
    a1 = a.astype(BF16)
    r1 = a - a1.astype(F32)
    a2 = r1.astype(BF16)
    a3 = (r1 - a2.astype(F32)).astype(BF16)
    return a1, a2, a3


def _lane_pick(a, idx):
    lane = lax.broadcasted_iota(I32, a.shape, 1)
    return jnp.sum(jnp.where(lane == idx, a, 0.0), axis=-1, keepdims=True)


def _mlstm_direction(reverse, q_ref, k_ref, v_ref, gc_ref, gr_ref, bc_ref, br_ref, h_ref,
                     ct_ref, n_ref, m_ref):
    L = ML_CHUNK
    d = 1 if reverse else 0
    row = lax.broadcasted_iota(I32, (L, L), 0)
    col = lax.broadcasted_iota(I32, (L, L), 1)
    lower = (col <= row)
    upper = (col >= row)
    lower_b = lower.astype(BF16)
    upper_b = upper.astype(BF16)
    cum_cols = upper_b if reverse else lower_b
    cum_rows = lower_b if reverse else upper_b
    mask = upper if reverse else lower

    pre_c = gc_ref[0] + bc_ref[...]
    pre_r = gr_ref[0] + br_ref[...]
    b_cols = sum(jnp.dot(cum_cols, part, preferred_element_type=F32) for part in _split3(_log_sigmoid(pre_c)))
    b_rows = sum(jnp.dot(part, cum_rows, preferred_element_type=F32) for part in _split3(_log_sigmoid(pre_r)))
    end = 0 if reverse else L - 1

    for hh in range(ML_HEADS):
        ci = d * ML_HEADS + hh
        cf = 2 * ML_HEADS + d * ML_HEADS + hh
        sidx = d * ML_HEADS + hh
        i_row = pre_r[ci:ci + 1, :]
        b_row = b_rows[cf:cf + 1, :]
        b_col = _lane_pick(b_cols, cf)
        m = m_ref[sidx][0:1, 0:1]
        dmat = jnp.where(mask, b_col - b_row + i_row, -jnp.inf)
        m_inter = b_col + m
        m_loc = jnp.maximum(m_inter, jnp.max(dmat, axis=-1, keepdims=True))
        a_inter = jnp.exp(m_inter - m_loc)
        q = q_ref[0, :, hh * ML_DQK:(hh + 1) * ML_DQK]
        k = k_ref[0, :, hh * ML_DQK:(hh + 1) * ML_DQK]
        v = v_ref[0, :, hh * ML_DV:(hh + 1) * ML_DV]
        qk = lax.dot_general(q, k, (((1,), (1,)), ((), ())), preferred_element_type=F32)
        s = qk * jnp.exp(dmat - m_loc)
        ct = ct_ref[sidx]
        n_row = n_ref[sidx]
        num = a_inter * jnp.dot(q, ct.astype(BF16), preferred_element_type=F32) \
            + jnp.dot(s.astype(BF16), v, preferred_element_type=F32)
        qn = jnp.sum(q.astype(F32) * n_row, axis=-1, keepdims=True)
        den = a_inter * qn + jnp.sum(s, axis=-1, keepdims=True)
        h_ref[0, :, hh * ML_DV:(hh + 1) * ML_DV] = num / jnp.maximum(jnp.abs(den), jnp.exp(-m_loc))

        b_end = _lane_pick(b_row, end)
        g_row = b_end - b_row + i_row
        m_new = jnp.maximum(b_end + m, jnp.max(g_row, axis=-1, keepdims=True))
        decay = jnp.exp(b_end + m - m_new)
        w_row = jnp.exp(g_row - m_new)
        ktw = (k.astype(F32).T * w_row).astype(BF16)
        ct_ref[sidx] = decay * ct + jnp.dot(ktw, v, preferred_element_type=F32)
        wk = jnp.dot(jnp.broadcast_to(w_row.astype(BF16), (8, L)), k, preferred_element_type=F32)
        n_ref[sidx] = decay * n_row + wk[0:1, :]
        m_ref[sidx] = jnp.broadcast_to(m_new, (8, LANES))


def _mlstm_kernel(qf, kf, vf, gcf, grf, qb, kb, vb, gcb, grb, bc_ref, br_ref, hf_ref, hb_ref,
                  ct_ref, n_ref, m_ref):
    @pl.when(pl.program_id(1) == 0)
    def _():
        ct_ref[...] = jnp.zeros_like(ct_ref)
        n_ref[...] = jnp.zeros_like(n_ref)
        m_ref[...] = jnp.zeros_like(m_ref)

    _mlstm_direction(False, qf, kf, vf, gcf, grf, bc_ref, br_ref, hf_ref, ct_ref, n_ref, m_ref)
    _mlstm_direction(True, qb, kb, vb, gcb, grb, bc_ref, br_ref, hb_ref, ct_ref, n_ref, m_ref)


def _mlstm(mq, mk, mv, mg, mg_t, bias_col, bias_row):
    b, t, _ = mq.shape
    L = ML_CHUNK
    nc = t // L
    fw = lambda width: pl.BlockSpec((1, L, width), lambda bi, c: (bi, c, 0))
    bw = lambda width: pl.BlockSpec((1, L, width), lambda bi, c: (bi, nc - 1 - c, 0))
    fw_t = pl.BlockSpec((1, C_MG, L), lambda bi, c: (bi, 0, c))
    bw_t = pl.BlockSpec((1, C_MG, L), lambda bi, c: (bi, 0, nc - 1 - c))
    ns = 2 * ML_HEADS
    return pl.pallas_call(
        _mlstm_kernel,
        grid=(b, nc),
        in_specs=[fw(C_MQ), fw(C_MK), fw(C_MV), fw(LANES), fw_t,
                  bw(C_MQ), bw(C_MK), bw(C_MV), bw(LANES), bw_t,
                  _resident((1, LANES)), _resident((C_MG, L))],
        out_specs=[fw(C_MV), bw(C_MV)],
        out_shape=[jax.ShapeDtypeStruct((b, t, C_MV), F32)] * 2,
        scratch_shapes=[pltpu.VMEM((ns, ML_DQK, ML_DV), F32),
                        pltpu.VMEM((ns, 1, ML_DQK), F32),
                        pltpu.VMEM((ns, 8, LANES), F32)],
        compiler_params=_params(("parallel", "arbitrary"), 32),
        name="mlstm",
    )(mq, mk, mv, mg, mg_t, mq, mk, mv, mg, mg_t, bias_col, bias_row)


def _mix_kernel(x_ref, att_ref, hf_ref, hb_ref, mo_ref, mlw_ref, wo_ref, mod_ref, n2_ref,
                wrh_ref, wrl_ref, br_ref, x1_ref, h2_ref, lg_ref):
    ml = hf_ref[0] + hb_ref[0]
    gate = jax.nn.sigmoid(mo_ref[0])
    mix = jnp.dot(att_ref[0], wo_ref[0:C_AQ, :], preferred_element_type=F32)
    for hh in range(ML_HEADS):
        sl = slice(hh * ML_DV, (hh + 1) * ML_DV)
        seg = (_rms(ml[:, sl], mlw_ref[:, sl]) * gate[:, sl]).astype(BF16)
        mix += jnp.dot(seg, wo_ref[C_AQ + hh * ML_DV:C_AQ + (hh + 1) * ML_DV, :], preferred_element_type=F32)
    x1 = x_ref[0] + mod_ref[0, 2:3, :] * mix
    x1_ref[0] = x1
    h2 = _rms(x1, n2_ref[...]) * (1.0 + mod_ref[0, 4:5, :]) + mod_ref[0, 3:4, :]
    h2_ref[0] = h2
    h_hi = h2.astype(BF16)
    h_lo = (h2 - h_hi.astype(F32)).astype(BF16)
    nt = (((1,), (1,)), ((), ()))
    lg = lax.dot_general(wrh_ref[...], h_hi, nt, preferred_element_type=F32)
    lg += lax.dot_general(wrl_ref[...], h_hi, nt, preferred_element_type=F32)
    lg += lax.dot_general(wrh_ref[...], h_lo, nt, preferred_element_type=F32)
    lg_ref[0] = lg + br_ref[:, 0:1]


def _mix_and_router(x, att, h_fw, h_bw, mo, ml_norm_w, w_out, mod, norm2_w, wr_hi, wr_lo, b_router):
    b, t, d = x.shape
    tm = min(TM_MIX, t)
    row = lambda width: pl.BlockSpec((1, tm, width), lambda bi, i: (bi, i, 0))
    return pl.pallas_call(
        _mix_kernel,
        grid=(b, t // tm),
        in_specs=[row(d), row(C_AQ), row(C_MV), row(C_MV), row(C_MO),
                  _resident((1, C_MV)), _resident((d, d)),
                  pl.BlockSpec((1, 6, d), lambda bi, i: (bi, 0, 0)),
                  _resident((1, d)), _resident((ROUTER_ROWS, d)), _resident((ROUTER_ROWS, d)),
                  _resident((ROUTER_ROWS, LANES))],
        out_specs=[row(d), row(d), pl.BlockSpec((1, ROUTER_ROWS, tm), lambda bi, i: (bi, 0, i))],
        out_shape=[jax.ShapeDtypeStruct((b, t, d), F32), jax.ShapeDtypeStruct((b, t, d), F32),
                   jax.ShapeDtypeStruct((b, ROUTER_ROWS, t), F32)],
        compiler_params=_params(("parallel", "arbitrary"), 48),
        name="mix_router",
    )(x, att, h_fw, h_bw, mo, ml_norm_w, w_out, mod, norm2_w, wr_hi, wr_lo, b_router)


def _first_argmax(rows):
    best = rows[0]
    idx = jnp.zeros_like(best)
    for j in range(1, len(rows)):
        better = rows[j] > best
        best = jnp.where(better, rows[j], best)
        idx = jnp.where(better, float(j), idx)
    return best, idx


def _softmax_rows(rows):
    mx = functools.reduce(jnp.maximum, rows)
    ex = [jnp.exp(r - mx) for r in rows]
    tot = functools.reduce(lambda a, c: a + c, ex)
    return [e / tot for e in ex]


def _route_kernel(lg_ref, idx_ref, wt_ref, cnt_ref, run_ref):
    @pl.when((pl.program_id(0) == 0) & (pl.program_id(1) == 0))
    def _():
        run_ref[...] = jnp.zeros_like(run_ref)

    tl = lg_ref.shape[2]
    lg = lg_ref[0]
    p_grp = _softmax_rows([lg[g:g + 1, :] for g in range(N_GROUPS)])
    p_g, g_idx = _first_argmax(p_grp)
    el = []
    for j in range(EXPERTS_PER_GROUP):
        sel = lg[N_GROUPS + (N_GROUPS - 1) * EXPERTS_PER_GROUP + j:N_GROUPS + (N_GROUPS - 1) * EXPERTS_PER_GROUP + j + 1, :]
        for g in range(N_GROUPS - 2, -1, -1):
            r = N_GROUPS + g * EXPERTS_PER_GROUP + j
            sel = jnp.where(g_idx == float(g), lg[r:r + 1, :], sel)
        el.append(sel)
    pe = _softmax_rows(el)
    w1, i1 = _first_argmax(pe)
    rest = [jnp.where(i1 == float(j), -1.0, pe[j]) for j in range(EXPERTS_PER_GROUP)]
    w2, i2 = _first_argmax(rest)
    tot = w1 + w2
    wt1 = w1 / tot * p_g
    wt2 = w2 / tot * p_g
    e1 = (g_idx * float(EXPERTS_PER_GROUP) + i1).astype(I32)
    e2 = (g_idx * float(EXPERTS_PER_GROUP) + i2).astype(I32)

    erow = lax.broadcasted_iota(I32, (N_EXPERTS, tl), 0)
    hit1 = erow == e1
    hit2 = erow == e2
    onehot = (hit1 | hit2).astype(F32)
    before = lax.broadcasted_iota(I32, (tl, tl), 0) <= lax.broadcasted_iota(I32, (tl, tl), 1)
    incl = jnp.dot(onehot.astype(BF16), before.astype(BF16), preferred_element_type=F32)
    base = run_ref[:, 0:1] + incl - onehot
    r1 = jnp.sum(jnp.where(hit1, base, 0.0), axis=0, keepdims=True).astype(I32)
    r2 = jnp.sum(jnp.where(hit2, base, 0.0), axis=0, keepdims=True).astype(I32)
    total = run_ref[...] + jnp.sum(onehot, axis=-1, keepdims=True)
    run_ref[...] = total
    cnt_ref[...] = total.astype(I32)

    zi = jnp.zeros((4, tl), I32)
    idx_ref[0] = jnp.concatenate([e1, e2, r1, r2, zi], axis=0)
    wt_ref[0] = jnp.concatenate([wt1, wt2, jnp.zeros((6, tl), F32)], axis=0)


def _route(logits_t):
    b, _, t = logits_t.shape
    tl = min(TL_ROUTE, t)
    blk = lambda rows: pl.BlockSpec((1, rows, tl), lambda bi, i: (bi, 0, i))
    return pl.pallas_call(
        _route_kernel,
        grid=(b, t // tl),
        in_specs=[blk(ROUTER_ROWS)],
        out_specs=[blk(8), blk(8), pl.BlockSpec((N_EXPERTS, LANES), lambda bi, i: (0, 0))],
        out_shape=[jax.ShapeDtypeStruct((b, 8, t), I32), jax.ShapeDtypeStruct((b, 8, t), F32),
                   jax.ShapeDtypeStruct((N_EXPERTS, LANES), I32)],
        scratch_shapes=[pltpu.VMEM((N_EXPERTS, LANES), F32)],
        compiler_params=_params(("arbitrary", "arbitrary"), 32),
        name="route",
    )(logits_t)


def _row_copy(src_ref, src_row, dst_ref, dst_row, sem):
    return pltpu.make_async_copy(src_ref.at[pl.ds(src_row, 1), :], dst_ref.at[pl.ds(dst_row, 1), :], sem)


def _dispatch_kernel(pstart_ref, padlo_ref, padhi_ref, idx_ref, h_ref, xs_ref, zero_ref, sem):
    td = h_ref.shape[0]

    def copies(t):
        out = []
        for k in range(2):
            dest = pstart_ref[idx_ref[0, k, t]] + idx_ref[0, 2 + k, t]
            out.append(_row_copy(h_ref, t, xs_ref, dest, sem))
        return out

    def start(t, carry):
        for c in copies(t):
            c.start()
        return carry

    def wait(t, carry):
        for c in copies(t):
            c.wait()
        return carry

    lax.fori_loop(0, td, start, 0, unroll=DMA_UNROLL)
    lax.fori_loop(0, td, wait, 0, unroll=DMA_UNROLL)

    @pl.when(pl.program_id(0) == pl.num_programs(0) - 1)
    def _():
        zero_ref[...] = jnp.zeros_like(zero_ref)

        def zero_start(r, carry):
            _row_copy(zero_ref, 0, xs_ref, r, sem).start()
            return carry

        def zero_wait(r, carry):
            _row_copy(zero_ref, 0, xs_ref, r, sem).wait()
            return carry

        for e in range(N_EXPERTS):
            lax.fori_loop(padlo_ref[e], padhi_ref[e], zero_start, 0)
        for e in range(N_EXPERTS):
            lax.fori_loop(padlo_ref[e], padhi_ref[e], zero_wait, 0)


def _dispatch(h2, idx, pstart, pad_lo, pad_hi, m_pad):
    n, d = h2.shape
    b, _, t = idx.shape
    td = min(TD_DISPATCH, t)
    per = t // td
    return pl.pallas_call(
        _dispatch_kernel,
        grid_spec=pltpu.PrefetchScalarGridSpec(
            num_scalar_prefetch=3,
            grid=(n // td,),
            in_specs=[pl.BlockSpec((1, 8, td), lambda i, *_: (i // per, 0, i % per), memory_space=pltpu.SMEM),
                      pl.BlockSpec((td, d), lambda i, *_: (i, 0))],
            out_specs=pl.BlockSpec(memory_space=pl.ANY),
            scratch_shapes=[pltpu.VMEM((8, d), F32), pltpu.SemaphoreType.DMA(())]),
        out_shape=jax.ShapeDtypeStruct((m_pad, d), F32),
        compiler_params=_params(("arbitrary",), 32),
        name="dispatch",
    )(pstart, pad_lo, pad_hi, idx, h2)


def _moe_kernel(blk_e_ref, nused_ref, x_ref, wg_ref, wu_ref, wd_ref, o_ref):
    r = pl.program_id(0)

    @pl.when(r < nused_ref[0])
    def _():
        xb = x_ref[...].astype(BF16)
        g = jnp.dot(xb, wg_ref[0], preferred_element_type=F32)
        u = jnp.dot(xb, wu_ref[0], preferred_element_type=F32)
        h = (g * jax.nn.sigmoid(g) * u).astype(BF16)
        o_ref[...] = jnp.dot(h, wd_ref[0], preferred_element_type=F32)

    @pl.when(r >= nused_ref[0])
    def _():
        o_ref[...] = jnp.zeros_like(o_ref)


def _moe_experts(xs, blk_e, nused, w_gate, w_up, w_down):
    m_pad, d = xs.shape
    f = w_gate.shape[2]
    tm = TM_MOE
    last = lambda r, nu: jnp.minimum(r, nu[0] - 1)
    return pl.pallas_call(
        _moe_kernel,
        grid_spec=pltpu.PrefetchScalarGridSpec(
            num_scalar_prefetch=2,
            grid=(m_pad // tm,),
            in_specs=[pl.BlockSpec((tm, d), lambda r, be, nu: (last(r, nu), 0)),
                      pl.BlockSpec((1, d, f), lambda r, be, nu: (be[last(r, nu)], 0, 0)),
                      pl.BlockSpec((1, d, f), lambda r, be, nu: (be[last(r, nu)], 0, 0)),
                      pl.BlockSpec((1, f, d), lambda r, be, nu: (be[last(r, nu)], 0, 0))],
            out_specs=pl.BlockSpec((tm, d), lambda r, be, nu: (r, 0))),
        out_shape=jax.ShapeDtypeStruct((m_pad, d), F32),
        compiler_params=_params(("arbitrary",), 56),
        name="moe_experts",
    )(blk_e, nused, xs, w_gate, w_up, w_down)


def _combine_kernel(pstart_ref, idx_ref, x1_ref, w_ref, mod_ref, o_ref, y_ref, g0_ref, g1_ref, sem):
    tc = x1_ref.shape[0]
    bufs = (g0_ref, g1_ref)

    def copies(t):
        out = []
        for k in range(2):
            src = pstart_ref[idx_ref[0, k, t]] + idx_ref[0, 2 + k, t]
            out.append(_row_copy(o_ref, src, bufs[k], t, sem))
        return out

    def start(t, carry):
        for c in copies(t):
            c.start()
        return carry

    def wait(t, carry):
        for c in copies(t):
            c.wait()
        return carry

    lax.fori_loop(0, tc, start, 0, unroll=DMA_UNROLL)
    lax.fori_loop(0, tc, wait, 0, unroll=DMA_UNROLL)
    moe = g0_ref[...] * w_ref[:, 0:1] + g1_ref[...] * w_ref[:, 1:2]
    y_ref[...] = x1_ref[...] + mod_ref[0, 5:6, :] * moe


def _combine(x1, idx, w_cols, mod, o_rows, pstart):
    n, d = x1.shape
    b, _, t = idx.shape
    tc = min(TC_COMBINE, t)
    per = t // tc
    return pl.pallas_call(
        _combine_kernel,
        grid_spec=pltpu.PrefetchScalarGridSpec(
            num_scalar_prefetch=1,
            grid=(n // tc,),
            in_specs=[pl.BlockSpec((1, 8, tc), lambda i, ps: (i // per, 0, i % per), memory_space=pltpu.SMEM),
                      pl.BlockSpec((tc, d), lambda i, ps: (i, 0)),
                      pl.BlockSpec((tc, 2), lambda i, ps: (i, 0)),
                      pl.BlockSpec((1, 6, d), lambda i, ps: (i // per, 0, 0)),
                      pl.BlockSpec(memory_space=pl.ANY)],
            out_specs=pl.BlockSpec((tc, d), lambda i, ps: (i, 0)),
            scratch_shapes=[pltpu.VMEM((tc, d), F32), pltpu.VMEM((tc, d), F32), pltpu.SemaphoreType.DMA(())]),
        out_shape=jax.ShapeDtypeStruct((n, d), F32),
        compiler_params=_params(("arbitrary",), 32),
        name="combine",
    )(pstart, idx, x1, w_cols, mod, o_rows)


def _rope_tables(t):
    rows = t // GRID_W
    row = jnp.repeat(jnp.arange(rows, dtype=F32), GRID_W)
    col = jnp.tile(jnp.arange(GRID_W, dtype=F32), rows)
    freqs = ROPE_THETA ** (-jnp.arange(ROPE_PAIRS, dtype=F32) / ROPE_PAIRS)
    ar = row[:, None] * freqs
    ac = col[:, None] * freqs
    cos_t = jnp.concatenate([jnp.cos(ar), jnp.cos(ar), jnp.cos(ac), jnp.cos(ac)], axis=-1)
    sin_t = jnp.concatenate([-jnp.sin(ar), jnp.sin(ar), -jnp.sin(ac), jnp.sin(ac)], axis=-1)
    return cos_t, sin_t


def _encoder_layer(x, mod, p):
    b, t, d = x.shape
    n = b * t
    cos_t, sin_t = _rope_tables(t)
    aq, ak, av_t, mq, mk, mv, mo, mg = _in_projection(
        x, mod, p["norm1_w"], p["w_main"], p["w_v_t"], p["w_gates"], p["q_norm_w"], p["k_norm_w"], cos_t, sin_t)
    att = _attention(aq, ak, av_t)
    mg_t = jnp.swapaxes(mg[:, :, :C_MG], 1, 2)
    h_fw, h_bw = _mlstm(mq, mk, mv, mg, mg_t, p["gate_bias_col"], p["gate_bias_row"])
    x1, h2, logits_t = _mix_and_router(x, att, h_fw, h_bw, mo, p["ml_norm_w"], p["w_out"], mod,
                                       p["norm2_w"], p["wr_hi"], p["wr_lo"], p["b_router"])
    idx, wts, counts = _route(logits_t)

    tm = TM_MOE
    counts = counts[:, 0]
    padded = (counts + tm - 1) // tm * tm
    pend = jnp.cumsum(padded)
    pstart = (pend - padded).astype(I32)
    nb = (2 * n + tm - 1) // tm + N_EXPERTS
    m_pad = nb * tm
    blk_e = jnp.minimum(jnp.searchsorted(pend, jnp.arange(nb, dtype=I32) * tm, side="right"), N_EXPERTS - 1).astype(I32)
    nused = (pend[-1:] // tm).astype(I32)

    pad_lo = (pstart + counts).astype(I32)
    pad_hi = jnp.concatenate([pstart[1:], jnp.full((1,), m_pad, I32)])
    xs = _dispatch(h2.reshape(n, d), idx, pstart, pad_lo, pad_hi, m_pad)
    o_rows = _moe_experts(xs, blk_e, nused, p["w_gate"], p["w_up"], p["w_down"])
    w_cols = jnp.swapaxes(wts[:, :2, :], 1, 2).reshape(n, 2)
    y = _combine(x1.reshape(n, d), idx, w_cols, mod, o_rows, pstart)
    return y.reshape(b, t, d)


def kernel(x_prompt, x_sample, c_prompt, c_sample, norm1_w, norm2_w, w_ada, b_ada, w_in, q_norm_w, k_norm_w, b_igate, b_fgate, ml_norm_w, w_out, w_gr, b_gr, w_er, b_er, w_gate, w_up, w_down):
    depth = norm1_w.shape[0]
    d = x_prompt.shape[-1]
    bp = x_prompt.shape[0]
    bs = x_sample.shape[0]
    rows = -(-(bp + bs) // 8) * 8
    y_prompt, y_sample = x_prompt, x_sample
    for l in range(depth):
        c_pad = jnp.zeros((rows, d), F32).at[:bp].set(c_prompt).at[bp:bp + bs].set(c_sample)
        mod = _ada_modulation(c_pad, w_ada[l], b_ada[l]).reshape(rows, 6, d)
        gate_bias = jnp.concatenate([b_igate[l].reshape(-1), b_fgate[l].reshape(-1)])
        w_router = jnp.concatenate([w_gr[l], w_er[l]], axis=1).T
        w_router = jnp.zeros((ROUTER_ROWS, d), F32).at[:N_GROUPS + N_EXPERTS].set(w_router)
        wr_hi = w_router.astype(BF16)
        b_router = jnp.zeros((ROUTER_ROWS,), F32).at[:N_GROUPS + N_EXPERTS].set(jnp.concatenate([b_gr[l], b_er[l]]))
        p = {
            "norm1_w": norm1_w[l].reshape(1, d),
            "norm2_w": norm2_w[l].reshape(1, d),
            "w_main": w_in[l][:, :C_MAIN].astype(BF16),
            "w_v_t": w_in[l][:, C_AQ + C_AK:C_AQ + C_AK + C_AV].T.astype(BF16),
            "w_gates": jnp.zeros((d, LANES), BF16).at[:, :C_MG].set(w_in[l][:, C_MAIN:].astype(BF16)),
            "q_norm_w": q_norm_w[l].reshape(1, HEAD_DIM),
            "k_norm_w": k_norm_w[l].reshape(1, HEAD_DIM),
            "gate_bias_col": jnp.zeros((1, LANES), F32).at[0, :C_MG].set(gate_bias),
            "gate_bias_row": jnp.broadcast_to(gate_bias[:, None], (C_MG, ML_CHUNK)),
            "ml_norm_w": ml_norm_w[l].reshape(1, C_MV),
            "w_out": w_out[l].astype(BF16),
            "wr_hi": wr_hi,
            "wr_lo": (w_router - wr_hi.astype(F32)).astype(BF16),
            "b_router": jnp.broadcast_to(b_router[:, None], (ROUTER_ROWS, LANES)),
            "w_gate": w_gate[l].astype(BF16),
            "w_up": w_up[l].astype(BF16),
            "w_down": w_down[l].astype(BF16),
        }
        y_prompt = _encoder_layer(y_prompt, mod[:bp], p)
        y_sample = _encoder_layer(y_sample, mod[bp:bp + bs], p)
    return (y_prompt, y_sample)
```

```python
import functools

import jax
import jax.numpy as jnp
import numpy as np
from jax import lax
from jax.experimental import pallas as pl
from jax.experimental.pallas import tpu as pltpu

F32 = jnp.float32
BF16 = jnp.bfloat16
I32 = jnp.int32

GRID_W = 64
HEAD_DIM = 128
ATT_HEADS = 8
ATT_KV_HEADS = 2
ATT_GROUP = ATT_HEADS // ATT_KV_HEADS
ROPE_THETA = 10000.0
ROPE_PAIRS = HEAD_DIM // 4
ML_HEADS = 4
ML_DV = 256
ML_DQK = 128
ML_CHUNK = 128
N_GROUPS = 4
EXPERTS_PER_GROUP = 4
N_EXPERTS = N_GROUPS * EXPERTS_PER_GROUP
EPS = 1e-6
Q_SCALE = HEAD_DIM ** -0.5 * float(np.log2(np.e))

C_AQ = ATT_HEADS * HEAD_DIM
C_AK = ATT_KV_HEADS * HEAD_DIM
C_AV = ATT_KV_HEADS * HEAD_DIM
C_MQ = ML_HEADS * ML_DQK
C_MK = ML_HEADS * ML_DQK
C_MV = ML_HEADS * ML_DV
C_MO = ML_HEADS * ML_DV
C_MG = 4 * ML_HEADS
C_MAIN = C_AQ + C_AK + C_AV + C_MQ + C_MK + C_MV + C_MO

LANES = 128
ROUTER_ROWS = 32
MIB = 1024 * 1024

TM_PROJ = 256
TQ_ATT = 256
TM_MIX = 256
TL_ROUTE = 512
TD_DISPATCH = 256
TM_MOE = 256
TC_COMBINE = 256
ML_SEQS = 2
ATT_SLOTS = 4
DMA_UNROLL = 8


def _params(semantics, vmem_mib):
    return pltpu.CompilerParams(dimension_semantics=semantics, vmem_limit_bytes=vmem_mib * MIB)


def _resident(shape):
    nd = len(shape)
    return pl.BlockSpec(shape, lambda *_: (0,) * nd, pipeline_mode=pl.Buffered(1))


def _ada_kernel(c_ref, w_ref, b_ref, o_ref):
    c = c_ref[...]
    s = (c * jax.nn.sigmoid(c)).astype(BF16)
    o_ref[...] = jnp.dot(s, w_ref[...].astype(BF16), preferred_element_type=F32) + b_ref[...]


def _ada_modulation(c_pad, w_ada, b_ada):
    rows, d = c_pad.shape
    n = w_ada.shape[1]
    tn = 1024
    return pl.pallas_call(
        _ada_kernel,
        grid=(n // tn,),
        in_specs=[pl.BlockSpec((rows, d), lambda j: (0, 0)),
                  pl.BlockSpec((d, tn), lambda j: (0, j)),
                  pl.BlockSpec((1, tn), lambda j: (0, j))],
        out_specs=pl.BlockSpec((rows, tn), lambda j: (0, j)),
        out_shape=jax.ShapeDtypeStruct((rows, n), F32),
        compiler_params=_params(("arbitrary",), 40),
        name="ada_modulation",
    )(c_pad, w_ada, b_ada.reshape(1, n))


def _rms(x, w):
    return x * lax.rsqrt(jnp.mean(x * x, axis=-1, keepdims=True) + EPS) * w


def _inproj_kernel(x_ref, mod_ref, n1_ref, w_ref, wvt_ref, wg_ref, qn_ref, kn_ref, cos_ref, sin_ref,
                   aq_ref, ak_ref, avt_ref, mq_ref, mk_ref, mv_ref, mo_ref, mg_ref):
    x = x_ref[0]
    h = _rms(x, n1_ref[...]) * (1.0 + mod_ref[0, 1:2, :]) + mod_ref[0, 0:1, :]
    hb = h.astype(BF16)

    def proj(c0, width):
        return jnp.dot(hb, w_ref[:, c0:c0 + width], preferred_element_type=F32)

    cos = cos_ref[...]
    sin = sin_ref[...]
    lane = lax.broadcasted_iota(I32, (1, HEAD_DIM), 1)
    first = (lane % (2 * ROPE_PAIRS)) < ROPE_PAIRS

    def norm_rope(p, w):
        pn = _rms(p, w)
        partner = jnp.where(first, pltpu.roll(pn, HEAD_DIM - ROPE_PAIRS, 1), pltpu.roll(pn, ROPE_PAIRS, 1))
        return pn * cos + partner * sin

    c0 = 0
    for half in range(2):
        p = proj(c0, C_AQ // 2)
        for hh in range(ATT_HEADS // 2):
            col = half * (C_AQ // 2) + hh * HEAD_DIM
            qh = norm_rope(p[:, hh * HEAD_DIM:(hh + 1) * HEAD_DIM], qn_ref[...])
            aq_ref[0, :, col:col + HEAD_DIM] = (qh * Q_SCALE).astype(BF16)
        c0 += C_AQ // 2
    p = proj(c0, C_AK)
    for hh in range(ATT_KV_HEADS):
        ak_ref[0, :, hh * HEAD_DIM:(hh + 1) * HEAD_DIM] = norm_rope(p[:, hh * HEAD_DIM:(hh + 1) * HEAD_DIM], kn_ref[...]).astype(BF16)
    avt_ref[0] = lax.dot_general(wvt_ref[...], hb, (((1,), (1,)), ((), ())), preferred_element_type=F32).astype(BF16)
    c0 += C_AK + C_AV
    mq_ref[0] = (proj(c0, C_MQ) * (ML_DQK ** -0.5)).astype(BF16)
    c0 += C_MQ
    mk_ref[0] = proj(c0, C_MK).astype(BF16)
    c0 += C_MK
    for half in range(2):
        mv_ref[0, :, half * 512:(half + 1) * 512] = proj(c0, 512).astype(BF16)
        c0 += 512
    for half in range(2):
        mo_ref[0, :, half * 512:(half + 1) * 512] = proj(c0, 512)
        c0 += 512
    mg_ref[0] = jnp.dot(hb, wg_ref[...], preferred_element_type=F32)


def _in_projection(x, mod, norm1_w, w_main, w_v_t, w_gates, q_norm_w, k_norm_w, cos_t, sin_t):
    b, t, d = x.shape
    tm = min(TM_PROJ, t)
    row = lambda width: pl.BlockSpec((1, tm, width), lambda bi, i: (bi, i, 0))
    widths = (C_AQ, C_AK, None, C_MQ, C_MK, C_MV, C_MO, LANES)
    dtypes = (BF16, BF16, BF16, BF16, BF16, BF16, F32, F32)
    v_t_spec = pl.BlockSpec((1, C_AV, tm), lambda bi, i: (bi, 0, i))
    return pl.pallas_call(
        _inproj_kernel,
        grid=(b, t // tm),
        in_specs=[row(d),
                  pl.BlockSpec((1, 6, d), lambda bi, i: (bi, 0, 0)),
                  _resident((1, d)),
                  _resident((d, C_MAIN)),
                  _resident((C_AV, d)),
                  _resident((d, LANES)),
                  _resident((1, HEAD_DIM)),
                  _resident((1, HEAD_DIM)),
                  pl.BlockSpec((tm, HEAD_DIM), lambda bi, i: (i, 0)),
                  pl.BlockSpec((tm, HEAD_DIM), lambda bi, i: (i, 0))],
        out_specs=[v_t_spec if w is None else row(w) for w in widths],
        out_shape=[jax.ShapeDtypeStruct((b, C_AV, t) if w is None else (b, t, w), dt) for w, dt in zip(widths, dtypes)],
        compiler_params=_params(("parallel", "arbitrary"), 48),
        name="in_projection",
    )(x, mod, norm1_w, w_main, w_v_t, w_gates, q_norm_w, k_norm_w, cos_t, sin_t)


def _attention_kernel(q_ref, k_ref, vt_ref, o_ref, s_ref, p_ref, l_ref):
    def scores(g):
        kv = g // ATT_GROUP
        q = q_ref[0, :, g * HEAD_DIM:(g + 1) * HEAD_DIM]
        k = k_ref[0, :, kv * HEAD_DIM:(kv + 1) * HEAD_DIM]
        s_ref[g % ATT_SLOTS] = lax.dot_general(k, q, (((1,), (1,)), ((), ())), preferred_element_type=F32)

    def softmax(g):
        s = s_ref[g % ATT_SLOTS]
        m = jnp.max(s, axis=0, keepdims=True)
        p = jnp.exp2(s - m)
        l_ref[g] = jnp.broadcast_to(jnp.sum(p, axis=0, keepdims=True), l_ref.shape[1:])
        p_ref[g % ATT_SLOTS] = p.astype(BF16)

    def values(g):
        kv = g // ATT_GROUP
        vt = vt_ref[0, kv * HEAD_DIM:(kv + 1) * HEAD_DIM, :]
        o_t = jnp.dot(vt, p_ref[g % ATT_SLOTS], preferred_element_type=F32)
        o_ref[0, :, g * HEAD_DIM:(g + 1) * HEAD_DIM] = (o_t / l_ref[g][0:1, :]).T.astype(BF16)

    n = ATT_HEADS
    for step in range(n + 2):
        if step < n:
            scores(step)
        if 1 <= step <= n:
            softmax(step - 1)
        if step >= 2:
            values(step - 2)


def _attention(aq, ak, av_t):
    b, t, _ = aq.shape
    tq = min(TQ_ATT, t)
    return pl.pallas_call(
        _attention_kernel,
        grid=(b, t // tq),
        in_specs=[pl.BlockSpec((1, tq, C_AQ), lambda bi, i: (bi, i, 0)),
                  pl.BlockSpec((1, t, C_AK), lambda bi, i: (bi, 0, 0)),
                  pl.BlockSpec((1, C_AV, t), lambda bi, i: (bi, 0, 0))],
        out_specs=pl.BlockSpec((1, tq, C_AQ), lambda bi, i: (bi, i, 0)),
        out_shape=jax.ShapeDtypeStruct((b, t, C_AQ), BF16),
        scratch_shapes=[pltpu.VMEM((ATT_SLOTS, t, tq), F32), pltpu.VMEM((ATT_SLOTS, t, tq), BF16),
                        pltpu.VMEM((ATT_HEADS, 8, tq), F32)],
        compiler_params=_params(("parallel", "arbitrary"), 48),
        name="attention",
    )(aq, ak, av_t)


def _log_sigmoid(x):
    return jnp.minimum(x, 0.0) - jnp.log1p(jnp.exp(-jnp.abs(x)))


def _split3(a):
    a1 = a.astype(BF16)
    r1 = a - a1.astype(F32)
    a2 = r1.astype(BF16)
    a3 = (r1 - a2.astype(F32)).astype(BF16)
    return a1, a2, a3


def _lane_pick(a, idx):
    lane = lax.broadcasted_iota(I32, a.shape, 1)
    return jnp.sum(jnp.where(lane == idx, a, 0.0), axis=-1, keepdims=True)


def _mlstm_kernel(qf, kf, vf, gcf, grf, qb, kb, vb, gcb, grb, bc_ref, br_ref, hf_ref, hb_ref,
                  ct_ref, n_ref, m_ref):
    @pl.when(pl.program_id(1) == 0)
    def _():
        ct_ref[...] = jnp.zeros_like(ct_ref)
        n_ref[...] = jnp.zeros_like(n_ref)
        m_ref[...] = jnp.zeros_like(m_ref)

    L = ML_CHUNK
    row = lax.broadcasted_iota(I32, (L, L), 0)
    col = lax.broadcasted_iota(I32, (L, L), 1)
    lower = (col <= row)
    upper = (col >= row)
    lower_b = lower.astype(BF16)
    upper_b = upper.astype(BF16)
    nt = (((1,), (1,)), ((), ()))

    gates = {}
    for bb in range(qf.shape[0]):
        for d, (gc_ref, gr_ref) in enumerate(((gcf, grf), (gcb, grb))):
            cum_cols = upper_b if d else lower_b
            cum_rows = lower_b if d else upper_b
            pre_c = gc_ref[bb] + bc_ref[...]
            pre_r = gr_ref[bb] + br_ref[...]
            b_cols = sum(jnp.dot(cum_cols, part, preferred_element_type=F32) for part in _split3(_log_sigmoid(pre_c)))
            b_rows = sum(jnp.dot(part, cum_rows, preferred_element_type=F32) for part in _split3(_log_sigmoid(pre_r)))
            gates[bb, d] = (pre_r, b_cols, b_rows)

    chains = [(bb, d, hh) for bb in range(qf.shape[0]) for d in range(2) for hh in range(ML_HEADS)]
    refs = ((qf, kf, vf, hf_ref), (qb, kb, vb, hb_ref))

    st = {}
    for (bb, d, hh) in chains:
        q_ref, k_ref, v_ref, _ = refs[d]
        pre_r, b_cols, b_rows = gates[bb, d]
        ci = d * ML_HEADS + hh
        cf = 2 * ML_HEADS + d * ML_HEADS + hh
        sidx = (bb * 2 + d) * ML_HEADS + hh
        i_row = pre_r[ci:ci + 1, :]
        b_row = b_rows[cf:cf + 1, :]
        b_col = _lane_pick(b_cols, cf)
        m = m_ref[sidx][0:1, 0:1]
        dmat = jnp.where(upper if d else lower, b_col - b_row + i_row, -jnp.inf)
        m_inter = b_col + m
        m_loc = jnp.maximum(m_inter, jnp.max(dmat, axis=-1, keepdims=True))
        q = q_ref[bb, :, hh * ML_DQK:(hh + 1) * ML_DQK]
        k = k_ref[bb, :, hh * ML_DQK:(hh + 1) * ML_DQK]
        v = v_ref[bb, :, hh * ML_DV:(hh + 1) * ML_DV]
        qk = lax.dot_general(q, k, nt, preferred_element_type=F32)
        ct = ct_ref[sidx]
        inter = jnp.dot(q, ct.astype(BF16), preferred_element_type=F32)
        b_end = _lane_pick(b_row, 0 if d else L - 1)
        g_row = b_end - b_row + i_row
        m_new = jnp.maximum(b_end + m, jnp.max(g_row, axis=-1, keepdims=True))
        decay = jnp.exp(b_end + m - m_new)
        w_row = jnp.exp(g_row - m_new)
        ktw = (k.astype(F32).T * w_row).astype(BF16)
        upd = jnp.dot(ktw, v, preferred_element_type=F32)
        wk = jnp.dot(jnp.broadcast_to(w_row.astype(BF16), (8, L)), k, preferred_element_type=F32)
        st[bb, d, hh] = (sidx, dmat, m_inter, m_loc, q, v, qk, ct, inter, m_new, decay, upd, wk)

    sv = {}
    for key in chains:
        sidx, dmat, m_inter, m_loc, q, v, qk, ct, inter, m_new, decay, upd, wk = st[key]
        s = qk * jnp.exp(dmat - m_loc)
        sv[key] = (jnp.sum(s, axis=-1, keepdims=True), jnp.dot(s.astype(BF16), v, preferred_element_type=F32))

    for key in chains:
        bb, d, hh = key
        sidx, dmat, m_inter, m_loc, q, v, qk, ct, inter, m_new, decay, upd, wk = st[key]
        s_sum, intra = sv[key]
        a_inter = jnp.exp(m_inter - m_loc)
        n_row = n_ref[sidx]
        qn = jnp.sum(q.astype(F32) * n_row, axis=-1, keepdims=True)
        den = a_inter * qn + s_sum
        num = a_inter * inter + intra
        h_ref = refs[d][3]
        h_ref[bb, :, hh * ML_DV:(hh + 1) * ML_DV] = num * (1.0 / jnp.maximum(jnp.abs(den), jnp.exp(-m_loc)))
        ct_ref[sidx] = decay * ct + upd
        n_ref[sidx] = decay * n_row + wk[0:1, :]
        m_ref[sidx] = jnp.broadcast_to(m_new, (8, LANES))


def _mlstm(mq, mk, mv, mg, mg_t, bias_col, bias_row):
    b, t, _ = mq.shape
    L = ML_CHUNK
    nc = t // L
    sb = ML_SEQS if b % ML_SEQS == 0 else 1
    fw = lambda width: pl.BlockSpec((sb, L, width), lambda bi, c: (bi, c, 0))
    bw = lambda width: pl.BlockSpec((sb, L, width), lambda bi, c: (bi, nc - 1 - c, 0))
    fw_t = pl.BlockSpec((sb, C_MG, L), lambda bi, c: (bi, 0, c))
    bw_t = pl.BlockSpec((sb, C_MG, L), lambda bi, c: (bi, 0, nc - 1 - c))
    ns = sb * 2 * ML_HEADS
    return pl.pallas_call(
        _mlstm_kernel,
        grid=(b // sb, nc),
        in_specs=[fw(C_MQ), fw(C_MK), fw(C_MV), fw(LANES), fw_t,
                  bw(C_MQ), bw(C_MK), bw(C_MV), bw(LANES), bw_t,
                  _resident((1, LANES)), _resident((C_MG, L))],
        out_specs=[fw(C_MV), bw(C_MV)],
        out_shape=[jax.ShapeDtypeStruct((b, t, C_MV), F32)] * 2,
        scratch_shapes=[pltpu.VMEM((ns, ML_DQK, ML_DV), F32),
                        pltpu.VMEM((ns, 1, ML_DQK), F32),
                        pltpu.VMEM((ns, 8, LANES), F32)],
        compiler_params=_params(("parallel", "arbitrary"), 32),
        name="mlstm",
    )(mq, mk, mv, mg, mg_t, mq, mk, mv, mg, mg_t, bias_col, bias_row)


def _moe_input(x1, n2_ref, mod_ref):
    return _rms(x1, n2_ref[...]) * (1.0 + mod_ref[0, 4:5, :]) + mod_ref[0, 3:4, :]


def _mix_kernel(x_ref, att_ref, hf_ref, hb_ref, mo_ref, mlw_ref, wo_ref, mod_ref, n2_ref,
                wrh_ref, wrl_ref, br_ref, x1_ref, lg_ref):
    ml = hf_ref[0] + hb_ref[0]
    gate = jax.nn.sigmoid(mo_ref[0])
    mix = jnp.dot(att_ref[0], wo_ref[0:C_AQ, :], preferred_element_type=F32)
    for hh in range(ML_HEADS):
        sl = slice(hh * ML_DV, (hh + 1) * ML_DV)
        seg = (_rms(ml[:, sl], mlw_ref[:, sl]) * gate[:, sl]).astype(BF16)
        mix += jnp.dot(seg, wo_ref[C_AQ + hh * ML_DV:C_AQ + (hh + 1) * ML_DV, :], preferred_element_type=F32)
    x1 = x_ref[0] + mod_ref[0, 2:3, :] * mix
    x1_ref[0] = x1
    h2 = _moe_input(x1, n2_ref, mod_ref)
    h_hi = h2.astype(BF16)
    h_lo = (h2 - h_hi.astype(F32)).astype(BF16)
    nt = (((1,), (1,)), ((), ()))
    lg = lax.dot_general(wrh_ref[...], h_hi, nt, preferred_element_type=F32)
    lg += lax.dot_general(wrl_ref[...], h_hi, nt, preferred_element_type=F32)
    lg += lax.dot_general(wrh_ref[...], h_lo, nt, preferred_element_type=F32)
    lg_ref[0] = lg + br_ref[:, 0:1]


def _mix_and_router(x, att, h_fw, h_bw, mo, ml_norm_w, w_out, mod, norm2_w, wr_hi, wr_lo, b_router):
    b, t, d = x.shape
    tm = min(TM_MIX, t)
    row = lambda width: pl.BlockSpec((1, tm, width), lambda bi, i: (bi, i, 0))
    return pl.pallas_call(
        _mix_kernel,
        grid=(b, t // tm),
        in_specs=[row(d), row(C_AQ), row(C_MV), row(C_MV), row(C_MO),
                  _resident((1, C_MV)), _resident((d, d)),
                  pl.BlockSpec((1, 6, d), lambda bi, i: (bi, 0, 0)),
                  _resident((1, d)), _resident((ROUTER_ROWS, d)), _resident((ROUTER_ROWS, d)),
                  _resident((ROUTER_ROWS, LANES))],
        out_specs=[row(d), pl.BlockSpec((1, ROUTER_ROWS, tm), lambda bi, i: (bi, 0, i))],
        out_shape=[jax.ShapeDtypeStruct((b, t, d), F32), jax.ShapeDtypeStruct((b, ROUTER_ROWS, t), F32)],
        compiler_params=_params(("parallel", "arbitrary"), 48),
        name="mix_router",
    )(x, att, h_fw, h_bw, mo, ml_norm_w, w_out, mod, norm2_w, wr_hi, wr_lo, b_router)


def _first_argmax(rows):
    best = rows[0]
    idx = jnp.zeros_like(best)
    for j in range(1, len(rows)):
        better = rows[j] > best
        best = jnp.where(better, rows[j], best)
        idx = jnp.where(better, float(j), idx)
    return best, idx


def _softmax_rows(rows):
    mx = functools.reduce(jnp.maximum, rows)
    ex = [jnp.exp(r - mx) for r in rows]
    tot = functools.reduce(lambda a, c: a + c, ex)
    return [e / tot for e in ex]


def _route_kernel(lg_ref, idx_ref, wt_ref, cnt_ref, run_ref):
    @pl.when((pl.program_id(0) == 0) & (pl.program_id(1) == 0))
    def _():
        run_ref[...] = jnp.zeros_like(run_ref)

    tl = lg_ref.shape[2]
    lg = lg_ref[0]
    p_grp = _softmax_rows([lg[g:g + 1, :] for g in range(N_GROUPS)])
    p_g, g_idx = _first_argmax(p_grp)
    el = []
    for j in range(EXPERTS_PER_GROUP):
        sel = lg[N_GROUPS + (N_GROUPS - 1) * EXPERTS_PER_GROUP + j:N_GROUPS + (N_GROUPS - 1) * EXPERTS_PER_GROUP + j + 1, :]
        for g in range(N_GROUPS - 2, -1, -1):
            r = N_GROUPS + g * EXPERTS_PER_GROUP + j
            sel = jnp.where(g_idx == float(g), lg[r:r + 1, :], sel)
        el.append(sel)
    pe = _softmax_rows(el)
    w1, i1 = _first_argmax(pe)
    rest = [jnp.where(i1 == float(j), -1.0, pe[j]) for j in range(EXPERTS_PER_GROUP)]
    w2, i2 = _first_argmax(rest)
    tot = w1 + w2
    wt1 = w1 / tot * p_g
    wt2 = w2 / tot * p_g
    e1 = (g_idx * float(EXPERTS_PER_GROUP) + i1).astype(I32)
    e2 = (g_idx * float(EXPERTS_PER_GROUP) + i2).astype(I32)

    erow = lax.broadcasted_iota(I32, (N_EXPERTS, tl), 0)
    hit1 = erow == e1
    hit2 = erow == e2
    onehot = (hit1 | hit2).astype(F32)
    before = lax.broadcasted_iota(I32, (tl, tl), 0) <= lax.broadcasted_iota(I32, (tl, tl), 1)
    incl = jnp.dot(onehot.astype(BF16), before.astype(BF16), preferred_element_type=F32)
    base = run_ref[:, 0:1] + incl - onehot
    r1 = jnp.sum(jnp.where(hit1, base, 0.0), axis=0, keepdims=True).astype(I32)
    r2 = jnp.sum(jnp.where(hit2, base, 0.0), axis=0, keepdims=True).astype(I32)
    total = run_ref[...] + jnp.sum(onehot, axis=-1, keepdims=True)
    run_ref[...] = total
    cnt_ref[...] = total.astype(I32)

    zi = jnp.zeros((4, tl), I32)
    idx_ref[0] = jnp.concatenate([e1, e2, r1, r2, zi], axis=0)
    wt_ref[0] = jnp.concatenate([wt1, wt2, jnp.zeros((6, tl), F32)], axis=0)


def _route(logits_t):
    b, _, t = logits_t.shape
    tl = min(TL_ROUTE, t)
    blk = lambda rows: pl.BlockSpec((1, rows, tl), lambda bi, i: (bi, 0, i))
    return pl.pallas_call(
        _route_kernel,
        grid=(b, t // tl),
        in_specs=[blk(ROUTER_ROWS)],
        out_specs=[blk(8), blk(8), pl.BlockSpec((N_EXPERTS, LANES), lambda bi, i: (0, 0))],
        out_shape=[jax.ShapeDtypeStruct((b, 8, t), I32), jax.ShapeDtypeStruct((b, 8, t), F32),
                   jax.ShapeDtypeStruct((N_EXPERTS, LANES), I32)],
        scratch_shapes=[pltpu.VMEM((N_EXPERTS, LANES), F32)],
        compiler_params=_params(("arbitrary", "arbitrary"), 32),
        name="route",
    )(logits_t)


def _row_copy(src_ref, src_row, dst_ref, dst_row, sem):
    return pltpu.make_async_copy(src_ref.at[pl.ds(src_row, 1), :], dst_ref.at[pl.ds(dst_row, 1), :], sem)


def _dispatch_kernel(padlo_ref, padhi_ref, dest_ref, x1_ref, n2_ref, mod_ref, xs_ref, h_ref, zero_ref, sem):
    td = x1_ref.shape[0]
    h_ref[...] = _moe_input(x1_ref[...], n2_ref, mod_ref)

    def start(t, carry):
        for k in range(2):
            _row_copy(h_ref, t, xs_ref, dest_ref[0, k, t], sem).start()
        return carry

    lax.fori_loop(0, td, start, 0, unroll=DMA_UNROLL)
    for _ in range(2):
        pltpu.make_async_copy(h_ref, xs_ref.at[pl.ds(0, td), :], sem).wait()

    @pl.when(pl.program_id(0) == pl.num_programs(0) - 1)
    def _():
        zero_ref[...] = jnp.zeros_like(zero_ref)

        def zero_start(r, carry):
            _row_copy(zero_ref, 0, xs_ref, r, sem).start()
            return carry

        def zero_wait(r, carry):
            _row_copy(zero_ref, 0, xs_ref, r, sem).wait()
            return carry

        for e in range(N_EXPERTS):
            lax.fori_loop(padlo_ref[e], padhi_ref[e], zero_start, 0)
        for e in range(N_EXPERTS):
            lax.fori_loop(padlo_ref[e], padhi_ref[e], zero_wait, 0)


def _dispatch(x1, dest, norm2_w, mod, pad_lo, pad_hi, m_pad):
    n, d = x1.shape
    b, _, t = dest.shape
    td = min(TD_DISPATCH, t)
    per = t // td
    return pl.pallas_call(
        _dispatch_kernel,
        grid_spec=pltpu.PrefetchScalarGridSpec(
            num_scalar_prefetch=2,
            grid=(n // td,),
            in_specs=[pl.BlockSpec((1, 2, td), lambda i, *_: (i // per, 0, i % per), memory_space=pltpu.SMEM),
                      pl.BlockSpec((td, d), lambda i, *_: (i, 0)),
                      pl.BlockSpec((1, d), lambda i, *_: (0, 0)),
                      pl.BlockSpec((1, 6, d), lambda i, *_: (i // per, 0, 0))],
            out_specs=pl.BlockSpec(memory_space=pl.ANY),
            scratch_shapes=[pltpu.VMEM((td, d), F32), pltpu.VMEM((8, d), F32), pltpu.SemaphoreType.DMA(())]),
        out_shape=jax.ShapeDtypeStruct((m_pad, d), F32),
        compiler_params=_params(("arbitrary",), 32),
        name="dispatch",
    )(pad_lo, pad_hi, dest, x1, norm2_w, mod)


def _moe_kernel(blk_e_ref, nused_ref, x_ref, wg_ref, wu_ref, wd_ref, o_ref):
    r = pl.program_id(0)

    @pl.when(r < nused_ref[0])
    def _():
        xb = x_ref[...].astype(BF16)
        g = jnp.dot(xb, wg_ref[0], preferred_element_type=F32)
        u = jnp.dot(xb, wu_ref[0], preferred_element_type=F32)
        h = (g * jax.nn.sigmoid(g) * u).astype(BF16)
        o_ref[...] = jnp.dot(h, wd_ref[0], preferred_element_type=F32)

    @pl.when(r >= nused_ref[0])
    def _():
        o_ref[...] = jnp.zeros_like(o_ref)


def _moe_experts(xs, blk_e, nused, w_gate, w_up, w_down):
    m_pad, d = xs.shape
    f = w_gate.shape[2]
    tm = TM_MOE
    last = lambda r, nu: jnp.minimum(r, nu[0] - 1)
    return pl.pallas_call(
        _moe_kernel,
        grid_spec=pltpu.PrefetchScalarGridSpec(
            num_scalar_prefetch=2,
            grid=(m_pad // tm,),
            in_specs=[pl.BlockSpec((tm, d), lambda r, be, nu: (last(r, nu), 0)),
                      pl.BlockSpec((1, d, f), lambda r, be, nu: (be[last(r, nu)], 0, 0)),
                      pl.BlockSpec((1, d, f), lambda r, be, nu: (be[last(r, nu)], 0, 0)),
                      pl.BlockSpec((1, f, d), lambda r, be, nu: (be[last(r, nu)], 0, 0))],
            out_specs=pl.BlockSpec((tm, d), lambda r, be, nu: (r, 0))),
        out_shape=jax.ShapeDtypeStruct((m_pad, d), F32),
        compiler_params=_params(("arbitrary",), 56),
        name="moe_experts",
    )(blk_e, nused, xs, w_gate, w_up, w_down)


def _combine_kernel(dest_ref, x1_ref, w_ref, mod_ref, o_ref, y_ref, g0_ref, g1_ref, sem):
    tc = x1_ref.shape[0]
    bufs = (g0_ref, g1_ref)

    def start(t, carry):
        for k in range(2):
            _row_copy(o_ref, dest_ref[0, k, t], bufs[k], t, sem).start()
        return carry

    lax.fori_loop(0, tc, start, 0, unroll=DMA_UNROLL)
    for k in range(2):
        pltpu.make_async_copy(o_ref.at[pl.ds(0, tc), :], bufs[k], sem).wait()
    moe = g0_ref[...] * w_ref[:, 0:1] + g1_ref[...] * w_ref[:, 1:2]
    y_ref[...] = x1_ref[...] + mod_ref[0, 5:6, :] * moe


def _combine(x1, dest, w_cols, mod, o_rows):
    n, d = x1.shape
    b, _, t = dest.shape
    tc = min(TC_COMBINE, t)
    per = t // tc
    return pl.pallas_call(
        _combine_kernel,
        grid=(n // tc,),
        in_specs=[pl.BlockSpec((1, 2, tc), lambda i: (i // per, 0, i % per), memory_space=pltpu.SMEM),
                  pl.BlockSpec((tc, d), lambda i: (i, 0)),
                  pl.BlockSpec((tc, 2), lambda i: (i, 0)),
                  pl.BlockSpec((1, 6, d), lambda i: (i // per, 0, 0)),
                  pl.BlockSpec(memory_space=pl.ANY)],
        out_specs=pl.BlockSpec((tc, d), lambda i: (i, 0)),
        scratch_shapes=[pltpu.VMEM((tc, d), F32), pltpu.VMEM((tc, d), F32), pltpu.SemaphoreType.DMA(())],
        out_shape=jax.ShapeDtypeStruct((n, d), F32),
        compiler_params=_params(("arbitrary",), 32),
        name="combine",
    )(dest, x1, w_cols, mod, o_rows)


def _rope_tables(t):
    rows = t // GRID_W
    row = jnp.repeat(jnp.arange(rows, dtype=F32), GRID_W)
    col = jnp.tile(jnp.arange(GRID_W, dtype=F32), rows)
    freqs = ROPE_THETA ** (-jnp.arange(ROPE_PAIRS, dtype=F32) / ROPE_PAIRS)
    ar = row[:, None] * freqs
    ac = col[:, None] * freqs
    cos_t = jnp.concatenate([jnp.cos(ar), jnp.cos(ar), jnp.cos(ac), jnp.cos(ac)], axis=-1)
    sin_t = jnp.concatenate([-jnp.sin(ar), jnp.sin(ar), -jnp.sin(ac), jnp.sin(ac)], axis=-1)
    return cos_t, sin_t


def _encoder_layer(x, mod, p):
    b, t, d = x.shape
    n = b * t
    cos_t, sin_t = _rope_tables(t)
    aq, ak, av_t, mq, mk, mv, mo, mg = _in_projection(
        x, mod, p["norm1_w"], p["w_main"], p["w_v_t"], p["w_gates"], p["q_norm_w"], p["k_norm_w"], cos_t, sin_t)
    att = _attention(aq, ak, av_t)
    mg_t = jnp.swapaxes(mg[:, :, :C_MG], 1, 2)
    h_fw, h_bw = _mlstm(mq, mk, mv, mg, mg_t, p["gate_bias_col"], p["gate_bias_row"])
    x1, logits_t = _mix_and_router(x, att, h_fw, h_bw, mo, p["ml_norm_w"], p["w_out"], mod,
                                       p["norm2_w"], p["wr_hi"], p["wr_lo"], p["b_router"])
    idx, wts, counts = _route(logits_t)

    tm = TM_MOE
    counts = counts[:, 0]
    padded = (counts + tm - 1) // tm * tm
    pend = jnp.cumsum(padded)
    pstart = (pend - padded).astype(I32)
    nb = (2 * n + tm - 1) // tm + N_EXPERTS
    m_pad = nb * tm
    block_row0 = jnp.arange(nb, dtype=I32) * tm
    blk_e = jnp.minimum(jnp.sum(pend[None, :] <= block_row0[:, None], axis=1), N_EXPERTS - 1).astype(I32)
    nused = (pend[-1:] // tm).astype(I32)
    onehot = idx[:, 0:2, :, None] == jnp.arange(N_EXPERTS, dtype=I32)
    dest = jnp.sum(jnp.where(onehot, pstart, 0), axis=-1) + idx[:, 2:4, :]

    pad_lo = (pstart + counts).astype(I32)
    pad_hi = jnp.concatenate([pstart[1:], jnp.full((1,), m_pad, I32)])
    x1 = x1.reshape(n, d)
    xs = _dispatch(x1, dest, p["norm2_w"], mod, pad_lo, pad_hi, m_pad)
    o_rows = _moe_experts(xs, blk_e, nused, p["w_gate"], p["w_up"], p["w_down"])
    w_cols = jnp.swapaxes(wts[:, :2, :], 1, 2).reshape(n, 2)
    y = _combine(x1, dest, w_cols, mod, o_rows)
    return y.reshape(b, t, d)


def kernel(x_prompt, x_sample, c_prompt, c_sample, norm1_w, norm2_w, w_ada, b_ada, w_in, q_norm_w, k_norm_w, b_igate, b_fgate, ml_norm_w, w_out, w_gr, b_gr, w_er, b_er, w_gate, w_up, w_down):
    depth = norm1_w.shape[0]
    d = x_prompt.shape[-1]
    bp = x_prompt.shape[0]
    bs = x_sample.shape[0]
    rows = -(-(bp + bs) // 8) * 8
    y_prompt, y_sample = x_prompt, x_sample
    for l in range(depth):
        c_pad = jnp.zeros((rows, d), F32).at[:bp].set(c_prompt).at[bp:bp + bs].set(c_sample)
        mod = _ada_modulation(c_pad, w_ada[l], b_ada[l]).reshape(rows, 6, d)
        gate_bias = jnp.concatenate([b_igate[l].reshape(-1), b_fgate[l].reshape(-1)])
        w_router = jnp.concatenate([w_gr[l], w_er[l]], axis=1).T
        w_router = jnp.zeros((ROUTER_ROWS, d), F32).at[:N_GROUPS + N_EXPERTS].set(w_router)
        wr_hi = w_router.astype(BF16)
        b_router = jnp.zeros((ROUTER_ROWS,), F32).at[:N_GROUPS + N_EXPERTS].set(jnp.concatenate([b_gr[l], b_er[l]]))
        p = {
            "norm1_w": norm1_w[l].reshape(1, d),
            "norm2_w": norm2_w[l].reshape(1, d),
            "w_main": w_in[l][:, :C_MAIN].astype(BF16),
            "w_v_t": w_in[l][:, C_AQ + C_AK:C_AQ + C_AK + C_AV].T.astype(BF16),
            "w_gates": jnp.zeros((d, LANES), BF16).at[:, :C_MG].set(w_in[l][:, C_MAIN:].astype(BF16)),
            "q_norm_w": q_norm_w[l].reshape(1, HEAD_DIM),
            "k_norm_w": k_norm_w[l].reshape(1, HEAD_DIM),
            "gate_bias_col": jnp.zeros((1, LANES), F32).at[0, :C_MG].set(gate_bias),
            "gate_bias_row": jnp.broadcast_to(gate_bias[:, None], (C_MG, ML_CHUNK)),
            "ml_norm_w": ml_norm_w[l].reshape(1, C_MV),
            "w_out": w_out[l].astype(BF16),
            "wr_hi": wr_hi,
            "wr_lo": (w_router - wr_hi.astype(F32)).astype(BF16),
            "b_router": jnp.broadcast_to(b_router[:, None], (ROUTER_ROWS, LANES)),
            "w_gate": w_gate[l].astype(BF16),
            "w_up": w_up[l].astype(BF16),
            "w_down": w_down[l].astype(BF16),
        }
        y_prompt = _encoder_layer(y_prompt, mod[:bp], p)
        y_sample = _encoder_layer(y_sample, mod[bp:bp + bs], p)
    return (y_prompt, y_sample)
```

```python
import functools

import jax
import jax.numpy as jnp
import numpy as np
from jax import lax
from jax.experimental import pallas as pl
from jax.experimental.pallas import tpu as pltpu

F32 = jnp.float32
BF16 = jnp.bfloat16
I32 = jnp.int32

GRID_W = 64
HEAD_DIM = 128
ATT_HEADS = 8
ATT_KV_HEADS = 2
ATT_GROUP = ATT_HEADS // ATT_KV_HEADS
ROPE_THETA = 10000.0
ROPE_PAIRS = HEAD_DIM // 4
ML_HEADS = 4
ML_DV = 256
ML_DQK = 128
ML_CHUNK = 128
N_GROUPS = 4
EXPERTS_PER_GROUP = 4
N_EXPERTS = N_GROUPS * EXPERTS_PER_GROUP
EPS = 1e-6
Q_SCALE = HEAD_DIM ** -0.5 * float(np.log2(np.e))

C_AQ = ATT_HEADS * HEAD_DIM
C_AK = ATT_KV_HEADS * HEAD_DIM
C_AV = ATT_KV_HEADS * HEAD_DIM
C_MQ = ML_HEADS * ML_DQK
C_MK = ML_HEADS * ML_DQK
C_MV = ML_HEADS * ML_DV
C_MO = ML_HEADS * ML_DV
C_MG = 4 * ML_HEADS
C_MAIN = C_AQ + C_AK + C_AV + C_MQ + C_MK + C_MV + C_MO

LANES = 128
ROUTER_ROWS = 32
MIB = 1024 * 1024

TM_PROJ = 256
TQ_ATT = 256
TM_MIX = 512
TL_ROUTE = 512
TD_DISPATCH = 256
TM_MOE = 256
TC_COMBINE = 256
ML_SEQS = 2
ATT_SLOTS = 4
DMA_UNROLL = 8


def _params(semantics, vmem_mib):
    return pltpu.CompilerParams(dimension_semantics=semantics, vmem_limit_bytes=vmem_mib * MIB)


def _resident(shape):
    nd = len(shape)
    return pl.BlockSpec(shape, lambda *_: (0,) * nd, pipeline_mode=pl.Buffered(1))


def _ada_kernel(c_ref, w_ref, b_ref, o_ref):
    c = c_ref[...]
    s = (c * jax.nn.sigmoid(c)).astype(BF16)
    o_ref[...] = jnp.dot(s, w_ref[...].astype(BF16), preferred_element_type=F32) + b_ref[...]


def _ada_modulation(c_pad, w_ada, b_ada):
    rows, d = c_pad.shape
    n = w_ada.shape[1]
    tn = 1024
    return pl.pallas_call(
        _ada_kernel,
        grid=(n // tn,),
        in_specs=[pl.BlockSpec((rows, d), lambda j: (0, 0)),
                  pl.BlockSpec((d, tn), lambda j: (0, j)),
                  pl.BlockSpec((1, tn), lambda j: (0, j))],
        out_specs=pl.BlockSpec((rows, tn), lambda j: (0, j)),
        out_shape=jax.ShapeDtypeStruct((rows, n), F32),
        compiler_params=_params(("arbitrary",), 40),
        name="ada_modulation",
    )(c_pad, w_ada, b_ada.reshape(1, n))


def _rms(x, w):
    return x * lax.rsqrt(jnp.mean(x * x, axis=-1, keepdims=True) + EPS) * w


def _inproj_kernel(x_ref, mod_ref, n1_ref, w_ref, wvt_ref, wg_ref, qn_ref, kn_ref, cos_ref, sin_ref,
                   aq_ref, ak_ref, avt_ref, mq_ref, mkt_ref, mv_ref, mo_ref, mg_ref):
    x = x_ref[0]
    h = _rms(x, n1_ref[...]) * (1.0 + mod_ref[0, 1:2, :]) + mod_ref[0, 0:1, :]
    hb = h.astype(BF16)

    def proj(c0, width):
        return jnp.dot(hb, w_ref[:, c0:c0 + width], preferred_element_type=F32)

    cos = cos_ref[...]
    sin = sin_ref[...]
    lane = lax.broadcasted_iota(I32, (1, HEAD_DIM), 1)
    first = (lane % (2 * ROPE_PAIRS)) < ROPE_PAIRS

    def norm_rope(p, w):
        pn = _rms(p, w)
        partner = jnp.where(first, pltpu.roll(pn, HEAD_DIM - ROPE_PAIRS, 1), pltpu.roll(pn, ROPE_PAIRS, 1))
        return pn * cos + partner * sin

    c0 = 0
    for half in range(2):
        p = proj(c0, C_AQ // 2)
        for hh in range(ATT_HEADS // 2):
            col = half * (C_AQ // 2) + hh * HEAD_DIM
            qh = norm_rope(p[:, hh * HEAD_DIM:(hh + 1) * HEAD_DIM], qn_ref[...])
            aq_ref[0, :, col:col + HEAD_DIM] = (qh * Q_SCALE).astype(BF16)
        c0 += C_AQ // 2
    p = proj(c0, C_AK)
    for hh in range(ATT_KV_HEADS):
        ak_ref[0, :, hh * HEAD_DIM:(hh + 1) * HEAD_DIM] = norm_rope(p[:, hh * HEAD_DIM:(hh + 1) * HEAD_DIM], kn_ref[...]).astype(BF16)
    nt = (((1,), (1,)), ((), ()))
    avt_ref[0] = lax.dot_general(wvt_ref[0:C_AV, :], hb, nt, preferred_element_type=F32).astype(BF16)
    mkt_ref[0] = lax.dot_general(wvt_ref[C_AV:, :], hb, nt, preferred_element_type=F32).astype(BF16)
    c0 += C_AK + C_AV
    mq_ref[0] = (proj(c0, C_MQ) * (ML_DQK ** -0.5)).astype(BF16)
    c0 += C_MQ + C_MK
    for half in range(2):
        mv_ref[0, :, half * 512:(half + 1) * 512] = proj(c0, 512).astype(BF16)
        c0 += 512
    for half in range(2):
        mo_ref[0, :, half * 512:(half + 1) * 512] = proj(c0, 512)
        c0 += 512
    mg_ref[0] = jnp.dot(hb, wg_ref[...], preferred_element_type=F32)


def _in_projection(x, mod, norm1_w, w_main, w_v_t, w_gates, q_norm_w, k_norm_w, cos_t, sin_t):
    b, t, d = x.shape
    tm = min(TM_PROJ, t)
    row = lambda width: pl.BlockSpec((1, tm, width), lambda bi, i: (bi, i, 0))
    outs = ((C_AQ, False), (C_AK, False), (C_AV, True), (C_MQ, False), (C_MK, True), (C_MV, False),
            (C_MO, False), (LANES, False))
    dtypes = (BF16, BF16, BF16, BF16, BF16, BF16, F32, F32)
    col = lambda width: pl.BlockSpec((1, width, tm), lambda bi, i: (bi, 0, i))
    return pl.pallas_call(
        _inproj_kernel,
        grid=(b, t // tm),
        in_specs=[row(d),
                  pl.BlockSpec((1, 6, d), lambda bi, i: (bi, 0, 0)),
                  _resident((1, d)),
                  _resident((d, C_MAIN)),
                  _resident((C_AV + C_MK, d)),
                  _resident((d, LANES)),
                  _resident((1, HEAD_DIM)),
                  _resident((1, HEAD_DIM)),
                  pl.BlockSpec((tm, HEAD_DIM), lambda bi, i: (i, 0)),
                  pl.BlockSpec((tm, HEAD_DIM), lambda bi, i: (i, 0))],
        out_specs=[col(w) if tr else row(w) for w, tr in outs],
        out_shape=[jax.ShapeDtypeStruct((b, w, t) if tr else (b, t, w), dt) for (w, tr), dt in zip(outs, dtypes)],
        compiler_params=_params(("parallel", "arbitrary"), 48),
        name="in_projection",
    )(x, mod, norm1_w, w_main, w_v_t, w_gates, q_norm_w, k_norm_w, cos_t, sin_t)


def _attention_kernel(q_ref, k_ref, vt_ref, o_ref, s_ref, p_ref, l_ref):
    def scores(g):
        kv = g // ATT_GROUP
        q = q_ref[0, :, g * HEAD_DIM:(g + 1) * HEAD_DIM]
        k = k_ref[0, :, kv * HEAD_DIM:(kv + 1) * HEAD_DIM]
        s_ref[g % ATT_SLOTS] = lax.dot_general(k, q, (((1,), (1,)), ((), ())), preferred_element_type=F32)

    def softmax(g):
        s = s_ref[g % ATT_SLOTS]
        m = jnp.max(s, axis=0, keepdims=True)
        p = jnp.exp2(s - m)
        l_ref[g] = jnp.broadcast_to(jnp.sum(p, axis=0, keepdims=True), l_ref.shape[1:])
        p_ref[g % ATT_SLOTS] = p.astype(BF16)

    def values(g):
        kv = g // ATT_GROUP
        vt = vt_ref[0, kv * HEAD_DIM:(kv + 1) * HEAD_DIM, :]
        o_t = jnp.dot(vt, p_ref[g % ATT_SLOTS], preferred_element_type=F32)
        o_ref[0, :, g * HEAD_DIM:(g + 1) * HEAD_DIM] = (o_t / l_ref[g][0:1, :]).T.astype(BF16)

    n = ATT_HEADS
    for step in range(n + 2):
        if step < n:
            scores(step)
        if 1 <= step <= n:
            softmax(step - 1)
        if step >= 2:
            values(step - 2)


def _attention(aq, ak, av_t):
    b, t, _ = aq.shape
    tq = min(TQ_ATT, t)
    return pl.pallas_call(
        _attention_kernel,
        grid=(b, t // tq),
        in_specs=[pl.BlockSpec((1, tq, C_AQ), lambda bi, i: (bi, i, 0)),
                  pl.BlockSpec((1, t, C_AK), lambda bi, i: (bi, 0, 0)),
                  pl.BlockSpec((1, C_AV, t), lambda bi, i: (bi, 0, 0))],
        out_specs=pl.BlockSpec((1, tq, C_AQ), lambda bi, i: (bi, i, 0)),
        out_shape=jax.ShapeDtypeStruct((b, t, C_AQ), BF16),
        scratch_shapes=[pltpu.VMEM((ATT_SLOTS, t, tq), F32), pltpu.VMEM((ATT_SLOTS, t, tq), BF16),
                        pltpu.VMEM((ATT_HEADS, 8, tq), F32)],
        compiler_params=_params(("parallel", "arbitrary"), 48),
        name="attention",
    )(aq, ak, av_t)


def _log_sigmoid(x):
    return jnp.minimum(x, 0.0) - jnp.log1p(jnp.exp(-jnp.abs(x)))


def _split3(a):
    a1 = a.astype(BF16)
    r1 = a - a1.astype(F32)
    a2 = r1.astype(BF16)
    a3 = (r1 - a2.astype(F32)).astype(BF16)
    return a1, a2, a3


def _lane_pick(a, idx):
    lane = lax.broadcasted_iota(I32, a.shape, 1)
    return jnp.sum(jnp.where(lane == idx, a, 0.0), axis=-1, keepdims=True)


def _mlstm_kernel(qf, kf, vf, gcf, grf, qb, kb, vb, gcb, grb, bc_ref, br_ref, hf_ref, hb_ref,
                  ct_ref, m_ref):
    @pl.when(pl.program_id(1) == 0)
    def _():
        ct_ref[...] = jnp.zeros_like(ct_ref)
        m_ref[...] = jnp.zeros_like(m_ref)

    L = ML_CHUNK
    row = lax.broadcasted_iota(I32, (L, L), 0)
    col = lax.broadcasted_iota(I32, (L, L), 1)
    lower = (col <= row)
    upper = (col >= row)
    lower_b = lower.astype(BF16)
    upper_b = upper.astype(BF16)

    gates = {}
    for bb in range(qf.shape[0]):
        for d, (gc_ref, gr_ref) in enumerate(((gcf, grf), (gcb, grb))):
            cum_cols = upper_b if d else lower_b
            cum_rows = lower_b if d else upper_b
            pre_c = gc_ref[bb] + bc_ref[...]
            pre_r = gr_ref[bb] + br_ref[...]
            b_cols = sum(jnp.dot(cum_cols, part, preferred_element_type=F32) for part in _split3(_log_sigmoid(pre_c)))
            b_rows = sum(jnp.dot(part, cum_rows, preferred_element_type=F32) for part in _split3(_log_sigmoid(pre_r)))
            gates[bb, d] = (pre_r, b_cols, b_rows)

    chains = [(bb, d, hh) for bb in range(qf.shape[0]) for d in range(2) for hh in range(ML_HEADS)]
    refs = ((qf, kf, vf, hf_ref), (qb, kb, vb, hb_ref))
    ones_block = jnp.ones((L, LANES), BF16)


    st = {}
    for (bb, d, hh) in chains:
        q_ref, kt_ref, v_ref, _ = refs[d]
        pre_r, b_cols, b_rows = gates[bb, d]
        ci = d * ML_HEADS + hh
        cf = 2 * ML_HEADS + d * ML_HEADS + hh
        sidx = (bb * 2 + d) * ML_HEADS + hh
        i_row = pre_r[ci:ci + 1, :]
        b_row = b_rows[cf:cf + 1, :]
        m = m_ref[sidx][0:1, 0:1]
        c_vis = jnp.where(upper if d else lower, i_row - b_row, -jnp.inf)
        shift = -jnp.maximum(m, jnp.max(c_vis, axis=-1, keepdims=True))
        dec = jnp.exp(c_vis + shift)
        a_inter = jnp.exp(m + shift)
        floor = jnp.exp(shift - _lane_pick(b_cols, cf))
        q = q_ref[bb, :, hh * ML_DQK:(hh + 1) * ML_DQK]
        kt = kt_ref[bb, hh * ML_DQK:(hh + 1) * ML_DQK, :]
        v_aug = jnp.concatenate([v_ref[bb, :, hh * ML_DV:(hh + 1) * ML_DV], ones_block], axis=1)
        qk = jnp.dot(q, kt, preferred_element_type=F32)
        ct = ct_ref[sidx]
        inter = jnp.dot(q, ct.astype(BF16), preferred_element_type=F32)
        b_end = _lane_pick(b_row, 0 if d else L - 1)
        g_row = b_end - b_row + i_row
        m_new = jnp.maximum(b_end + m, jnp.max(g_row, axis=-1, keepdims=True))
        decay = jnp.exp(b_end + m - m_new)
        w_row = jnp.exp(g_row - m_new)
        upd = jnp.dot((kt.astype(F32) * w_row).astype(BF16), v_aug, preferred_element_type=F32)
        st[bb, d, hh] = (sidx, dec, a_inter, floor, v_aug, qk, ct, inter, m_new, decay, upd)

    sv = {}
    for key in chains:
        sidx, dec, a_inter, floor, v_aug, qk, ct, inter, m_new, decay, upd = st[key]
        sv[key] = jnp.dot((qk * dec).astype(BF16), v_aug, preferred_element_type=F32)

    for key in chains:
        bb, d, hh = key
        sidx, dec, a_inter, floor, v_aug, qk, ct, inter, m_new, decay, upd = st[key]
        both = a_inter * inter + sv[key]
        scale = 1.0 / jnp.maximum(jnp.abs(both[:, ML_DV:]), floor)
        h_ref = refs[d][3]
        for blk in range(ML_DV // LANES):
            lo = hh * ML_DV + blk * LANES
            h_ref[bb, :, lo:lo + LANES] = both[:, blk * LANES:(blk + 1) * LANES] * scale
        ct_ref[sidx] = decay * ct + upd
        m_ref[sidx] = jnp.broadcast_to(m_new, (8, LANES))


def _mlstm(mq, mk_t, mv, mg, mg_t, bias_col, bias_row):
    b, t, _ = mq.shape
    L = ML_CHUNK
    nc = t // L
    sb = ML_SEQS if b % ML_SEQS == 0 else 1
    fw = lambda width: pl.BlockSpec((sb, L, width), lambda bi, c: (bi, c, 0))
    bw = lambda width: pl.BlockSpec((sb, L, width), lambda bi, c: (bi, nc - 1 - c, 0))
    fw_t = lambda rows: pl.BlockSpec((sb, rows, L), lambda bi, c: (bi, 0, c))
    bw_t = lambda rows: pl.BlockSpec((sb, rows, L), lambda bi, c: (bi, 0, nc - 1 - c))
    ns = sb * 2 * ML_HEADS
    return pl.pallas_call(
        _mlstm_kernel,
        grid=(b // sb, nc),
        in_specs=[fw(C_MQ), fw_t(C_MK), fw(C_MV), fw(LANES), fw_t(C_MG),
                  bw(C_MQ), bw_t(C_MK), bw(C_MV), bw(LANES), bw_t(C_MG),
                  _resident((1, LANES)), _resident((C_MG, L))],
        out_specs=[fw(C_MV), bw(C_MV)],
        out_shape=[jax.ShapeDtypeStruct((b, t, C_MV), F32)] * 2,
        scratch_shapes=[pltpu.VMEM((ns, ML_DQK, ML_DV + LANES), F32),
                        pltpu.VMEM((ns, 8, LANES), F32)],
        compiler_params=_params(("parallel", "arbitrary"), 32),
        name="mlstm",
    )(mq, mk_t, mv, mg, mg_t, mq, mk_t, mv, mg, mg_t, bias_col, bias_row)


def _moe_input(x1, n2_ref, mod_ref):
    return _rms(x1, n2_ref[...]) * (1.0 + mod_ref[0, 4:5, :]) + mod_ref[0, 3:4, :]


def _mix_kernel(x_ref, att_ref, hf_ref, hb_ref, mo_ref, mlw_ref, wo_ref, mod_ref, n2_ref,
                wrh_ref, wrl_ref, br_ref, x1_ref, lg_ref):
    ml = hf_ref[0] + hb_ref[0]
    gate = jax.nn.sigmoid(mo_ref[0])
    mix = jnp.dot(att_ref[0], wo_ref[0:C_AQ, :], preferred_element_type=F32)
    for hh in range(ML_HEADS):
        sl = slice(hh * ML_DV, (hh + 1) * ML_DV)
        seg = (_rms(ml[:, sl], mlw_ref[:, sl]) * gate[:, sl]).astype(BF16)
        mix += jnp.dot(seg, wo_ref[C_AQ + hh * ML_DV:C_AQ + (hh + 1) * ML_DV, :], preferred_element_type=F32)
    x1 = x_ref[0] + mod_ref[0, 2:3, :] * mix
    x1_ref[0] = x1
    h2 = _moe_input(x1, n2_ref, mod_ref)
    h_hi = h2.astype(BF16)
    h_lo = (h2 - h_hi.astype(F32)).astype(BF16)
    nt = (((1,), (1,)), ((), ()))
    both = lax.dot_general(jnp.concatenate([wrh_ref[...], wrl_ref[...]], axis=0), h_hi, nt, preferred_element_type=F32)
    lg = both[:ROUTER_ROWS] + both[ROUTER_ROWS:] + lax.dot_general(wrh_ref[...], h_lo, nt, preferred_element_type=F32)
    lg_ref[0] = lg + br_ref[:, 0:1]


def _mix_and_router(x, att, h_fw, h_bw, mo, ml_norm_w, w_out, mod, norm2_w, wr_hi, wr_lo, b_router):
    b, t, d = x.shape
    tm = min(TM_MIX, t)
    row = lambda width: pl.BlockSpec((1, tm, width), lambda bi, i: (bi, i, 0))
    return pl.pallas_call(
        _mix_kernel,
        grid=(b, t // tm),
        in_specs=[row(d), row(C_AQ), row(C_MV), row(C_MV), row(C_MO),
                  _resident((1, C_MV)), _resident((d, d)),
                  pl.BlockSpec((1, 6, d), lambda bi, i: (bi, 0, 0)),
                  _resident((1, d)), _resident((ROUTER_ROWS, d)), _resident((ROUTER_ROWS, d)),
                  _resident((ROUTER_ROWS, LANES))],
        out_specs=[row(d), pl.BlockSpec((1, ROUTER_ROWS, tm), lambda bi, i: (bi, 0, i))],
        out_shape=[jax.ShapeDtypeStruct((b, t, d), F32), jax.ShapeDtypeStruct((b, ROUTER_ROWS, t), F32)],
        compiler_params=_params(("parallel", "arbitrary"), 48),
        name="mix_router",
    )(x, att, h_fw, h_bw, mo, ml_norm_w, w_out, mod, norm2_w, wr_hi, wr_lo, b_router)


def _first_argmax(rows):
    best = rows[0]
    idx = jnp.zeros_like(best)
    for j in range(1, len(rows)):
        better = rows[j] > best
        best = jnp.where(better, rows[j], best)
        idx = jnp.where(better, float(j), idx)
    return best, idx


def _softmax_rows(rows):
    mx = functools.reduce(jnp.maximum, rows)
    ex = [jnp.exp(r - mx) for r in rows]
    tot = functools.reduce(lambda a, c: a + c, ex)
    return [e / tot for e in ex]


def _route_kernel(lg_ref, idx_ref, wt_ref, cnt_ref, run_ref):
    @pl.when((pl.program_id(0) == 0) & (pl.program_id(1) == 0))
    def _():
        run_ref[...] = jnp.zeros_like(run_ref)

    tl = lg_ref.shape[2]
    lg = lg_ref[0]
    p_grp = _softmax_rows([lg[g:g + 1, :] for g in range(N_GROUPS)])
    p_g, g_idx = _first_argmax(p_grp)
    el = []
    for j in range(EXPERTS_PER_GROUP):
        sel = lg[N_GROUPS + (N_GROUPS - 1) * EXPERTS_PER_GROUP + j:N_GROUPS + (N_GROUPS - 1) * EXPERTS_PER_GROUP + j + 1, :]
        for g in range(N_GROUPS - 2, -1, -1):
            r = N_GROUPS + g * EXPERTS_PER_GROUP + j
            sel = jnp.where(g_idx == float(g), lg[r:r + 1, :], sel)
        el.append(sel)
    pe = _softmax_rows(el)
    w1, i1 = _first_argmax(pe)
    rest = [jnp.where(i1 == float(j), -1.0, pe[j]) for j in range(EXPERTS_PER_GROUP)]
    w2, i2 = _first_argmax(rest)
    tot = w1 + w2
    wt1 = w1 / tot * p_g
    wt2 = w2 / tot * p_g
    e1 = (g_idx * float(EXPERTS_PER_GROUP) + i1).astype(I32)
    e2 = (g_idx * float(EXPERTS_PER_GROUP) + i2).astype(I32)

    erow = lax.broadcasted_iota(I32, (N_EXPERTS, tl), 0)
    hit1 = erow == e1
    hit2 = erow == e2
    onehot = (hit1 | hit2).astype(F32)
    before = lax.broadcasted_iota(I32, (tl, tl), 0) <= lax.broadcasted_iota(I32, (tl, tl), 1)
    incl = jnp.dot(onehot.astype(BF16), before.astype(BF16), preferred_element_type=F32)
    base = run_ref[:, 0:1] + incl - onehot
    r1 = jnp.sum(jnp.where(hit1, base, 0.0), axis=0, keepdims=True).astype(I32)
    r2 = jnp.sum(jnp.where(hit2, base, 0.0), axis=0, keepdims=True).astype(I32)
    total = run_ref[...] + jnp.sum(onehot, axis=-1, keepdims=True)
    run_ref[...] = total
    cnt_ref[...] = total.astype(I32)

    zi = jnp.zeros((4, tl), I32)
    idx_ref[0] = jnp.concatenate([e1, e2, r1, r2, zi], axis=0)
    wt_ref[0] = jnp.concatenate([wt1, wt2, jnp.zeros((6, tl), F32)], axis=0)


def _route(logits_t):
    b, _, t = logits_t.shape
    tl = min(TL_ROUTE, t)
    blk = lambda rows: pl.BlockSpec((1, rows, tl), lambda bi, i: (bi, 0, i))
    return pl.pallas_call(
        _route_kernel,
        grid=(b, t // tl),
        in_specs=[blk(ROUTER_ROWS)],
        out_specs=[blk(8), blk(8), pl.BlockSpec((N_EXPERTS, LANES), lambda bi, i: (0, 0))],
        out_shape=[jax.ShapeDtypeStruct((b, 8, t), I32), jax.ShapeDtypeStruct((b, 8, t), F32),
                   jax.ShapeDtypeStruct((N_EXPERTS, LANES), I32)],
        scratch_shapes=[pltpu.VMEM((N_EXPERTS, LANES), F32)],
        compiler_params=_params(("arbitrary", "arbitrary"), 32),
        name="route",
    )(logits_t)


def _row_copy(src_ref, src_row, dst_ref, dst_row, sem):
    return pltpu.make_async_copy(src_ref.at[pl.ds(src_row, 1), :], dst_ref.at[pl.ds(dst_row, 1), :], sem)


def _start_row_copy(src_ref, src_row, dst_ref, dst_row, sem, priority):
    pltpu.async_copy(src_ref.at[pl.ds(src_row, 1), :], dst_ref.at[pl.ds(dst_row, 1), :], sem, priority=priority)


def _dispatch_kernel(padlo_ref, padhi_ref, dest_ref, x1_ref, n2_ref, mod_ref, xs_ref, h_ref, zero_ref, sem):
    td = x1_ref.shape[0]
    h_ref[...] = _moe_input(x1_ref[...], n2_ref, mod_ref)

    def start(t, carry):
        for k in range(2):
            _start_row_copy(h_ref, t, xs_ref, dest_ref[0, k, t], sem, priority=k)
        return carry

    lax.fori_loop(0, td, start, 0, unroll=DMA_UNROLL)
    for _ in range(2):
        pltpu.make_async_copy(h_ref, xs_ref.at[pl.ds(0, td), :], sem).wait()

    @pl.when(pl.program_id(0) == pl.num_programs(0) - 1)
    def _():
        zero_ref[...] = jnp.zeros_like(zero_ref)

        def zero_start(r, carry):
            _row_copy(zero_ref, 0, xs_ref, r, sem).start()
            return carry

        def zero_wait(r, carry):
            _row_copy(zero_ref, 0, xs_ref, r, sem).wait()
            return carry

        for e in range(N_EXPERTS):
            lax.fori_loop(padlo_ref[e], padhi_ref[e], zero_start, 0)
        for e in range(N_EXPERTS):
            lax.fori_loop(padlo_ref[e], padhi_ref[e], zero_wait, 0)


def _dispatch(x1, dest, norm2_w, mod, pad_lo, pad_hi, m_pad):
    n, d = x1.shape
    b, _, t = dest.shape
    td = min(TD_DISPATCH, t)
    per = t // td
    return pl.pallas_call(
        _dispatch_kernel,
        grid_spec=pltpu.PrefetchScalarGridSpec(
            num_scalar_prefetch=2,
            grid=(n // td,),
            in_specs=[pl.BlockSpec((1, 2, td), lambda i, *_: (i // per, 0, i % per), memory_space=pltpu.SMEM),
                      pl.BlockSpec((td, d), lambda i, *_: (i, 0)),
                      pl.BlockSpec((1, d), lambda i, *_: (0, 0)),
                      pl.BlockSpec((1, 6, d), lambda i, *_: (i // per, 0, 0))],
            out_specs=pl.BlockSpec(memory_space=pl.ANY),
            scratch_shapes=[pltpu.VMEM((td, d), F32), pltpu.VMEM((8, d), F32), pltpu.SemaphoreType.DMA(())]),
        out_shape=jax.ShapeDtypeStruct((m_pad, d), F32),
        compiler_params=_params(("arbitrary",), 32),
        name="dispatch",
    )(pad_lo, pad_hi, dest, x1, norm2_w, mod)


def _moe_kernel(blk_e_ref, nused_ref, x_ref, wg_ref, wu_ref, wd_ref, o_ref):
    r = pl.program_id(0)

    @pl.when(r < nused_ref[0])
    def _():
        xb = x_ref[...].astype(BF16)
        g = jnp.dot(xb, wg_ref[0], preferred_element_type=F32)
        u = jnp.dot(xb, wu_ref[0], preferred_element_type=F32)
        h = (g * jax.nn.sigmoid(g) * u).astype(BF16)
        o_ref[...] = jnp.dot(h, wd_ref[0], preferred_element_type=F32)

    @pl.when(r >= nused_ref[0])
    def _():
        o_ref[...] = jnp.zeros_like(o_ref)


def _moe_experts(xs, blk_e, nused, w_gate, w_up, w_down):
    m_pad, d = xs.shape
    f = w_gate.shape[2]
    tm = TM_MOE
    last = lambda r, nu: jnp.minimum(r, nu[0] - 1)
    return pl.pallas_call(
        _moe_kernel,
        grid_spec=pltpu.PrefetchScalarGridSpec(
            num_scalar_prefetch=2,
            grid=(m_pad // tm,),
            in_specs=[pl.BlockSpec((tm, d), lambda r, be, nu: (last(r, nu), 0)),
                      pl.BlockSpec((1, d, f), lambda r, be, nu: (be[last(r, nu)], 0, 0)),
                      pl.BlockSpec((1, d, f), lambda r, be, nu: (be[last(r, nu)], 0, 0)),
                      pl.BlockSpec((1, f, d), lambda r, be, nu: (be[last(r, nu)], 0, 0))],
            out_specs=pl.BlockSpec((tm, d), lambda r, be, nu: (r, 0))),
        out_shape=jax.ShapeDtypeStruct((m_pad, d), F32),
        compiler_params=_params(("arbitrary",), 56),
        name="moe_experts",
    )(blk_e, nused, xs, w_gate, w_up, w_down)


def _combine_kernel(dest_ref, x1_ref, w_ref, mod_ref, o_ref, y_ref, g0_ref, g1_ref, sem):
    tc = x1_ref.shape[0]
    bufs = (g0_ref, g1_ref)

    def start(t, carry):
        for k in range(2):
            _start_row_copy(o_ref, dest_ref[0, k, t], bufs[k], t, sem, priority=k)
        return carry

    lax.fori_loop(0, tc, start, 0, unroll=DMA_UNROLL)
    for k in range(2):
        pltpu.make_async_copy(o_ref.at[pl.ds(0, tc), :], bufs[k], sem).wait()
    moe = g0_ref[...] * w_ref[:, 0:1] + g1_ref[...] * w_ref[:, 1:2]
    y_ref[...] = x1_ref[...] + mod_ref[0, 5:6, :] * moe


def _combine(x1, dest, w_cols, mod, o_rows):
    n, d = x1.shape
    b, _, t = dest.shape
    tc = min(TC_COMBINE, t)
    per = t // tc
    return pl.pallas_call(
        _combine_kernel,
        grid=(n // tc,),
        in_specs=[pl.BlockSpec((1, 2, tc), lambda i: (i // per, 0, i % per), memory_space=pltpu.SMEM),
                  pl.BlockSpec((tc, d), lambda i: (i, 0)),
                  pl.BlockSpec((tc, 2), lambda i: (i, 0)),
                  pl.BlockSpec((1, 6, d), lambda i: (i // per, 0, 0)),
                  pl.BlockSpec(memory_space=pl.ANY)],
        out_specs=pl.BlockSpec((tc, d), lambda i: (i, 0)),
        scratch_shapes=[pltpu.VMEM((tc, d), F32), pltpu.VMEM((tc, d), F32), pltpu.SemaphoreType.DMA(())],
        out_shape=jax.ShapeDtypeStruct((n, d), F32),
        compiler_params=_params(("arbitrary",), 32),
        name="combine",
    )(dest, x1, w_cols, mod, o_rows)


def _rope_tables(t):
    rows = t // GRID_W
    row = jnp.repeat(jnp.arange(rows, dtype=F32), GRID_W)
    col = jnp.tile(jnp.arange(GRID_W, dtype=F32), rows)
    freqs = ROPE_THETA ** (-jnp.arange(ROPE_PAIRS, dtype=F32) / ROPE_PAIRS)
    ar = row[:, None] * freqs
    ac = col[:, None] * freqs
    cos_t = jnp.concatenate([jnp.cos(ar), jnp.cos(ar), jnp.cos(ac), jnp.cos(ac)], axis=-1)
    sin_t = jnp.concatenate([-jnp.sin(ar), jnp.sin(ar), -jnp.sin(ac), jnp.sin(ac)], axis=-1)
    return cos_t, sin_t


def _encoder_layer(x, mod, p):
    b, t, d = x.shape
    n = b * t
    cos_t, sin_t = _rope_tables(t)
    aq, ak, av_t, mq, mk_t, mv, mo, mg = _in_projection(
        x, mod, p["norm1_w"], p["w_main"], p["w_v_t"], p["w_gates"], p["q_norm_w"], p["k_norm_w"], cos_t, sin_t)
    att = _attention(aq, ak, av_t)
    mg_t = jnp.swapaxes(mg[:, :, :C_MG], 1, 2)
    h_fw, h_bw = _mlstm(mq, mk_t, mv, mg, mg_t, p["gate_bias_col"], p["gate_bias_row"])
    x1, logits_t = _mix_and_router(x, att, h_fw, h_bw, mo, p["ml_norm_w"], p["w_out"], mod,
                                       p["norm2_w"], p["wr_hi"], p["wr_lo"], p["b_router"])
    idx, wts, counts = _route(logits_t)

    tm = TM_MOE
    counts = counts[:, 0]
    padded = (counts + tm - 1) // tm * tm
    pend = jnp.cumsum(padded)
    pstart = (pend - padded).astype(I32)
    nb = (2 * n + tm - 1) // tm + N_EXPERTS
    m_pad = nb * tm
    block_row0 = jnp.arange(nb, dtype=I32) * tm
    blk_e = jnp.minimum(jnp.sum(pend[None, :] <= block_row0[:, None], axis=1), N_EXPERTS - 1).astype(I32)
    nused = (pend[-1:] // tm).astype(I32)
    onehot = idx[:, 0:2, :, None] == jnp.arange(N_EXPERTS, dtype=I32)
    dest = jnp.sum(jnp.where(onehot, pstart, 0), axis=-1) + idx[:, 2:4, :]

    pad_lo = (pstart + counts).astype(I32)
    pad_hi = jnp.concatenate([pstart[1:], jnp.full((1,), m_pad, I32)])
    x1 = x1.reshape(n, d)
    xs = _dispatch(x1, dest, p["norm2_w"], mod, pad_lo, pad_hi, m_pad)
    o_rows = _moe_experts(xs, blk_e, nused, p["w_gate"], p["w_up"], p["w_down"])
    w_cols = jnp.swapaxes(wts[:, :2, :], 1, 2).reshape(n, 2)
    y = _combine(x1, dest, w_cols, mod, o_rows)
    return y.reshape(b, t, d)


def kernel(x_prompt, x_sample, c_prompt, c_sample, norm1_w, norm2_w, w_ada, b_ada, w_in, q_norm_w, k_norm_w, b_igate, b_fgate, ml_norm_w, w_out, w_gr, b_gr, w_er, b_er, w_gate, w_up, w_down):
    depth = norm1_w.shape[0]
    d = x_prompt.shape[-1]
    bp = x_prompt.shape[0]
    bs = x_sample.shape[0]
    rows = -(-(bp + bs) // 8) * 8
    y_prompt, y_sample = x_prompt, x_sample
    for l in range(depth):
        c_pad = jnp.zeros((rows, d), F32).at[:bp].set(c_prompt).at[bp:bp + bs].set(c_sample)
        mod = _ada_modulation(c_pad, w_ada[l], b_ada[l]).reshape(rows, 6, d)
        gate_bias = jnp.concatenate([b_igate[l].reshape(-1), b_fgate[l].reshape(-1)])
        w_router = jnp.concatenate([w_gr[l], w_er[l]], axis=1).T
        w_router = jnp.zeros((ROUTER_ROWS, d), F32).at[:N_GROUPS + N_EXPERTS].set(w_router)
        wr_hi = w_router.astype(BF16)
        b_router = jnp.zeros((ROUTER_ROWS,), F32).at[:N_GROUPS + N_EXPERTS].set(jnp.concatenate([b_gr[l], b_er[l]]))
        p = {
            "norm1_w": norm1_w[l].reshape(1, d),
            "norm2_w": norm2_w[l].reshape(1, d),
            "w_main": w_in[l][:, :C_MAIN].astype(BF16),
            "w_v_t": jnp.concatenate([w_in[l][:, C_AQ + C_AK:C_AQ + C_AK + C_AV],
                                      w_in[l][:, C_AQ + C_AK + C_AV + C_MQ:C_AQ + C_AK + C_AV + C_MQ + C_MK]],
                                     axis=1).T.astype(BF16),
            "w_gates": jnp.zeros((d, LANES), BF16).at[:, :C_MG].set(w_in[l][:, C_MAIN:].astype(BF16)),
            "q_norm_w": q_norm_w[l].reshape(1, HEAD_DIM),
            "k_norm_w": k_norm_w[l].reshape(1, HEAD_DIM),
            "gate_bias_col": jnp.zeros((1, LANES), F32).at[0, :C_MG].set(gate_bias),
            "gate_bias_row": jnp.broadcast_to(gate_bias[:, None], (C_MG, ML_CHUNK)),
            "ml_norm_w": ml_norm_w[l].reshape(1, C_MV),
            "w_out": w_out[l].astype(BF16),
            "wr_hi": wr_hi,
            "wr_lo": (w_router - wr_hi.astype(F32)).astype(BF16),
            "b_router": jnp.broadcast_to(b_router[:, None], (ROUTER_ROWS, LANES)),
            "w_gate": w_gate[l].astype(BF16),
            "w_up": w_up[l].astype(BF16),
            "w_down": w_down[l].astype(BF16),
        }
        y_prompt = _encoder_layer(y_prompt, mod[:bp], p)
        y_sample = _encoder_layer(y_sample, mod[bp:bp + bs], p)
    return (y_prompt, y_sample)
```

```python
import functools

import jax
import jax.numpy as jnp
import numpy as np
from jax import lax
from jax.experimental import pallas as pl
from jax.experimental.pallas import tpu as pltpu

F32 = jnp.float32
BF16 = jnp.bfloat16
I32 = jnp.int32

GRID_W = 64
HEAD_DIM = 128
ATT_HEADS = 8
ATT_KV_HEADS = 2
ATT_GROUP = ATT_HEADS // ATT_KV_HEADS
ROPE_THETA = 10000.0
ROPE_PAIRS = HEAD_DIM // 4
ML_HEADS = 4
ML_DV = 256
ML_DQK = 128
ML_CHUNK = 128
N_GROUPS = 4
EXPERTS_PER_GROUP = 4
N_EXPERTS = N_GROUPS * EXPERTS_PER_GROUP
EPS = 1e-6
Q_SCALE = HEAD_DIM ** -0.5 * float(np.log2(np.e))

C_AQ = ATT_HEADS * HEAD_DIM
C_AK = ATT_KV_HEADS * HEAD_DIM
C_AV = ATT_KV_HEADS * HEAD_DIM
C_MQ = ML_HEADS * ML_DQK
C_MK = ML_HEADS * ML_DQK
C_MV = ML_HEADS * ML_DV
C_MO = ML_HEADS * ML_DV
C_MG = 4 * ML_HEADS
C_MAIN = C_AQ + C_AK + C_AV + C_MQ + C_MK + C_MV + C_MO

LANES = 128
ROUTER_ROWS = 32
MIB = 1024 * 1024

TM_PROJ = 256
TQ_ATT = 256
TM_MIX = 512
TL_ROUTE = 512
TD_DISPATCH = 256
TM_MOE = 256
TC_COMBINE = 256
ML_SEQS = 2
ATT_SOFTMAX_LAG = 2
ATT_VALUES_LAG = 1
ATT_SLOTS = 4
DMA_UNROLL = 8


def _params(semantics, vmem_mib):
    return pltpu.CompilerParams(dimension_semantics=semantics, vmem_limit_bytes=vmem_mib * MIB)


def _resident(shape):
    nd = len(shape)
    return pl.BlockSpec(shape, lambda *_: (0,) * nd, pipeline_mode=pl.Buffered(1))


def _ada_kernel(c_ref, w_ref, b_ref, o_ref):
    c = c_ref[...]
    s = (c * jax.nn.sigmoid(c)).astype(BF16)
    o_ref[...] = jnp.dot(s, w_ref[...].astype(BF16), preferred_element_type=F32) + b_ref[...]


def _ada_modulation(c_pad, w_ada, b_ada):
    rows, d = c_pad.shape
    n = w_ada.shape[1]
    tn = 1024
    return pl.pallas_call(
        _ada_kernel,
        grid=(n // tn,),
        in_specs=[pl.BlockSpec((rows, d), lambda j: (0, 0)),
                  pl.BlockSpec((d, tn), lambda j: (0, j)),
                  pl.BlockSpec((1, tn), lambda j: (0, j))],
        out_specs=pl.BlockSpec((rows, tn), lambda j: (0, j)),
        out_shape=jax.ShapeDtypeStruct((rows, n), F32),
        compiler_params=_params(("arbitrary",), 40),
        name="ada_modulation",
    )(c_pad, w_ada, b_ada.reshape(1, n))


def _rms(x, w):
    return x * lax.rsqrt(jnp.mean(x * x, axis=-1, keepdims=True) + EPS) * w


def _inproj_kernel(x_ref, mod_ref, n1_ref, w_ref, wvt_ref, wg_ref, qn_ref, kn_ref, cos_ref, sin_ref,
                   aq_ref, ak_ref, avt_ref, mq_ref, mkt_ref, mv_ref, mo_ref, mg_ref):
    x = x_ref[0]
    h = _rms(x, n1_ref[...]) * (1.0 + mod_ref[0, 1:2, :]) + mod_ref[0, 0:1, :]
    hb = h.astype(BF16)

    def proj(c0, width):
        return jnp.dot(hb, w_ref[:, c0:c0 + width], preferred_element_type=F32)

    cos = cos_ref[...]
    sin = sin_ref[...]
    lane = lax.broadcasted_iota(I32, (1, HEAD_DIM), 1)
    first = (lane % (2 * ROPE_PAIRS)) < ROPE_PAIRS

    def norm_rope(p, w):
        pn = _rms(p, w)
        partner = jnp.where(first, pltpu.roll(pn, HEAD_DIM - ROPE_PAIRS, 1), pltpu.roll(pn, ROPE_PAIRS, 1))
        return pn * cos + partner * sin

    c0 = 0
    for half in range(2):
        p = proj(c0, C_AQ // 2)
        for hh in range(ATT_HEADS // 2):
            col = half * (C_AQ // 2) + hh * HEAD_DIM
            qh = norm_rope(p[:, hh * HEAD_DIM:(hh + 1) * HEAD_DIM], qn_ref[...])
            aq_ref[0, :, col:col + HEAD_DIM] = (qh * Q_SCALE).astype(BF16)
        c0 += C_AQ // 2
    p = proj(c0, C_AK)
    for hh in range(ATT_KV_HEADS):
        ak_ref[0, :, hh * HEAD_DIM:(hh + 1) * HEAD_DIM] = norm_rope(p[:, hh * HEAD_DIM:(hh + 1) * HEAD_DIM], kn_ref[...]).astype(BF16)
    nt = (((1,), (1,)), ((), ()))
    avt_ref[0] = lax.dot_general(wvt_ref[0:C_AV, :], hb, nt, preferred_element_type=F32).astype(BF16)
    mkt_ref[0] = lax.dot_general(wvt_ref[C_AV:, :], hb, nt, preferred_element_type=F32).astype(BF16)
    c0 += C_AK + C_AV
    mq_ref[0] = (proj(c0, C_MQ) * (ML_DQK ** -0.5)).astype(BF16)
    c0 += C_MQ + C_MK
    for half in range(2):
        mv_ref[0, :, half * 512:(half + 1) * 512] = proj(c0, 512).astype(BF16)
        c0 += 512
    for half in range(2):
        mo_ref[0, :, half * 512:(half + 1) * 512] = proj(c0, 512)
        c0 += 512
    mg_ref[0] = jnp.dot(hb, wg_ref[...], preferred_element_type=F32)


def _in_projection(x, mod, norm1_w, w_main, w_v_t, w_gates, q_norm_w, k_norm_w, cos_t, sin_t):
    b, t, d = x.shape
    tm = min(TM_PROJ, t)
    row = lambda width: pl.BlockSpec((1, tm, width), lambda bi, i: (bi, i, 0))
    outs = ((C_AQ, False), (C_AK, False), (C_AV, True), (C_MQ, False), (C_MK, True), (C_MV, False),
            (C_MO, False), (LANES, False))
    dtypes = (BF16, BF16, BF16, BF16, BF16, BF16, F32, F32)
    col = lambda width: pl.BlockSpec((1, width, tm), lambda bi, i: (bi, 0, i))
    return pl.pallas_call(
        _inproj_kernel,
        grid=(b, t // tm),
        in_specs=[row(d),
                  pl.BlockSpec((1, 6, d), lambda bi, i: (bi, 0, 0)),
                  _resident((1, d)),
                  _resident((d, C_MAIN)),
                  _resident((C_AV + C_MK, d)),
                  _resident((d, LANES)),
                  _resident((1, HEAD_DIM)),
                  _resident((1, HEAD_DIM)),
                  pl.BlockSpec((tm, HEAD_DIM), lambda bi, i: (i, 0)),
                  pl.BlockSpec((tm, HEAD_DIM), lambda bi, i: (i, 0))],
        out_specs=[col(w) if tr else row(w) for w, tr in outs],
        out_shape=[jax.ShapeDtypeStruct((b, w, t) if tr else (b, t, w), dt) for (w, tr), dt in zip(outs, dtypes)],
        compiler_params=_params(("parallel", "arbitrary"), 48),
        name="in_projection",
    )(x, mod, norm1_w, w_main, w_v_t, w_gates, q_norm_w, k_norm_w, cos_t, sin_t)


def _attention_kernel(q_ref, qn_ref, k_ref, vt_ref, o_ref, s_ref, p_ref, l_ref):
    def scores(g, src_ref=q_ref):
        kv = g // ATT_GROUP
        q = src_ref[0, :, g * HEAD_DIM:(g + 1) * HEAD_DIM]
        k = k_ref[0, :, kv * HEAD_DIM:(kv + 1) * HEAD_DIM]
        s_ref[g % ATT_SLOTS] = lax.dot_general(k, q, (((1,), (1,)), ((), ())), preferred_element_type=F32)

    def softmax(g):
        s = s_ref[g % ATT_SLOTS]
        m = jnp.max(s, axis=0, keepdims=True)
        p = jnp.exp2(s - m)
        l_ref[g] = jnp.broadcast_to(jnp.sum(p, axis=0, keepdims=True), l_ref.shape[1:])
        p_ref[g % ATT_SLOTS] = p.astype(BF16)

    def values(g):
        kv = g // ATT_GROUP
        vt = vt_ref[0, kv * HEAD_DIM:(kv + 1) * HEAD_DIM, :]
        o_t = jnp.dot(vt, p_ref[g % ATT_SLOTS], preferred_element_type=F32)
        o_ref[0, :, g * HEAD_DIM:(g + 1) * HEAD_DIM] = (o_t * (1.0 / l_ref[g][0:1, :])).T.astype(BF16)

    n = ATT_HEADS
    lag = ATT_SOFTMAX_LAG

    @pl.when(pl.program_id(1) == 0)
    def _():
        for g in range(lag):
            scores(g)

    for g in range(n):
        if g + lag < n:
            scores(g + lag)
        else:
            scores(g + lag - n, qn_ref)
        softmax(g)
        if g >= ATT_VALUES_LAG:
            values(g - ATT_VALUES_LAG)
    for g in range(n - ATT_VALUES_LAG, n):
        values(g)


def _attention(aq, ak, av_t):
    b, t, _ = aq.shape
    tq = min(TQ_ATT, t)
    nt = t // tq
    return pl.pallas_call(
        _attention_kernel,
        grid=(b, nt),
        in_specs=[pl.BlockSpec((1, tq, C_AQ), lambda bi, i: (bi, i, 0)),
                  pl.BlockSpec((1, tq, C_AQ), lambda bi, i: (bi, jnp.minimum(i + 1, nt - 1), 0)),
                  pl.BlockSpec((1, t, C_AK), lambda bi, i: (bi, 0, 0)),
                  pl.BlockSpec((1, C_AV, t), lambda bi, i: (bi, 0, 0))],
        out_specs=pl.BlockSpec((1, tq, C_AQ), lambda bi, i: (bi, i, 0)),
        out_shape=jax.ShapeDtypeStruct((b, t, C_AQ), BF16),
        scratch_shapes=[pltpu.VMEM((ATT_SLOTS, t, tq), F32), pltpu.VMEM((ATT_SLOTS, t, tq), BF16),
                        pltpu.VMEM((ATT_HEADS, 8, tq), F32)],
        compiler_params=_params(("parallel", "arbitrary"), 48),
        name="attention",
    )(aq, aq, ak, av_t)


def _log_sigmoid(x):
    return jnp.minimum(x, 0.0) - jnp.log1p(jnp.exp(-jnp.abs(x)))


def _split3(a):
    a1 = a.astype(BF16)
    r1 = a - a1.astype(F32)
    a2 = r1.astype(BF16)
    a3 = (r1 - a2.astype(F32)).astype(BF16)
    return a1, a2, a3


def _lane_pick(a, idx):
    lane = lax.broadcasted_iota(I32, a.shape, 1)
    return jnp.sum(jnp.where(lane == idx, a, 0.0), axis=-1, keepdims=True)


def _mlstm_kernel(qf, kf, vf, gcf, grf, qb, kb, vb, gcb, grb, bc_ref, br_ref, hf_ref, hb_ref,
                  ct_ref, m_ref):
    @pl.when(pl.program_id(1) == 0)
    def _():
        ct_ref[...] = jnp.zeros_like(ct_ref)
        m_ref[...] = jnp.zeros_like(m_ref)

    L = ML_CHUNK
    row = lax.broadcasted_iota(I32, (L, L), 0)
    col = lax.broadcasted_iota(I32, (L, L), 1)
    lower = (col <= row)
    upper = (col >= row)
    lower_b = lower.astype(BF16)
    upper_b = upper.astype(BF16)

    gates = {}
    for bb in range(qf.shape[0]):
        for d, (gc_ref, gr_ref) in enumerate(((gcf, grf), (gcb, grb))):
            cum_cols = upper_b if d else lower_b
            cum_rows = lower_b if d else upper_b
            pre_c = gc_ref[bb] + bc_ref[...]
            pre_r = gr_ref[bb] + br_ref[...]
            b_cols = sum(jnp.dot(cum_cols, part, preferred_element_type=F32) for part in _split3(_log_sigmoid(pre_c)))
            b_rows = sum(jnp.dot(part, cum_rows, preferred_element_type=F32) for part in _split3(_log_sigmoid(pre_r)))
            gates[bb, d] = (pre_r, b_cols, b_rows)

    chains = [(bb, d, hh) for bb in range(qf.shape[0]) for d in range(2) for hh in range(ML_HEADS)]
    refs = ((qf, kf, vf, hf_ref), (qb, kb, vb, hb_ref))
    ones_block = jnp.ones((L, LANES), BF16)


    st = {}
    for (bb, d, hh) in chains:
        q_ref, kt_ref, v_ref, _ = refs[d]
        pre_r, b_cols, b_rows = gates[bb, d]
        ci = d * ML_HEADS + hh
        cf = 2 * ML_HEADS + d * ML_HEADS + hh
        sidx = (bb * 2 + d) * ML_HEADS + hh
        i_row = pre_r[ci:ci + 1, :]
        b_row = b_rows[cf:cf + 1, :]
        m = m_ref[sidx][0:1, 0:1]
        c_vis = jnp.where(upper if d else lower, i_row - b_row, -jnp.inf)
        shift = -jnp.maximum(m, jnp.max(c_vis, axis=-1, keepdims=True))
        dec = jnp.exp(c_vis + shift)
        a_inter = jnp.exp(m + shift)
        floor = jnp.exp(shift - _lane_pick(b_cols, cf))
        q = q_ref[bb, :, hh * ML_DQK:(hh + 1) * ML_DQK]
        kt = kt_ref[bb, hh * ML_DQK:(hh + 1) * ML_DQK, :]
        v_aug = jnp.concatenate([v_ref[bb, :, hh * ML_DV:(hh + 1) * ML_DV], ones_block], axis=1)
        qk = jnp.dot(q, kt, preferred_element_type=F32)
        ct = ct_ref[sidx]
        inter = jnp.dot(q, ct.astype(BF16), preferred_element_type=F32)
        b_end = _lane_pick(b_row, 0 if d else L - 1)
        g_row = b_end - b_row + i_row
        m_new = jnp.maximum(b_end + m, jnp.max(g_row, axis=-1, keepdims=True))
        decay = jnp.exp(b_end + m - m_new)
        w_row = jnp.exp(g_row - m_new)
        upd = jnp.dot((kt.astype(F32) * w_row).astype(BF16), v_aug, preferred_element_type=F32)
        st[bb, d, hh] = (sidx, dec, a_inter, floor, v_aug, qk, ct, inter, m_new, decay, upd)

    sv = {}
    for key in chains:
        sidx, dec, a_inter, floor, v_aug, qk, ct, inter, m_new, decay, upd = st[key]
        sv[key] = jnp.dot((qk * dec).astype(BF16), v_aug, preferred_element_type=F32)

    for key in chains:
        bb, d, hh = key
        sidx, dec, a_inter, floor, v_aug, qk, ct, inter, m_new, decay, upd = st[key]
        both = a_inter * inter + sv[key]
        scale = 1.0 / jnp.maximum(jnp.abs(both[:, ML_DV:]), floor)
        h_ref = refs[d][3]
        for blk in range(ML_DV // LANES):
            lo = hh * ML_DV + blk * LANES
            h_ref[bb, :, lo:lo + LANES] = both[:, blk * LANES:(blk + 1) * LANES] * scale
        ct_ref[sidx] = decay * ct + upd
        m_ref[sidx] = jnp.broadcast_to(m_new, (8, LANES))


def _mlstm(mq, mk_t, mv, mg, mg_t, bias_col, bias_row):
    b, t, _ = mq.shape
    L = ML_CHUNK
    nc = t // L
    sb = ML_SEQS if b % ML_SEQS == 0 else 1
    fw = lambda width: pl.BlockSpec((sb, L, width), lambda bi, c: (bi, c, 0))
    bw = lambda width: pl.BlockSpec((sb, L, width), lambda bi, c: (bi, nc - 1 - c, 0))
    fw_t = lambda rows: pl.BlockSpec((sb, rows, L), lambda bi, c: (bi, 0, c))
    bw_t = lambda rows: pl.BlockSpec((sb, rows, L), lambda bi, c: (bi, 0, nc - 1 - c))
    ns = sb * 2 * ML_HEADS
    return pl.pallas_call(
        _mlstm_kernel,
        grid=(b // sb, nc),
        in_specs=[fw(C_MQ), fw_t(C_MK), fw(C_MV), fw(LANES), fw_t(C_MG),
                  bw(C_MQ), bw_t(C_MK), bw(C_MV), bw(LANES), bw_t(C_MG),
                  _resident((1, LANES)), _resident((C_MG, L))],
        out_specs=[fw(C_MV), bw(C_MV)],
        out_shape=[jax.ShapeDtypeStruct((b, t, C_MV), F32)] * 2,
        scratch_shapes=[pltpu.VMEM((ns, ML_DQK, ML_DV + LANES), F32),
                        pltpu.VMEM((ns, 8, LANES), F32)],
        compiler_params=_params(("parallel", "arbitrary"), 32),
        name="mlstm",
    )(mq, mk_t, mv, mg, mg_t, mq, mk_t, mv, mg, mg_t, bias_col, bias_row)


def _moe_input(x1, n2_ref, mod_ref):
    return _rms(x1, n2_ref[...]) * (1.0 + mod_ref[0, 4:5, :]) + mod_ref[0, 3:4, :]


def _mix_kernel(x_ref, att_ref, hf_ref, hb_ref, mo_ref, mlw_ref, wo_ref, mod_ref, n2_ref,
                wrh_ref, wrl_ref, br_ref, x1_ref, lg_ref):
    ml = hf_ref[0] + hb_ref[0]
    gate = jax.nn.sigmoid(mo_ref[0])
    mix = jnp.dot(att_ref[0], wo_ref[0:C_AQ, :], preferred_element_type=F32)
    for hh in range(ML_HEADS):
        sl = slice(hh * ML_DV, (hh + 1) * ML_DV)
        seg = (_rms(ml[:, sl], mlw_ref[:, sl]) * gate[:, sl]).astype(BF16)
        mix += jnp.dot(seg, wo_ref[C_AQ + hh * ML_DV:C_AQ + (hh + 1) * ML_DV, :], preferred_element_type=F32)
    x1 = x_ref[0] + mod_ref[0, 2:3, :] * mix
    x1_ref[0] = x1
    h2 = _moe_input(x1, n2_ref, mod_ref)
    h_hi = h2.astype(BF16)
    h_lo = (h2 - h_hi.astype(F32)).astype(BF16)
    nt = (((1,), (1,)), ((), ()))
    both = lax.dot_general(jnp.concatenate([wrh_ref[...], wrl_ref[...]], axis=0), h_hi, nt, preferred_element_type=F32)
    lg = both[:ROUTER_ROWS] + both[ROUTER_ROWS:] + lax.dot_general(wrh_ref[...], h_lo, nt, preferred_element_type=F32)
    lg_ref[0] = lg + br_ref[:, 0:1]


def _mix_and_router(x, att, h_fw, h_bw, mo, ml_norm_w, w_out, mod, norm2_w, wr_hi, wr_lo, b_router):
    b, t, d = x.shape
    tm = min(TM_MIX, t)
    row = lambda width: pl.BlockSpec((1, tm, width), lambda bi, i: (bi, i, 0))
    return pl.pallas_call(
        _mix_kernel,
        grid=(b, t // tm),
        in_specs=[row(d), row(C_AQ), row(C_MV), row(C_MV), row(C_MO),
                  _resident((1, C_MV)), _resident((d, d)),
                  pl.BlockSpec((1, 6, d), lambda bi, i: (bi, 0, 0)),
                  _resident((1, d)), _resident((ROUTER_ROWS, d)), _resident((ROUTER_ROWS, d)),
                  _resident((ROUTER_ROWS, LANES))],
        out_specs=[row(d), pl.BlockSpec((1, ROUTER_ROWS, tm), lambda bi, i: (bi, 0, i))],
        out_shape=[jax.ShapeDtypeStruct((b, t, d), F32), jax.ShapeDtypeStruct((b, ROUTER_ROWS, t), F32)],
        compiler_params=_params(("parallel", "arbitrary"), 48),
        name="mix_router",
    )(x, att, h_fw, h_bw, mo, ml_norm_w, w_out, mod, norm2_w, wr_hi, wr_lo, b_router)


def _first_argmax(rows):
    best = rows[0]
    idx = jnp.zeros_like(best)
    for j in range(1, len(rows)):
        better = rows[j] > best
        best = jnp.where(better, rows[j], best)
        idx = jnp.where(better, float(j), idx)
    return best, idx


def _softmax_rows(rows):
    mx = functools.reduce(jnp.maximum, rows)
    ex = [jnp.exp(r - mx) for r in rows]
    tot = functools.reduce(lambda a, c: a + c, ex)
    return [e / tot for e in ex]


def _route_kernel(lg_ref, idx_ref, wt_ref, cnt_ref, run_ref):
    @pl.when((pl.program_id(0) == 0) & (pl.program_id(1) == 0))
    def _():
        run_ref[...] = jnp.zeros_like(run_ref)

    tl = lg_ref.shape[2]
    lg = lg_ref[0]
    p_grp = _softmax_rows([lg[g:g + 1, :] for g in range(N_GROUPS)])
    p_g, g_idx = _first_argmax(p_grp)
    el = []
    for j in range(EXPERTS_PER_GROUP):
        sel = lg[N_GROUPS + (N_GROUPS - 1) * EXPERTS_PER_GROUP + j:N_GROUPS + (N_GROUPS - 1) * EXPERTS_PER_GROUP + j + 1, :]
        for g in range(N_GROUPS - 2, -1, -1):
            r = N_GROUPS + g * EXPERTS_PER_GROUP + j
            sel = jnp.where(g_idx == float(g), lg[r:r + 1, :], sel)
        el.append(sel)
    pe = _softmax_rows(el)
    w1, i1 = _first_argmax(pe)
    rest = [jnp.where(i1 == float(j), -1.0, pe[j]) for j in range(EXPERTS_PER_GROUP)]
    w2, i2 = _first_argmax(rest)
    tot = w1 + w2
    wt1 = w1 / tot * p_g
    wt2 = w2 / tot * p_g
    e1 = (g_idx * float(EXPERTS_PER_GROUP) + i1).astype(I32)
    e2 = (g_idx * float(EXPERTS_PER_GROUP) + i2).astype(I32)

    erow = lax.broadcasted_iota(I32, (N_EXPERTS, tl), 0)
    hit1 = erow == e1
    hit2 = erow == e2
    onehot = (hit1 | hit2).astype(F32)
    before = lax.broadcasted_iota(I32, (tl, tl), 0) <= lax.broadcasted_iota(I32, (tl, tl), 1)
    incl = jnp.dot(onehot.astype(BF16), before.astype(BF16), preferred_element_type=F32)
    base = run_ref[:, 0:1] + incl - onehot
    r1 = jnp.sum(jnp.where(hit1, base, 0.0), axis=0, keepdims=True).astype(I32)
    r2 = jnp.sum(jnp.where(hit2, base, 0.0), axis=0, keepdims=True).astype(I32)
    total = run_ref[...] + jnp.sum(onehot, axis=-1, keepdims=True)
    run_ref[...] = total
    cnt_ref[...] = total.astype(I32)

    zi = jnp.zeros((4, tl), I32)
    idx_ref[0] = jnp.concatenate([e1, e2, r1, r2, zi], axis=0)
    wt_ref[0] = jnp.concatenate([wt1, wt2, jnp.zeros((6, tl), F32)], axis=0)


def _route(logits_t):
    b, _, t = logits_t.shape
    tl = min(TL_ROUTE, t)
    blk = lambda rows: pl.BlockSpec((1, rows, tl), lambda bi, i: (bi, 0, i))
    return pl.pallas_call(
        _route_kernel,
        grid=(b, t // tl),
        in_specs=[blk(ROUTER_ROWS)],
        out_specs=[blk(8), blk(8), pl.BlockSpec((N_EXPERTS, LANES), lambda bi, i: (0, 0))],
        out_shape=[jax.ShapeDtypeStruct((b, 8, t), I32), jax.ShapeDtypeStruct((b, 8, t), F32),
                   jax.ShapeDtypeStruct((N_EXPERTS, LANES), I32)],
        scratch_shapes=[pltpu.VMEM((N_EXPERTS, LANES), F32)],
        compiler_params=_params(("arbitrary", "arbitrary"), 32),
        name="route",
    )(logits_t)


def _row_copy(src_ref, src_row, dst_ref, dst_row, sem):
    return pltpu.make_async_copy(src_ref.at[pl.ds(src_row, 1), :], dst_ref.at[pl.ds(dst_row, 1), :], sem)


def _start_row_copy(src_ref, src_row, dst_ref, dst_row, sem, priority):
    pltpu.async_copy(src_ref.at[pl.ds(src_row, 1), :], dst_ref.at[pl.ds(dst_row, 1), :], sem, priority=priority)


def _dispatch_kernel(padlo_ref, padhi_ref, dest_ref, x1_ref, n2_ref, mod_ref, xs_ref, h_ref, zero_ref, sem):
    td = x1_ref.shape[0]
    h_ref[...] = _moe_input(x1_ref[...], n2_ref, mod_ref)

    def start(t, carry):
        for k in range(2):
            _start_row_copy(h_ref, t, xs_ref, dest_ref[0, k, t], sem, priority=k)
        return carry

    lax.fori_loop(0, td, start, 0, unroll=DMA_UNROLL)
    for _ in range(2):
        pltpu.make_async_copy(h_ref, xs_ref.at[pl.ds(0, td), :], sem).wait()

    @pl.when(pl.program_id(0) == pl.num_programs(0) - 1)
    def _():
        zero_ref[...] = jnp.zeros_like(zero_ref)

        def zero_start(r, carry):
            _row_copy(zero_ref, 0, xs_ref, r, sem).start()
            return carry

        def zero_wait(r, carry):
            _row_copy(zero_ref, 0, xs_ref, r, sem).wait()
            return carry

        for e in range(N_EXPERTS):
            lax.fori_loop(padlo_ref[e], padhi_ref[e], zero_start, 0)
        for e in range(N_EXPERTS):
            lax.fori_loop(padlo_ref[e], padhi_ref[e], zero_wait, 0)


def _dispatch(x1, dest, norm2_w, mod, pad_lo, pad_hi, m_pad):
    n, d = x1.shape
    b, _, t = dest.shape
    td = min(TD_DISPATCH, t)
    per = t // td
    return pl.pallas_call(
        _dispatch_kernel,
        grid_spec=pltpu.PrefetchScalarGridSpec(
            num_scalar_prefetch=2,
            grid=(n // td,),
            in_specs=[pl.BlockSpec((1, 2, td), lambda i, *_: (i // per, 0, i % per), memory_space=pltpu.SMEM),
                      pl.BlockSpec((td, d), lambda i, *_: (i, 0)),
                      pl.BlockSpec((1, d), lambda i, *_: (0, 0)),
                      pl.BlockSpec((1, 6, d), lambda i, *_: (i // per, 0, 0))],
            out_specs=pl.BlockSpec(memory_space=pl.ANY),
            scratch_shapes=[pltpu.VMEM((td, d), F32), pltpu.VMEM((8, d), F32), pltpu.SemaphoreType.DMA(())]),
        out_shape=jax.ShapeDtypeStruct((m_pad, d), F32),
        compiler_params=_params(("arbitrary",), 32),
        name="dispatch",
    )(pad_lo, pad_hi, dest, x1, norm2_w, mod)


def _moe_kernel(blk_e_ref, nused_ref, x_ref, wg_ref, wu_ref, wd_ref, o_ref):
    r = pl.program_id(0)

    @pl.when(r < nused_ref[0])
    def _():
        xb = x_ref[...].astype(BF16)
        g = jnp.dot(xb, wg_ref[0], preferred_element_type=F32)
        u = jnp.dot(xb, wu_ref[0], preferred_element_type=F32)
        h = (g * jax.nn.sigmoid(g) * u).astype(BF16)
        o_ref[...] = jnp.dot(h, wd_ref[0], preferred_element_type=F32)

    @pl.when(r >= nused_ref[0])
    def _():
        o_ref[...] = jnp.zeros_like(o_ref)


def _moe_experts(xs, blk_e, nused, w_gate, w_up, w_down):
    m_pad, d = xs.shape
    f = w_gate.shape[2]
    tm = TM_MOE
    last = lambda r, nu: jnp.minimum(r, nu[0] - 1)
    return pl.pallas_call(
        _moe_kernel,
        grid_spec=pltpu.PrefetchScalarGridSpec(
            num_scalar_prefetch=2,
            grid=(m_pad // tm,),
            in_specs=[pl.BlockSpec((tm, d), lambda r, be, nu: (last(r, nu), 0)),
                      pl.BlockSpec((1, d, f), lambda r, be, nu: (be[last(r, nu)], 0, 0)),
                      pl.BlockSpec((1, d, f), lambda r, be, nu: (be[last(r, nu)], 0, 0)),
                      pl.BlockSpec((1, f, d), lambda r, be, nu: (be[last(r, nu)], 0, 0))],
            out_specs=pl.BlockSpec((tm, d), lambda r, be, nu: (r, 0))),
        out_shape=jax.ShapeDtypeStruct((m_pad, d), F32),
        compiler_params=_params(("arbitrary",), 56),
        name="moe_experts",
    )(blk_e, nused, xs, w_gate, w_up, w_down)


def _combine_kernel(dest_ref, x1_ref, w_ref, mod_ref, o_ref, y_ref, g0_ref, g1_ref, sem):
    tc = x1_ref.shape[0]
    bufs = (g0_ref, g1_ref)

    def start(t, carry):
        for k in range(2):
            _start_row_copy(o_ref, dest_ref[0, k, t], bufs[k], t, sem, priority=k)
        return carry

    lax.fori_loop(0, tc, start, 0, unroll=DMA_UNROLL)
    for k in range(2):
        pltpu.make_async_copy(o_ref.at[pl.ds(0, tc), :], bufs[k], sem).wait()
    moe = g0_ref[...] * w_ref[:, 0:1] + g1_ref[...] * w_ref[:, 1:2]
    y_ref[...] = x1_ref[...] + mod_ref[0, 5:6, :] * moe


def _combine(x1, dest, w_cols, mod, o_rows):
    n, d = x1.shape
    b, _, t = dest.shape
    tc = min(TC_COMBINE, t)
    per = t // tc
    return pl.pallas_call(
        _combine_kernel,
        grid=(n // tc,),
        in_specs=[pl.BlockSpec((1, 2, tc), lambda i: (i // per, 0, i % per), memory_space=pltpu.SMEM),
                  pl.BlockSpec((tc, d), lambda i: (i, 0)),
                  pl.BlockSpec((tc, 2), lambda i: (i, 0)),
                  pl.BlockSpec((1, 6, d), lambda i: (i // per, 0, 0)),
                  pl.BlockSpec(memory_space=pl.ANY)],
        out_specs=pl.BlockSpec((tc, d), lambda i: (i, 0)),
        scratch_shapes=[pltpu.VMEM((tc, d), F32), pltpu.VMEM((tc, d), F32), pltpu.SemaphoreType.DMA(())],
        out_shape=jax.ShapeDtypeStruct((n, d), F32),
        compiler_params=_params(("arbitrary",), 32),
        name="combine",
    )(dest, x1, w_cols, mod, o_rows)


def _rope_tables(t):
    rows = t // GRID_W
    row = jnp.repeat(jnp.arange(rows, dtype=F32), GRID_W)
    col = jnp.tile(jnp.arange(GRID_W, dtype=F32), rows)
    freqs = ROPE_THETA ** (-jnp.arange(ROPE_PAIRS, dtype=F32) / ROPE_PAIRS)
    ar = row[:, None] * freqs
    ac = col[:, None] * freqs
    cos_t = jnp.concatenate([jnp.cos(ar), jnp.cos(ar), jnp.cos(ac), jnp.cos(ac)], axis=-1)
    sin_t = jnp.concatenate([-jnp.sin(ar), jnp.sin(ar), -jnp.sin(ac), jnp.sin(ac)], axis=-1)
    return cos_t, sin_t


def _encoder_layer(x, mod, p):
    b, t, d = x.shape
    n = b * t
    cos_t, sin_t = _rope_tables(t)
    aq, ak, av_t, mq, mk_t, mv, mo, mg = _in_projection(
        x, mod, p["norm1_w"], p["w_main"], p["w_v_t"], p["w_gates"], p["q_norm_w"], p["k_norm_w"], cos_t, sin_t)
    att = _attention(aq, ak, av_t)
    mg_t = jnp.swapaxes(mg[:, :, :C_MG], 1, 2)
    h_fw, h_bw = _mlstm(mq, mk_t, mv, mg, mg_t, p["gate_bias_col"], p["gate_bias_row"])
    x1, logits_t = _mix_and_router(x, att, h_fw, h_bw, mo, p["ml_norm_w"], p["w_out"], mod,
                                       p["norm2_w"], p["wr_hi"], p["wr_lo"], p["b_router"])
    idx, wts, counts = _route(logits_t)

    tm = TM_MOE
    counts = counts[:, 0]
    padded = (counts + tm - 1) // tm * tm
    pend = jnp.cumsum(padded)
    pstart = (pend - padded).astype(I32)
    nb = (2 * n + tm - 1) // tm + N_EXPERTS
    m_pad = nb * tm
    block_row0 = jnp.arange(nb, dtype=I32) * tm
    blk_e = jnp.minimum(jnp.sum(pend[None, :] <= block_row0[:, None], axis=1), N_EXPERTS - 1).astype(I32)
    nused = (pend[-1:] // tm).astype(I32)
    onehot = idx[:, 0:2, :, None] == jnp.arange(N_EXPERTS, dtype=I32)
    dest = jnp.sum(jnp.where(onehot, pstart, 0), axis=-1) + idx[:, 2:4, :]

    pad_lo = (pstart + counts).astype(I32)
    pad_hi = jnp.concatenate([pstart[1:], jnp.full((1,), m_pad, I32)])
    x1 = x1.reshape(n, d)
    xs = _dispatch(x1, dest, p["norm2_w"], mod, pad_lo, pad_hi, m_pad)
    o_rows = _moe_experts(xs, blk_e, nused, p["w_gate"], p["w_up"], p["w_down"])
    w_cols = jnp.swapaxes(wts[:, :2, :], 1, 2).reshape(n, 2)
    y = _combine(x1, dest, w_cols, mod, o_rows)
    return y.reshape(b, t, d)


def kernel(x_prompt, x_sample, c_prompt, c_sample, norm1_w, norm2_w, w_ada, b_ada, w_in, q_norm_w, k_norm_w, b_igate, b_fgate, ml_norm_w, w_out, w_gr, b_gr, w_er, b_er, w_gate, w_up, w_down):
    depth = norm1_w.shape[0]
    d = x_prompt.shape[-1]
    bp = x_prompt.shape[0]
    bs = x_sample.shape[0]
    rows = -(-(bp + bs) // 8) * 8
    y_prompt, y_sample = x_prompt, x_sample
    for l in range(depth):
        c_pad = jnp.zeros((rows, d), F32).at[:bp].set(c_prompt).at[bp:bp + bs].set(c_sample)
        mod = _ada_modulation(c_pad, w_ada[l], b_ada[l]).reshape(rows, 6, d)
        gate_bias = jnp.concatenate([b_igate[l].reshape(-1), b_fgate[l].reshape(-1)])
        w_router = jnp.concatenate([w_gr[l], w_er[l]], axis=1).T
        w_router = jnp.zeros((ROUTER_ROWS, d), F32).at[:N_GROUPS + N_EXPERTS].set(w_router)
        wr_hi = w_router.astype(BF16)
        b_router = jnp.zeros((ROUTER_ROWS,), F32).at[:N_GROUPS + N_EXPERTS].set(jnp.concatenate([b_gr[l], b_er[l]]))
        p = {
            "norm1_w": norm1_w[l].reshape(1, d),
            "norm2_w": norm2_w[l].reshape(1, d),
            "w_main": w_in[l][:, :C_MAIN].astype(BF16),
            "w_v_t": jnp.concatenate([w_in[l][:, C_AQ + C_AK:C_AQ + C_AK + C_AV],
                                      w_in[l][:, C_AQ + C_AK + C_AV + C_MQ:C_AQ + C_AK + C_AV + C_MQ + C_MK]],
                                     axis=1).T.astype(BF16),
            "w_gates": jnp.zeros((d, LANES), BF16).at[:, :C_MG].set(w_in[l][:, C_MAIN:].astype(BF16)),
            "q_norm_w": q_norm_w[l].reshape(1, HEAD_DIM),
            "k_norm_w": k_norm_w[l].reshape(1, HEAD_DIM),
            "gate_bias_col": jnp.zeros((1, LANES), F32).at[0, :C_MG].set(gate_bias),
            "gate_bias_row": jnp.broadcast_to(gate_bias[:, None], (C_MG, ML_CHUNK)),
            "ml_norm_w": ml_norm_w[l].reshape(1, C_MV),
            "w_out": w_out[l].astype(BF16),
            "wr_hi": wr_hi,
            "wr_lo": (w_router - wr_hi.astype(F32)).astype(BF16),
            "b_router": jnp.broadcast_to(b_router[:, None], (ROUTER_ROWS, LANES)),
            "w_gate": w_gate[l].astype(BF16),
            "w_up": w_up[l].astype(BF16),
            "w_down": w_down[l].astype(BF16),
        }
        y_prompt = _encoder_layer(y_prompt, mod[:bp], p)
        y_sample = _encoder_layer(y_sample, mod[bp:bp + bs], p)
    return (y_prompt, y_sample)
```

```python
import functools

import jax
import jax.numpy as jnp
import numpy as np
from jax import lax
from jax.experimental import pallas as pl
from jax.experimental.pallas import tpu as pltpu

F32 = jnp.float32
BF16 = jnp.bfloat16
I32 = jnp.int32

GRID_W = 64
HEAD_DIM = 128
ATT_HEADS = 8
ATT_KV_HEADS = 2
ATT_GROUP = ATT_HEADS // ATT_KV_HEADS
ROPE_THETA = 10000.0
ROPE_PAIRS = HEAD_DIM // 4
ML_HEADS = 4
ML_DV = 256
ML_DQK = 128
ML_CHUNK = 128
N_GROUPS = 4
EXPERTS_PER_GROUP = 4
N_EXPERTS = N_GROUPS * EXPERTS_PER_GROUP
EPS = 1e-6
Q_SCALE = HEAD_DIM ** -0.5 * float(np.log2(np.e))

C_AQ = ATT_HEADS * HEAD_DIM
C_AK = ATT_KV_HEADS * HEAD_DIM
C_AV = ATT_KV_HEADS * HEAD_DIM
C_MQ = ML_HEADS * ML_DQK
C_MK = ML_HEADS * ML_DQK
C_MV = ML_HEADS * ML_DV
C_MO = ML_HEADS * ML_DV
C_MG = 4 * ML_HEADS
C_MAIN = C_AQ + C_AK + C_AV + C_MQ + C_MK + C_MV + C_MO

N_PAIRS = 6
PAIR_SLOT_A = (0, 2, 2, 3, 3, 3)
PAIR_SLOT_B = (1, 1, 0, 0, 1, 2)
N_BUCKETS = N_GROUPS * N_PAIRS
BUCKET_ROWS = 32

LANES = 128
ROUTER_ROWS = 32
MIB = 1024 * 1024

TM_PROJ = 256
TQ_ATT = 256
TM_MIX = 512
TL_ROUTE = 512
TD_DISPATCH = 256
TM_MOE = 256
TC_COMBINE = 256
ML_SEQS = 2
ATT_SOFTMAX_LAG = 2
ATT_VALUES_LAG = 1
ATT_SLOTS = 4
DMA_UNROLL = 8


def _params(semantics, vmem_mib):
    return pltpu.CompilerParams(dimension_semantics=semantics, vmem_limit_bytes=vmem_mib * MIB)


def _resident(shape):
    nd = len(shape)
    return pl.BlockSpec(shape, lambda *_: (0,) * nd, pipeline_mode=pl.Buffered(1))


def _ada_kernel(c_ref, w_ref, b_ref, o_ref):
    c = c_ref[...]
    s = (c * jax.nn.sigmoid(c)).astype(BF16)
    o_ref[...] = jnp.dot(s, w_ref[...].astype(BF16), preferred_element_type=F32) + b_ref[...]


def _ada_modulation(c_pad, w_ada, b_ada):
    rows, d = c_pad.shape
    n = w_ada.shape[1]
    tn = 1024
    return pl.pallas_call(
        _ada_kernel,
        grid=(n // tn,),
        in_specs=[pl.BlockSpec((rows, d), lambda j: (0, 0)),
                  pl.BlockSpec((d, tn), lambda j: (0, j)),
                  pl.BlockSpec((1, tn), lambda j: (0, j))],
        out_specs=pl.BlockSpec((rows, tn), lambda j: (0, j)),
        out_shape=jax.ShapeDtypeStruct((rows, n), F32),
        compiler_params=_params(("arbitrary",), 40),
        name="ada_modulation",
    )(c_pad, w_ada, b_ada.reshape(1, n))


def _rms(x, w):
    return x * lax.rsqrt(jnp.mean(x * x, axis=-1, keepdims=True) + EPS) * w


def _inproj_kernel(x_ref, mod_ref, n1_ref, w_ref, wvt_ref, wg_ref, qn_ref, kn_ref, cos_ref, sin_ref,
                   aq_ref, ak_ref, avt_ref, mq_ref, mkt_ref, mv_ref, mo_ref, mg_ref):
    x = x_ref[0]
    h = _rms(x, n1_ref[...]) * (1.0 + mod_ref[0, 1:2, :]) + mod_ref[0, 0:1, :]
    hb = h.astype(BF16)

    def proj(c0, width):
        return jnp.dot(hb, w_ref[:, c0:c0 + width], preferred_element_type=F32)

    cos = cos_ref[...]
    sin = sin_ref[...]
    lane = lax.broadcasted_iota(I32, (1, HEAD_DIM), 1)
    first = (lane % (2 * ROPE_PAIRS)) < ROPE_PAIRS

    def norm_rope(p, w):
        pn = _rms(p, w)
        partner = jnp.where(first, pltpu.roll(pn, HEAD_DIM - ROPE_PAIRS, 1), pltpu.roll(pn, ROPE_PAIRS, 1))
        return pn * cos + partner * sin

    c0 = 0
    for half in range(2):
        p = proj(c0, C_AQ // 2)
        for hh in range(ATT_HEADS // 2):
            col = half * (C_AQ // 2) + hh * HEAD_DIM
            qh = norm_rope(p[:, hh * HEAD_DIM:(hh + 1) * HEAD_DIM], qn_ref[...])
            aq_ref[0, :, col:col + HEAD_DIM] = (qh * Q_SCALE).astype(BF16)
        c0 += C_AQ // 2
    p = proj(c0, C_AK)
    for hh in range(ATT_KV_HEADS):
        ak_ref[0, :, hh * HEAD_DIM:(hh + 1) * HEAD_DIM] = norm_rope(p[:, hh * HEAD_DIM:(hh + 1) * HEAD_DIM], kn_ref[...]).astype(BF16)
    nt = (((1,), (1,)), ((), ()))
    avt_ref[0] = lax.dot_general(wvt_ref[0:C_AV, :], hb, nt, preferred_element_type=F32).astype(BF16)
    mkt_ref[0] = lax.dot_general(wvt_ref[C_AV:, :], hb, nt, preferred_element_type=F32).astype(BF16)
    c0 += C_AK + C_AV
    mq_ref[0] = (proj(c0, C_MQ) * (ML_DQK ** -0.5)).astype(BF16)
    c0 += C_MQ + C_MK
    for half in range(2):
        mv_ref[0, :, half * 512:(half + 1) * 512] = proj(c0, 512).astype(BF16)
        c0 += 512
    for half in range(2):
        mo_ref[0, :, half * 512:(half + 1) * 512] = proj(c0, 512)
        c0 += 512
    mg_ref[0] = jnp.dot(hb, wg_ref[...], preferred_element_type=F32)


def _in_projection(x, mod, norm1_w, w_main, w_v_t, w_gates, q_norm_w, k_norm_w, cos_t, sin_t):
    b, t, d = x.shape
    tm = min(TM_PROJ, t)
    row = lambda width: pl.BlockSpec((1, tm, width), lambda bi, i: (bi, i, 0))
    outs = ((C_AQ, False), (C_AK, False), (C_AV, True), (C_MQ, False), (C_MK, True), (C_MV, False),
            (C_MO, False), (LANES, False))
    dtypes = (BF16, BF16, BF16, BF16, BF16, BF16, F32, F32)
    col = lambda width: pl.BlockSpec((1, width, tm), lambda bi, i: (bi, 0, i))
    return pl.pallas_call(
        _inproj_kernel,
        grid=(b, t // tm),
        in_specs=[row(d),
                  pl.BlockSpec((1, 6, d), lambda bi, i: (bi, 0, 0)),
                  _resident((1, d)),
                  _resident((d, C_MAIN)),
                  _resident((C_AV + C_MK, d)),
                  _resident((d, LANES)),
                  _resident((1, HEAD_DIM)),
                  _resident((1, HEAD_DIM)),
                  pl.BlockSpec((tm, HEAD_DIM), lambda bi, i: (i, 0)),
                  pl.BlockSpec((tm, HEAD_DIM), lambda bi, i: (i, 0))],
        out_specs=[col(w) if tr else row(w) for w, tr in outs],
        out_shape=[jax.ShapeDtypeStruct((b, w, t) if tr else (b, t, w), dt) for (w, tr), dt in zip(outs, dtypes)],
        compiler_params=_params(("parallel", "arbitrary"), 48),
        name="in_projection",
    )(x, mod, norm1_w, w_main, w_v_t, w_gates, q_norm_w, k_norm_w, cos_t, sin_t)


def _attention_kernel(q_ref, qn_ref, k_ref, vt_ref, o_ref, s_ref, p_ref, l_ref):
    def scores(g, src_ref=q_ref):
        kv = g // ATT_GROUP
        q = src_ref[0, :, g * HEAD_DIM:(g + 1) * HEAD_DIM]
        k = k_ref[0, :, kv * HEAD_DIM:(kv + 1) * HEAD_DIM]
        s_ref[g % ATT_SLOTS] = lax.dot_general(k, q, (((1,), (1,)), ((), ())), preferred_element_type=F32)

    def softmax(g):
        s = s_ref[g % ATT_SLOTS]
        m = jnp.max(s, axis=0, keepdims=True)
        p = jnp.exp2(s - m)
        l_ref[g] = jnp.broadcast_to(jnp.sum(p, axis=0, keepdims=True), l_ref.shape[1:])
        p_ref[g % ATT_SLOTS] = p.astype(BF16)

    def values(g):
        kv = g // ATT_GROUP
        vt = vt_ref[0, kv * HEAD_DIM:(kv + 1) * HEAD_DIM, :]
        o_t = jnp.dot(vt, p_ref[g % ATT_SLOTS], preferred_element_type=F32)
        o_ref[0, :, g * HEAD_DIM:(g + 1) * HEAD_DIM] = (o_t * (1.0 / l_ref[g][0:1, :])).T.astype(BF16)

    n = ATT_HEADS
    lag = ATT_SOFTMAX_LAG

    @pl.when(pl.program_id(1) == 0)
    def _():
        for g in range(lag):
            scores(g)

    for g in range(n):
        if g + lag < n:
            scores(g + lag)
        else:
            scores(g + lag - n, qn_ref)
        softmax(g)
        if g >= ATT_VALUES_LAG:
            values(g - ATT_VALUES_LAG)
    for g in range(n - ATT_VALUES_LAG, n):
        values(g)


def _attention(aq, ak, av_t):
    b, t, _ = aq.shape
    tq = min(TQ_ATT, t)
    nt = t // tq
    return pl.pallas_call(
        _attention_kernel,
        grid=(b, nt),
        in_specs=[pl.BlockSpec((1, tq, C_AQ), lambda bi, i: (bi, i, 0)),
                  pl.BlockSpec((1, tq, C_AQ), lambda bi, i: (bi, jnp.minimum(i + 1, nt - 1), 0)),
                  pl.BlockSpec((1, t, C_AK), lambda bi, i: (bi, 0, 0)),
                  pl.BlockSpec((1, C_AV, t), lambda bi, i: (bi, 0, 0))],
        out_specs=pl.BlockSpec((1, tq, C_AQ), lambda bi, i: (bi, i, 0)),
        out_shape=jax.ShapeDtypeStruct((b, t, C_AQ), BF16),
        scratch_shapes=[pltpu.VMEM((ATT_SLOTS, t, tq), F32), pltpu.VMEM((ATT_SLOTS, t, tq), BF16),
                        pltpu.VMEM((ATT_HEADS, 8, tq), F32)],
        compiler_params=_params(("parallel", "arbitrary"), 48),
        name="attention",
    )(aq, aq, ak, av_t)


def _log_sigmoid(x):
    return jnp.minimum(x, 0.0) - jnp.log1p(jnp.exp(-jnp.abs(x)))


def _split3(a):
    a1 = a.astype(BF16)
    r1 = a - a1.astype(F32)
    a2 = r1.astype(BF16)
    a3 = (r1 - a2.astype(F32)).astype(BF16)
    return a1, a2, a3


def _lane_pick(a, idx):
    lane = lax.broadcasted_iota(I32, a.shape, 1)
    return jnp.sum(jnp.where(lane == idx, a, 0.0), axis=-1, keepdims=True)


def _mlstm_kernel(qf, kf, vf, gcf, grf, qb, kb, vb, gcb, grb, bc_ref, br_ref, hf_ref, hb_ref,
                  ct_ref, m_ref):
    @pl.when(pl.program_id(1) == 0)
    def _():
        ct_ref[...] = jnp.zeros_like(ct_ref)
        m_ref[...] = jnp.zeros_like(m_ref)

    L = ML_CHUNK
    row = lax.broadcasted_iota(I32, (L, L), 0)
    col = lax.broadcasted_iota(I32, (L, L), 1)
    lower = (col <= row)
    upper = (col >= row)
    lower_b = lower.astype(BF16)
    upper_b = upper.astype(BF16)

    gates = {}
    for bb in range(qf.shape[0]):
        for d, (gc_ref, gr_ref) in enumerate(((gcf, grf), (gcb, grb))):
            cum_cols = upper_b if d else lower_b
            cum_rows = lower_b if d else upper_b
            pre_c = gc_ref[bb] + bc_ref[...]
            pre_r = gr_ref[bb] + br_ref[...]
            b_cols = sum(jnp.dot(cum_cols, part, preferred_element_type=F32) for part in _split3(_log_sigmoid(pre_c)))
            b_rows = sum(jnp.dot(part, cum_rows, preferred_element_type=F32) for part in _split3(_log_sigmoid(pre_r)))
            gates[bb, d] = (pre_r, b_cols, b_rows)

    chains = [(bb, d, hh) for bb in range(qf.shape[0]) for d in range(2) for hh in range(ML_HEADS)]
    refs = ((qf, kf, vf, hf_ref), (qb, kb, vb, hb_ref))
    ones_block = jnp.ones((L, LANES), BF16)


    st = {}
    for (bb, d, hh) in chains:
        q_ref, kt_ref, v_ref, _ = refs[d]
        pre_r, b_cols, b_rows = gates[bb, d]
        ci = d * ML_HEADS + hh
        cf = 2 * ML_HEADS + d * ML_HEADS + hh
        sidx = (bb * 2 + d) * ML_HEADS + hh
        i_row = pre_r[ci:ci + 1, :]
        b_row = b_rows[cf:cf + 1, :]
        m = m_ref[sidx][0:1, 0:1]
        c_vis = jnp.where(upper if d else lower, i_row - b_row, -jnp.inf)
        shift = -jnp.maximum(m, jnp.max(c_vis, axis=-1, keepdims=True))
        dec = jnp.exp(c_vis + shift)
        a_inter = jnp.exp(m + shift)
        floor = jnp.exp(shift - _lane_pick(b_cols, cf))
        q = q_ref[bb, :, hh * ML_DQK:(hh + 1) * ML_DQK]
        kt = kt_ref[bb, hh * ML_DQK:(hh + 1) * ML_DQK, :]
        v_aug = jnp.concatenate([v_ref[bb, :, hh * ML_DV:(hh + 1) * ML_DV], ones_block], axis=1)
        qk = jnp.dot(q, kt, preferred_element_type=F32)
        ct = ct_ref[sidx]
        inter = jnp.dot(q, ct.astype(BF16), preferred_element_type=F32)
        b_end = _lane_pick(b_row, 0 if d else L - 1)
        g_row = b_end - b_row + i_row
        m_new = jnp.maximum(b_end + m, jnp.max(g_row, axis=-1, keepdims=True))
        decay = jnp.exp(b_end + m - m_new)
        w_row = jnp.exp(g_row - m_new)
        upd = jnp.dot((kt.astype(F32) * w_row).astype(BF16), v_aug, preferred_element_type=F32)
        st[bb, d, hh] = (sidx, dec, a_inter, floor, v_aug, qk, ct, inter, m_new, decay, upd)

    sv = {}
    for key in chains:
        sidx, dec, a_inter, floor, v_aug, qk, ct, inter, m_new, decay, upd = st[key]
        sv[key] = jnp.dot((qk * dec).astype(BF16), v_aug, preferred_element_type=F32)

    for key in chains:
        bb, d, hh = key
        sidx, dec, a_inter, floor, v_aug, qk, ct, inter, m_new, decay, upd = st[key]
        both = a_inter * inter + sv[key]
        scale = 1.0 / jnp.maximum(jnp.abs(both[:, ML_DV:]), floor)
        h_ref = refs[d][3]
        for blk in range(ML_DV // LANES):
            lo = hh * ML_DV + blk * LANES
            h_ref[bb, :, lo:lo + LANES] = both[:, blk * LANES:(blk + 1) * LANES] * scale
        ct_ref[sidx] = decay * ct + upd
        m_ref[sidx] = jnp.broadcast_to(m_new, (8, LANES))


def _mlstm(mq, mk_t, mv, mg, mg_t, bias_col, bias_row):
    b, t, _ = mq.shape
    L = ML_CHUNK
    nc = t // L
    sb = ML_SEQS if b % ML_SEQS == 0 else 1
    fw = lambda width: pl.BlockSpec((sb, L, width), lambda bi, c: (bi, c, 0))
    bw = lambda width: pl.BlockSpec((sb, L, width), lambda bi, c: (bi, nc - 1 - c, 0))
    fw_t = lambda rows: pl.BlockSpec((sb, rows, L), lambda bi, c: (bi, 0, c))
    bw_t = lambda rows: pl.BlockSpec((sb, rows, L), lambda bi, c: (bi, 0, nc - 1 - c))
    ns = sb * 2 * ML_HEADS
    return pl.pallas_call(
        _mlstm_kernel,
        grid=(b // sb, nc),
        in_specs=[fw(C_MQ), fw_t(C_MK), fw(C_MV), fw(LANES), fw_t(C_MG),
                  bw(C_MQ), bw_t(C_MK), bw(C_MV), bw(LANES), bw_t(C_MG),
                  _resident((1, LANES)), _resident((C_MG, L))],
        out_specs=[fw(C_MV), bw(C_MV)],
        out_shape=[jax.ShapeDtypeStruct((b, t, C_MV), F32)] * 2,
        scratch_shapes=[pltpu.VMEM((ns, ML_DQK, ML_DV + LANES), F32),
                        pltpu.VMEM((ns, 8, LANES), F32)],
        compiler_params=_params(("parallel", "arbitrary"), 32),
        name="mlstm",
    )(mq, mk_t, mv, mg, mg_t, mq, mk_t, mv, mg, mg_t, bias_col, bias_row)


def _moe_input(x1, n2_ref, mod_ref):
    return _rms(x1, n2_ref[...]) * (1.0 + mod_ref[0, 4:5, :]) + mod_ref[0, 3:4, :]


def _mix_kernel(x_ref, att_ref, hf_ref, hb_ref, mo_ref, mlw_ref, wo_ref, mod_ref, n2_ref,
                wrh_ref, wrl_ref, br_ref, x1_ref, lg_ref):
    ml = hf_ref[0] + hb_ref[0]
    gate = jax.nn.sigmoid(mo_ref[0])
    mix = jnp.dot(att_ref[0], wo_ref[0:C_AQ, :], preferred_element_type=F32)
    for hh in range(ML_HEADS):
        sl = slice(hh * ML_DV, (hh + 1) * ML_DV)
        seg = (_rms(ml[:, sl], mlw_ref[:, sl]) * gate[:, sl]).astype(BF16)
        mix += jnp.dot(seg, wo_ref[C_AQ + hh * ML_DV:C_AQ + (hh + 1) * ML_DV, :], preferred_element_type=F32)
    x1 = x_ref[0] + mod_ref[0, 2:3, :] * mix
    x1_ref[0] = x1
    h2 = _moe_input(x1, n2_ref, mod_ref)
    h_hi = h2.astype(BF16)
    h_lo = (h2 - h_hi.astype(F32)).astype(BF16)
    nt = (((1,), (1,)), ((), ()))
    both = lax.dot_general(jnp.concatenate([wrh_ref[...], wrl_ref[...]], axis=0), h_hi, nt, preferred_element_type=F32)
    lg = both[:ROUTER_ROWS] + both[ROUTER_ROWS:] + lax.dot_general(wrh_ref[...], h_lo, nt, preferred_element_type=F32)
    lg_ref[0] = lg + br_ref[:, 0:1]


def _mix_and_router(x, att, h_fw, h_bw, mo, ml_norm_w, w_out, mod, norm2_w, wr_hi, wr_lo, b_router):
    b, t, d = x.shape
    tm = min(TM_MIX, t)
    row = lambda width: pl.BlockSpec((1, tm, width), lambda bi, i: (bi, i, 0))
    return pl.pallas_call(
        _mix_kernel,
        grid=(b, t // tm),
        in_specs=[row(d), row(C_AQ), row(C_MV), row(C_MV), row(C_MO),
                  _resident((1, C_MV)), _resident((d, d)),
                  pl.BlockSpec((1, 6, d), lambda bi, i: (bi, 0, 0)),
                  _resident((1, d)), _resident((ROUTER_ROWS, d)), _resident((ROUTER_ROWS, d)),
                  _resident((ROUTER_ROWS, LANES))],
        out_specs=[row(d), pl.BlockSpec((1, ROUTER_ROWS, tm), lambda bi, i: (bi, 0, i))],
        out_shape=[jax.ShapeDtypeStruct((b, t, d), F32), jax.ShapeDtypeStruct((b, ROUTER_ROWS, t), F32)],
        compiler_params=_params(("parallel", "arbitrary"), 48),
        name="mix_router",
    )(x, att, h_fw, h_bw, mo, ml_norm_w, w_out, mod, norm2_w, wr_hi, wr_lo, b_router)


def _first_argmax(rows):
    best = rows[0]
    idx = jnp.zeros_like(best)
    for j in range(1, len(rows)):
        better = rows[j] > best
        best = jnp.where(better, rows[j], best)
        idx = jnp.where(better, float(j), idx)
    return best, idx


def _softmax_rows(rows):
    mx = functools.reduce(jnp.maximum, rows)
    ex = [jnp.exp(r - mx) for r in rows]
    tot = functools.reduce(lambda a, c: a + c, ex)
    return [e / tot for e in ex]


def _route_kernel(lg_ref, idx_ref, wt_ref, cnt_ref, run_ref):
    @pl.when((pl.program_id(0) == 0) & (pl.program_id(1) == 0))
    def _():
        run_ref[...] = jnp.zeros_like(run_ref)

    tl = lg_ref.shape[2]
    lg = lg_ref[0]
    p_grp = _softmax_rows([lg[g:g + 1, :] for g in range(N_GROUPS)])
    p_g, g_idx = _first_argmax(p_grp)
    el = []
    for j in range(EXPERTS_PER_GROUP):
        sel = lg[N_GROUPS + (N_GROUPS - 1) * EXPERTS_PER_GROUP + j:N_GROUPS + (N_GROUPS - 1) * EXPERTS_PER_GROUP + j + 1, :]
        for g in range(N_GROUPS - 2, -1, -1):
            r = N_GROUPS + g * EXPERTS_PER_GROUP + j
            sel = jnp.where(g_idx == float(g), lg[r:r + 1, :], sel)
        el.append(sel)
    pe = _softmax_rows(el)
    w1, i1 = _first_argmax(pe)
    rest = [jnp.where(i1 == float(j), -1.0, pe[j]) for j in range(EXPERTS_PER_GROUP)]
    w2, i2 = _first_argmax(rest)
    tot = w1 + w2
    wt1 = w1 / tot * p_g
    wt2 = w2 / tot * p_g
    lo = jnp.minimum(i1, i2)
    hi = jnp.maximum(i1, i2)
    pair = jnp.where(lo == 0.0, jnp.where(hi == 1.0, 0.0, jnp.where(hi == 2.0, 2.0, 3.0)),
                     jnp.where(lo == 1.0, jnp.where(hi == 2.0, 1.0, 4.0), 5.0))
    slot_a = functools.reduce(lambda acc, p: jnp.where(pair == float(p), float(PAIR_SLOT_A[p]), acc),
                              range(1, N_PAIRS), jnp.full_like(pair, float(PAIR_SLOT_A[0])))
    first_in_a = slot_a == i1
    wa = jnp.where(first_in_a, wt1, wt2)
    wb = jnp.where(first_in_a, wt2, wt1)
    bucket = (g_idx * float(N_PAIRS) + pair).astype(I32)

    brow = lax.broadcasted_iota(I32, (BUCKET_ROWS, tl), 0)
    hit = brow == bucket
    onehot = hit.astype(F32)
    before = lax.broadcasted_iota(I32, (tl, tl), 0) <= lax.broadcasted_iota(I32, (tl, tl), 1)
    incl = jnp.dot(onehot.astype(BF16), before.astype(BF16), preferred_element_type=F32)
    base = run_ref[:, 0:1] + incl - onehot
    rank = jnp.sum(jnp.where(hit, base, 0.0), axis=0, keepdims=True).astype(I32)
    total = run_ref[...] + jnp.sum(onehot, axis=-1, keepdims=True)
    run_ref[...] = total
    cnt_ref[...] = total.astype(I32)

    idx_ref[0] = jnp.concatenate([bucket, rank, jnp.zeros((6, tl), I32)], axis=0)
    wt_ref[0] = jnp.concatenate([wa, wb, jnp.zeros((6, tl), F32)], axis=0)


def _route(logits_t):
    b, _, t = logits_t.shape
    tl = min(TL_ROUTE, t)
    blk = lambda rows: pl.BlockSpec((1, rows, tl), lambda bi, i: (bi, 0, i))
    return pl.pallas_call(
        _route_kernel,
        grid=(b, t // tl),
        in_specs=[blk(ROUTER_ROWS)],
        out_specs=[blk(8), blk(8), pl.BlockSpec((BUCKET_ROWS, LANES), lambda bi, i: (0, 0))],
        out_shape=[jax.ShapeDtypeStruct((b, 8, t), I32), jax.ShapeDtypeStruct((b, 8, t), F32),
                   jax.ShapeDtypeStruct((BUCKET_ROWS, LANES), I32)],
        scratch_shapes=[pltpu.VMEM((BUCKET_ROWS, LANES), F32)],
        compiler_params=_params(("arbitrary", "arbitrary"), 32),
        name="route",
    )(logits_t)


def _row_copy(src_ref, src_row, dst_ref, dst_row, sem):
    return pltpu.make_async_copy(src_ref.at[pl.ds(src_row, 1), :], dst_ref.at[pl.ds(dst_row, 1), :], sem)


def _dispatch_kernel(padlo_ref, padhi_ref, dest_ref, x1_ref, wtail_ref, n2_ref, mod_ref, xs_ref, h_ref, zero_ref, sem):
    td, d = x1_ref.shape
    h_ref[:, 0:d] = _moe_input(x1_ref[...], n2_ref, mod_ref)
    h_ref[:, d:] = wtail_ref[...]

    def start(t, carry):
        _row_copy(h_ref, t, xs_ref, dest_ref[0, 0, t], sem).start()
        return carry

    lax.fori_loop(0, td, start, 0, unroll=DMA_UNROLL)
    pltpu.make_async_copy(h_ref, xs_ref.at[pl.ds(0, td), :], sem).wait()

    @pl.when(pl.program_id(0) == pl.num_programs(0) - 1)
    def _():
        zero_ref[...] = jnp.zeros_like(zero_ref)

        def zero_start(r, carry):
            _row_copy(zero_ref, 0, xs_ref, r, sem).start()
            return carry

        def zero_wait(r, carry):
            _row_copy(zero_ref, 0, xs_ref, r, sem).wait()
            return carry

        for e in range(N_BUCKETS):
            lax.fori_loop(padlo_ref[e], padhi_ref[e], zero_start, 0)
        for e in range(N_BUCKETS):
            lax.fori_loop(padlo_ref[e], padhi_ref[e], zero_wait, 0)


def _dispatch(x1, dest, w_tail, norm2_w, mod, pad_lo, pad_hi, m_pad):
    n, d = x1.shape
    b, _, t = dest.shape
    td = min(TD_DISPATCH, t)
    per = t // td
    width = d + LANES
    return pl.pallas_call(
        _dispatch_kernel,
        grid_spec=pltpu.PrefetchScalarGridSpec(
            num_scalar_prefetch=2,
            grid=(n // td,),
            in_specs=[pl.BlockSpec((1, 1, td), lambda i, *_: (i // per, 0, i % per), memory_space=pltpu.SMEM),
                      pl.BlockSpec((td, d), lambda i, *_: (i, 0)),
                      pl.BlockSpec((td, LANES), lambda i, *_: (i, 0)),
                      pl.BlockSpec((1, d), lambda i, *_: (0, 0)),
                      pl.BlockSpec((1, 6, d), lambda i, *_: (i // per, 0, 0))],
            out_specs=pl.BlockSpec(memory_space=pl.ANY),
            scratch_shapes=[pltpu.VMEM((td, width), F32), pltpu.VMEM((8, width), F32), pltpu.SemaphoreType.DMA(())]),
        out_shape=jax.ShapeDtypeStruct((m_pad, width), F32),
        compiler_params=_params(("arbitrary",), 32),
        name="dispatch",
    )(pad_lo, pad_hi, dest, x1, w_tail, norm2_w, mod)


def _moe_kernel(ea_ref, eb_ref, nused_ref, x_ref, wga_ref, wua_ref, wda_ref, wgb_ref, wub_ref, wdb_ref, o_ref):
    r = pl.program_id(0)
    d = o_ref.shape[1]

    @pl.when(r < nused_ref[0])
    def _():
        xb = x_ref[:, 0:d].astype(BF16)

        def expert(wg_ref, wu_ref, wd_ref):
            g = jnp.dot(xb, wg_ref[0], preferred_element_type=F32)
            u = jnp.dot(xb, wu_ref[0], preferred_element_type=F32)
            h = (g * jax.nn.sigmoid(g) * u).astype(BF16)
            return jnp.dot(h, wd_ref[0], preferred_element_type=F32)

        tail = x_ref[:, d:]
        o_ref[...] = (expert(wga_ref, wua_ref, wda_ref) * _lane_pick(tail, 0)
                      + expert(wgb_ref, wub_ref, wdb_ref) * _lane_pick(tail, 1))

    @pl.when(r >= nused_ref[0])
    def _():
        o_ref[...] = jnp.zeros_like(o_ref)


def _moe_experts(xs, blk_ea, blk_eb, nused, w_gate, w_up, w_down):
    m_pad, width = xs.shape
    d = width - LANES
    f = w_gate.shape[2]
    tm = TM_MOE
    last = lambda r, nu: jnp.minimum(r, nu[0] - 1)
    slot = lambda shape, pick: pl.BlockSpec(shape, lambda r, ea, eb, nu: (pick(ea, eb)[last(r, nu)], 0, 0),
                                            pipeline_mode=pl.Buffered(1))
    slot_a = lambda shape: slot(shape, lambda ea, eb: ea)
    slot_b = lambda shape: slot(shape, lambda ea, eb: eb)
    return pl.pallas_call(
        _moe_kernel,
        grid_spec=pltpu.PrefetchScalarGridSpec(
            num_scalar_prefetch=3,
            grid=(m_pad // tm,),
            in_specs=[pl.BlockSpec((tm, width), lambda r, ea, eb, nu: (last(r, nu), 0)),
                      slot_a((1, d, f)), slot_a((1, d, f)), slot_a((1, f, d)),
                      slot_b((1, d, f)), slot_b((1, d, f)), slot_b((1, f, d))],
            out_specs=pl.BlockSpec((tm, d), lambda r, ea, eb, nu: (r, 0))),
        out_shape=jax.ShapeDtypeStruct((m_pad, d), F32),
        compiler_params=_params(("arbitrary",), 56),
        name="moe_experts",
    )(blk_ea, blk_eb, nused, xs, w_gate, w_up, w_down, w_gate, w_up, w_down)


def _combine_kernel(dest_ref, x1_ref, mod_ref, o_ref, y_ref, g_ref, sem):
    tc = x1_ref.shape[0]

    def start(t, carry):
        _row_copy(o_ref, dest_ref[0, 0, t], g_ref, t, sem).start()
        return carry

    lax.fori_loop(0, tc, start, 0, unroll=DMA_UNROLL)
    pltpu.make_async_copy(o_ref.at[pl.ds(0, tc), :], g_ref, sem).wait()
    y_ref[...] = x1_ref[...] + mod_ref[0, 5:6, :] * g_ref[...]


def _combine(x1, dest, mod, o_rows):
    n, d = x1.shape
    b, _, t = dest.shape
    tc = min(TC_COMBINE, t)
    per = t // tc
    return pl.pallas_call(
        _combine_kernel,
        grid=(n // tc,),
        in_specs=[pl.BlockSpec((1, 1, tc), lambda i: (i // per, 0, i % per), memory_space=pltpu.SMEM),
                  pl.BlockSpec((tc, d), lambda i: (i, 0)),
                  pl.BlockSpec((1, 6, d), lambda i: (i // per, 0, 0)),
                  pl.BlockSpec(memory_space=pl.ANY)],
        out_specs=pl.BlockSpec((tc, d), lambda i: (i, 0)),
        scratch_shapes=[pltpu.VMEM((tc, d), F32), pltpu.SemaphoreType.DMA(())],
        out_shape=jax.ShapeDtypeStruct((n, d), F32),
        compiler_params=_params(("arbitrary",), 32),
        name="combine",
    )(dest, x1, mod, o_rows)


def _rope_tables(t):
    rows = t // GRID_W
    row = jnp.repeat(jnp.arange(rows, dtype=F32), GRID_W)
    col = jnp.tile(jnp.arange(GRID_W, dtype=F32), rows)
    freqs = ROPE_THETA ** (-jnp.arange(ROPE_PAIRS, dtype=F32) / ROPE_PAIRS)
    ar = row[:, None] * freqs
    ac = col[:, None] * freqs
    cos_t = jnp.concatenate([jnp.cos(ar), jnp.cos(ar), jnp.cos(ac), jnp.cos(ac)], axis=-1)
    sin_t = jnp.concatenate([-jnp.sin(ar), jnp.sin(ar), -jnp.sin(ac), jnp.sin(ac)], axis=-1)
    return cos_t, sin_t


def _encoder_layer(x, mod, p):
    b, t, d = x.shape
    n = b * t
    cos_t, sin_t = _rope_tables(t)
    aq, ak, av_t, mq, mk_t, mv, mo, mg = _in_projection(
        x, mod, p["norm1_w"], p["w_main"], p["w_v_t"], p["w_gates"], p["q_norm_w"], p["k_norm_w"], cos_t, sin_t)
    att = _attention(aq, ak, av_t)
    mg_t = jnp.swapaxes(mg[:, :, :C_MG], 1, 2)
    h_fw, h_bw = _mlstm(mq, mk_t, mv, mg, mg_t, p["gate_bias_col"], p["gate_bias_row"])
    x1, logits_t = _mix_and_router(x, att, h_fw, h_bw, mo, p["ml_norm_w"], p["w_out"], mod,
                                       p["norm2_w"], p["wr_hi"], p["wr_lo"], p["b_router"])
    idx, wts, counts = _route(logits_t)

    tm = TM_MOE
    counts = counts[:N_BUCKETS, 0]
    padded = (counts + tm - 1) // tm * tm
    pend = jnp.cumsum(padded)
    pstart = (pend - padded).astype(I32)
    nb = (n + tm - 1) // tm + N_BUCKETS
    m_pad = nb * tm
    block_row0 = jnp.arange(nb, dtype=I32) * tm
    blk_bucket = jnp.minimum(jnp.sum(pend[None, :] <= block_row0[:, None], axis=1), N_BUCKETS - 1)
    group_base = np.repeat(np.arange(N_GROUPS) * EXPERTS_PER_GROUP, N_PAIRS)
    expert_a = jnp.asarray(group_base + np.tile(PAIR_SLOT_A, N_GROUPS), I32)
    expert_b = jnp.asarray(group_base + np.tile(PAIR_SLOT_B, N_GROUPS), I32)
    blk_onehot = blk_bucket[:, None] == jnp.arange(N_BUCKETS)
    blk_ea = jnp.sum(jnp.where(blk_onehot, expert_a, 0), axis=1).astype(I32)
    blk_eb = jnp.sum(jnp.where(blk_onehot, expert_b, 0), axis=1).astype(I32)
    nused = (pend[-1:] // tm).astype(I32)
    onehot = idx[:, 0:1, :, None] == jnp.arange(N_BUCKETS, dtype=I32)
    dest = jnp.sum(jnp.where(onehot, pstart, 0), axis=-1) + idx[:, 1:2, :]

    pad_lo = (pstart + counts).astype(I32)
    pad_hi = jnp.concatenate([pstart[1:], jnp.full((1,), m_pad, I32)])
    x1 = x1.reshape(n, d)
    w_tail = jnp.zeros((n, LANES), F32).at[:, :2].set(jnp.swapaxes(wts[:, :2, :], 1, 2).reshape(n, 2))
    xs = _dispatch(x1, dest, w_tail, p["norm2_w"], mod, pad_lo, pad_hi, m_pad)
    o_rows = _moe_experts(xs, blk_ea, blk_eb, nused, p["w_gate"], p["w_up"], p["w_down"])
    y = _combine(x1, dest, mod, o_rows)
    return y.reshape(b, t, d)


def kernel(x_prompt, x_sample, c_prompt, c_sample, norm1_w, norm2_w, w_ada, b_ada, w_in, q_norm_w, k_norm_w, b_igate, b_fgate, ml_norm_w, w_out, w_gr, b_gr, w_er, b_er, w_gate, w_up, w_down):
    depth = norm1_w.shape[0]
    d = x_prompt.shape[-1]
    bp = x_prompt.shape[0]
    bs = x_sample.shape[0]
    rows = -(-(bp + bs) // 8) * 8
    y_prompt, y_sample = x_prompt, x_sample
    for l in range(depth):
        c_pad = jnp.zeros((rows, d), F32).at[:bp].set(c_prompt).at[bp:bp + bs].set(c_sample)
        mod = _ada_modulation(c_pad, w_ada[l], b_ada[l]).reshape(rows, 6, d)
        gate_bias = jnp.concatenate([b_igate[l].reshape(-1), b_fgate[l].reshape(-1)])
        w_router = jnp.concatenate([w_gr[l], w_er[l]], axis=1).T
        w_router = jnp.zeros((ROUTER_ROWS, d), F32).at[:N_GROUPS + N_EXPERTS].set(w_router)
        wr_hi = w_router.astype(BF16)
        b_router = jnp.zeros((ROUTER_ROWS,), F32).at[:N_GROUPS + N_EXPERTS].set(jnp.concatenate([b_gr[l], b_er[l]]))
        p = {
            "norm1_w": norm1_w[l].reshape(1, d),
            "norm2_w": norm2_w[l].reshape(1, d),
            "w_main": w_in[l][:, :C_MAIN].astype(BF16),
            "w_v_t": jnp.concatenate([w_in[l][:, C_AQ + C_AK:C_AQ + C_AK + C_AV],
                                      w_in[l][:, C_AQ + C_AK + C_AV + C_MQ:C_AQ + C_AK + C_AV + C_MQ + C_MK]],
                                     axis=1).T.astype(BF16),
            "w_gates": jnp.zeros((d, LANES), BF16).at[:, :C_MG].set(w_in[l][:, C_MAIN:].astype(BF16)),
            "q_norm_w": q_norm_w[l].reshape(1, HEAD_DIM),
            "k_norm_w": k_norm_w[l].reshape(1, HEAD_DIM),
            "gate_bias_col": jnp.zeros((1, LANES), F32).at[0, :C_MG].set(gate_bias),
            "gate_bias_row": jnp.broadcast_to(gate_bias[:, None], (C_MG, ML_CHUNK)),
            "ml_norm_w": ml_norm_w[l].reshape(1, C_MV),
            "w_out": w_out[l].astype(BF16),
            "wr_hi": wr_hi,
            "wr_lo": (w_router - wr_hi.astype(F32)).astype(BF16),
            "b_router": jnp.broadcast_to(b_router[:, None], (ROUTER_ROWS, LANES)),
            "w_gate": w_gate[l].astype(BF16),
            "w_up": w_up[l].astype(BF16),
            "w_down": w_down[l].astype(BF16),
        }
        y_prompt = _encoder_layer(y_prompt, mod[:bp], p)
        y_sample = _encoder_layer(y_sample, mod[bp:bp + bs], p)
    return (y_prompt, y_sample)
```

```python
import functools

import jax
import jax.numpy as jnp
import numpy as np
from jax import lax
from jax.experimental import pallas as pl
from jax.experimental.pallas import tpu as pltpu

F32 = jnp.float32
BF16 = jnp.bfloat16
I32 = jnp.int32

GRID_W = 64
HEAD_DIM = 128
ATT_HEADS = 8
ATT_KV_HEADS = 2
ATT_GROUP = ATT_HEADS // ATT_KV_HEADS
ROPE_THETA = 10000.0
ROPE_PAIRS = HEAD_DIM // 4
ML_HEADS = 4
ML_DV = 256
ML_DQK = 128
ML_CHUNK = 128
N_GROUPS = 4
EXPERTS_PER_GROUP = 4
N_EXPERTS = N_GROUPS * EXPERTS_PER_GROUP
EPS = 1e-6
Q_SCALE = HEAD_DIM ** -0.5 * float(np.log2(np.e))

C_AQ = ATT_HEADS * HEAD_DIM
C_AK = ATT_KV_HEADS * HEAD_DIM
C_AV = ATT_KV_HEADS * HEAD_DIM
C_MQ = ML_HEADS * ML_DQK
C_MK = ML_HEADS * ML_DQK
C_MV = ML_HEADS * ML_DV
C_MO = ML_HEADS * ML_DV
C_MG = 4 * ML_HEADS
C_MAIN = C_AQ + C_AK + C_AV + C_MQ + C_MK + C_MV + C_MO

N_PAIRS = 6
PAIR_SLOT_A = (0, 1, 2, 2, 0, 0)
PAIR_SLOT_B = (3, 3, 3, 1, 1, 2)
N_BUCKETS = N_GROUPS * N_PAIRS
BUCKET_ROWS = 32

LANES = 128
ROUTER_ROWS = 32
MIB = 1024 * 1024

TM_PROJ = 256
TQ_ATT = 256
TM_MIX = 512
TL_ROUTE = 512
TD_DISPATCH = 256
TM_MOE = 256
TC_COMBINE = 256
ML_SEQS = 2
ATT_SOFTMAX_LAG = 2
ATT_VALUES_LAG = 1
ATT_SLOTS = 4
DMA_UNROLL = 8


def _params(semantics, vmem_mib):
    return pltpu.CompilerParams(dimension_semantics=semantics, vmem_limit_bytes=vmem_mib * MIB)


def _resident(shape):
    nd = len(shape)
    return pl.BlockSpec(shape, lambda *_: (0,) * nd, pipeline_mode=pl.Buffered(1))


def _ada_kernel(c_ref, w_ref, b_ref, o_ref):
    c = c_ref[...]
    s = (c * jax.nn.sigmoid(c)).astype(BF16)
    o_ref[...] = jnp.dot(s, w_ref[...].astype(BF16), preferred_element_type=F32) + b_ref[...]


def _ada_modulation(c_pad, w_ada, b_ada):
    rows, d = c_pad.shape
    n = w_ada.shape[1]
    tn = 1024
    return pl.pallas_call(
        _ada_kernel,
        grid=(n // tn,),
        in_specs=[pl.BlockSpec((rows, d), lambda j: (0, 0)),
                  pl.BlockSpec((d, tn), lambda j: (0, j)),
                  pl.BlockSpec((1, tn), lambda j: (0, j))],
        out_specs=pl.BlockSpec((rows, tn), lambda j: (0, j)),
        out_shape=jax.ShapeDtypeStruct((rows, n), F32),
        compiler_params=_params(("arbitrary",), 40),
        name="ada_modulation",
    )(c_pad, w_ada, b_ada.reshape(1, n))


def _rms(x, w):
    return x * lax.rsqrt(jnp.mean(x * x, axis=-1, keepdims=True) + EPS) * w


def _inproj_kernel(x_ref, mod_ref, n1_ref, w_ref, wvt_ref, wg_ref, qn_ref, kn_ref, cos_ref, sin_ref,
                   aq_ref, ak_ref, avt_ref, mq_ref, mkt_ref, mv_ref, mo_ref, mg_ref):
    x = x_ref[0]
    h = _rms(x, n1_ref[...]) * (1.0 + mod_ref[0, 1:2, :]) + mod_ref[0, 0:1, :]
    hb = h.astype(BF16)

    def proj(c0, width):
        return jnp.dot(hb, w_ref[:, c0:c0 + width], preferred_element_type=F32)

    cos = cos_ref[...]
    sin = sin_ref[...]
    lane = lax.broadcasted_iota(I32, (1, HEAD_DIM), 1)
    first = (lane % (2 * ROPE_PAIRS)) < ROPE_PAIRS

    def norm_rope(p, w):
        pn = _rms(p, w)
        partner = jnp.where(first, pltpu.roll(pn, HEAD_DIM - ROPE_PAIRS, 1), pltpu.roll(pn, ROPE_PAIRS, 1))
        return pn * cos + partner * sin

    c0 = 0
    for half in range(2):
        p = proj(c0, C_AQ // 2)
        for hh in range(ATT_HEADS // 2):
            col = half * (C_AQ // 2) + hh * HEAD_DIM
            qh = norm_rope(p[:, hh * HEAD_DIM:(hh + 1) * HEAD_DIM], qn_ref[...])
            aq_ref[0, :, col:col + HEAD_DIM] = (qh * Q_SCALE).astype(BF16)
        c0 += C_AQ // 2
    p = proj(c0, C_AK)
    for hh in range(ATT_KV_HEADS):
        ak_ref[0, :, hh * HEAD_DIM:(hh + 1) * HEAD_DIM] = norm_rope(p[:, hh * HEAD_DIM:(hh + 1) * HEAD_DIM], kn_ref[...]).astype(BF16)
    nt = (((1,), (1,)), ((), ()))
    avt_ref[0] = lax.dot_general(wvt_ref[0:C_AV, :], hb, nt, preferred_element_type=F32).astype(BF16)
    mkt_ref[0] = lax.dot_general(wvt_ref[C_AV:, :], hb, nt, preferred_element_type=F32).astype(BF16)
    c0 += C_AK + C_AV
    mq_ref[0] = (proj(c0, C_MQ) * (ML_DQK ** -0.5)).astype(BF16)
    c0 += C_MQ + C_MK
    for half in range(2):
        mv_ref[0, :, half * 512:(half + 1) * 512] = proj(c0, 512).astype(BF16)
        c0 += 512
    for half in range(2):
        mo_ref[0, :, half * 512:(half + 1) * 512] = proj(c0, 512)
        c0 += 512
    mg_ref[0] = jnp.dot(hb, wg_ref[...], preferred_element_type=F32)


def _in_projection(x, mod, norm1_w, w_main, w_v_t, w_gates, q_norm_w, k_norm_w, cos_t, sin_t):
    b, t, d = x.shape
    tm = min(TM_PROJ, t)
    row = lambda width: pl.BlockSpec((1, tm, width), lambda bi, i: (bi, i, 0))
    outs = ((C_AQ, False), (C_AK, False), (C_AV, True), (C_MQ, False), (C_MK, True), (C_MV, False),
            (C_MO, False), (LANES, False))
    dtypes = (BF16, BF16, BF16, BF16, BF16, BF16, F32, F32)
    col = lambda width: pl.BlockSpec((1, width, tm), lambda bi, i: (bi, 0, i))
    return pl.pallas_call(
        _inproj_kernel,
        grid=(b, t // tm),
        in_specs=[row(d),
                  pl.BlockSpec((1, 6, d), lambda bi, i: (bi, 0, 0)),
                  _resident((1, d)),
                  _resident((d, C_MAIN)),
                  _resident((C_AV + C_MK, d)),
                  _resident((d, LANES)),
                  _resident((1, HEAD_DIM)),
                  _resident((1, HEAD_DIM)),
                  pl.BlockSpec((tm, HEAD_DIM), lambda bi, i: (i, 0)),
                  pl.BlockSpec((tm, HEAD_DIM), lambda bi, i: (i, 0))],
        out_specs=[col(w) if tr else row(w) for w, tr in outs],
        out_shape=[jax.ShapeDtypeStruct((b, w, t) if tr else (b, t, w), dt) for (w, tr), dt in zip(outs, dtypes)],
        compiler_params=_params(("parallel", "arbitrary"), 48),
        name="in_projection",
    )(x, mod, norm1_w, w_main, w_v_t, w_gates, q_norm_w, k_norm_w, cos_t, sin_t)


def _attention_kernel(q_ref, qn_ref, k_ref, vt_ref, o_ref, s_ref, p_ref, l_ref):
    def scores(g, src_ref=q_ref):
        kv = g // ATT_GROUP
        q = src_ref[0, :, g * HEAD_DIM:(g + 1) * HEAD_DIM]
        k = k_ref[0, :, kv * HEAD_DIM:(kv + 1) * HEAD_DIM]
        s_ref[g % ATT_SLOTS] = lax.dot_general(k, q, (((1,), (1,)), ((), ())), preferred_element_type=F32)

    def softmax(g):
        s = s_ref[g % ATT_SLOTS]
        m = jnp.max(s, axis=0, keepdims=True)
        p = jnp.exp2(s - m)
        l_ref[g] = jnp.broadcast_to(jnp.sum(p, axis=0, keepdims=True), l_ref.shape[1:])
        p_ref[g % ATT_SLOTS] = p.astype(BF16)

    def values(g):
        kv = g // ATT_GROUP
        vt = vt_ref[0, kv * HEAD_DIM:(kv + 1) * HEAD_DIM, :]
        o_t = jnp.dot(vt, p_ref[g % ATT_SLOTS], preferred_element_type=F32)
        o_ref[0, :, g * HEAD_DIM:(g + 1) * HEAD_DIM] = (o_t * (1.0 / l_ref[g][0:1, :])).T.astype(BF16)

    n = ATT_HEADS
    lag = ATT_SOFTMAX_LAG

    @pl.when(pl.program_id(1) == 0)
    def _():
        for g in range(lag):
            scores(g)

    for g in range(n):
        if g + lag < n:
            scores(g + lag)
        else:
            scores(g + lag - n, qn_ref)
        softmax(g)
        if g >= ATT_VALUES_LAG:
            values(g - ATT_VALUES_LAG)
    for g in range(n - ATT_VALUES_LAG, n):
        values(g)


def _attention(aq, ak, av_t):
    b, t, _ = aq.shape
    tq = min(TQ_ATT, t)
    nt = t // tq
    return pl.pallas_call(
        _attention_kernel,
        grid=(b, nt),
        in_specs=[pl.BlockSpec((1, tq, C_AQ), lambda bi, i: (bi, i, 0)),
                  pl.BlockSpec((1, tq, C_AQ), lambda bi, i: (bi, jnp.minimum(i + 1, nt - 1), 0)),
                  pl.BlockSpec((1, t, C_AK), lambda bi, i: (bi, 0, 0)),
                  pl.BlockSpec((1, C_AV, t), lambda bi, i: (bi, 0, 0))],
        out_specs=pl.BlockSpec((1, tq, C_AQ), lambda bi, i: (bi, i, 0)),
        out_shape=jax.ShapeDtypeStruct((b, t, C_AQ), BF16),
        scratch_shapes=[pltpu.VMEM((ATT_SLOTS, t, tq), F32), pltpu.VMEM((ATT_SLOTS, t, tq), BF16),
                        pltpu.VMEM((ATT_HEADS, 8, tq), F32)],
        compiler_params=_params(("parallel", "arbitrary"), 48),
        name="attention",
    )(aq, aq, ak, av_t)


def _log_sigmoid(x):
    return jnp.minimum(x, 0.0) - jnp.log1p(jnp.exp(-jnp.abs(x)))


def _split3(a):
    a1 = a.astype(BF16)
    r1 = a - a1.astype(F32)
    a2 = r1.astype(BF16)
    a3 = (r1 - a2.astype(F32)).astype(BF16)
    return a1, a2, a3


def _lane_pick(a, idx):
    lane = lax.broadcasted_iota(I32, a.shape, 1)
    return jnp.sum(jnp.where(lane == idx, a, 0.0), axis=-1, keepdims=True)


def _mlstm_kernel(qf, kf, vf, gcf, grf, qb, kb, vb, gcb, grb, bc_ref, br_ref, hf_ref, hb_ref,
                  ct_ref, m_ref):
    @pl.when(pl.program_id(1) == 0)
    def _():
        ct_ref[...] = jnp.zeros_like(ct_ref)
        m_ref[...] = jnp.zeros_like(m_ref)

    L = ML_CHUNK
    row = lax.broadcasted_iota(I32, (L, L), 0)
    col = lax.broadcasted_iota(I32, (L, L), 1)
    lower = (col <= row)
    upper = (col >= row)
    lower_b = lower.astype(BF16)
    upper_b = upper.astype(BF16)

    gates = {}
    for bb in range(qf.shape[0]):
        for d, (gc_ref, gr_ref) in enumerate(((gcf, grf), (gcb, grb))):
            cum_cols = upper_b if d else lower_b
            cum_rows = lower_b if d else upper_b
            pre_c = gc_ref[bb] + bc_ref[...]
            pre_r = gr_ref[bb] + br_ref[...]
            b_cols = sum(jnp.dot(cum_cols, part, preferred_element_type=F32) for part in _split3(_log_sigmoid(pre_c)))
            b_rows = sum(jnp.dot(part, cum_rows, preferred_element_type=F32) for part in _split3(_log_sigmoid(pre_r)))
            gates[bb, d] = (pre_r, b_cols, b_rows)

    chains = [(bb, d, hh) for bb in range(qf.shape[0]) for d in range(2) for hh in range(ML_HEADS)]
    refs = ((qf, kf, vf, hf_ref), (qb, kb, vb, hb_ref))
    ones_block = jnp.ones((L, LANES), BF16)


    st = {}
    for (bb, d, hh) in chains:
        q_ref, kt_ref, v_ref, _ = refs[d]
        pre_r, b_cols, b_rows = gates[bb, d]
        ci = d * ML_HEADS + hh
        cf = 2 * ML_HEADS + d * ML_HEADS + hh
        sidx = (bb * 2 + d) * ML_HEADS + hh
        i_row = pre_r[ci:ci + 1, :]
        b_row = b_rows[cf:cf + 1, :]
        m = m_ref[sidx][0:1, 0:1]
        c_vis = jnp.where(upper if d else lower, i_row - b_row, -jnp.inf)
        shift = -jnp.maximum(m, jnp.max(c_vis, axis=-1, keepdims=True))
        dec = jnp.exp(c_vis + shift)
        a_inter = jnp.exp(m + shift)
        floor = jnp.exp(shift - _lane_pick(b_cols, cf))
        q = q_ref[bb, :, hh * ML_DQK:(hh + 1) * ML_DQK]
        kt = kt_ref[bb, hh * ML_DQK:(hh + 1) * ML_DQK, :]
        v_aug = jnp.concatenate([v_ref[bb, :, hh * ML_DV:(hh + 1) * ML_DV], ones_block], axis=1)
        qk = jnp.dot(q, kt, preferred_element_type=F32)
        ct = ct_ref[sidx]
        inter = jnp.dot(q, ct.astype(BF16), preferred_element_type=F32)
        b_end = _lane_pick(b_row, 0 if d else L - 1)
        g_row = b_end - b_row + i_row
        m_new = jnp.maximum(b_end + m, jnp.max(g_row, axis=-1, keepdims=True))
        decay = jnp.exp(b_end + m - m_new)
        w_row = jnp.exp(g_row - m_new)
        upd = jnp.dot((kt.astype(F32) * w_row).astype(BF16), v_aug, preferred_element_type=F32)
        st[bb, d, hh] = (sidx, dec, a_inter, floor, v_aug, qk, ct, inter, m_new, decay, upd)

    sv = {}
    for key in chains:
        sidx, dec, a_inter, floor, v_aug, qk, ct, inter, m_new, decay, upd = st[key]
        sv[key] = jnp.dot((qk * dec).astype(BF16), v_aug, preferred_element_type=F32)

    for key in chains:
        bb, d, hh = key
        sidx, dec, a_inter, floor, v_aug, qk, ct, inter, m_new, decay, upd = st[key]
        both = a_inter * inter + sv[key]
        scale = 1.0 / jnp.maximum(jnp.abs(both[:, ML_DV:]), floor)
        h_ref = refs[d][3]
        for blk in range(ML_DV // LANES):
            lo = hh * ML_DV + blk * LANES
            h_ref[bb, :, lo:lo + LANES] = both[:, blk * LANES:(blk + 1) * LANES] * scale
        ct_ref[sidx] = decay * ct + upd
        m_ref[sidx] = jnp.broadcast_to(m_new, (8, LANES))


def _mlstm(mq, mk_t, mv, mg, mg_t, bias_col, bias_row):
    b, t, _ = mq.shape
    L = ML_CHUNK
    nc = t // L
    sb = ML_SEQS if b % ML_SEQS == 0 else 1
    fw = lambda width: pl.BlockSpec((sb, L, width), lambda bi, c: (bi, c, 0))
    bw = lambda width: pl.BlockSpec((sb, L, width), lambda bi, c: (bi, nc - 1 - c, 0))
    fw_t = lambda rows: pl.BlockSpec((sb, rows, L), lambda bi, c: (bi, 0, c))
    bw_t = lambda rows: pl.BlockSpec((sb, rows, L), lambda bi, c: (bi, 0, nc - 1 - c))
    ns = sb * 2 * ML_HEADS
    return pl.pallas_call(
        _mlstm_kernel,
        grid=(b // sb, nc),
        in_specs=[fw(C_MQ), fw_t(C_MK), fw(C_MV), fw(LANES), fw_t(C_MG),
                  bw(C_MQ), bw_t(C_MK), bw(C_MV), bw(LANES), bw_t(C_MG),
                  _resident((1, LANES)), _resident((C_MG, L))],
        out_specs=[fw(C_MV), bw(C_MV)],
        out_shape=[jax.ShapeDtypeStruct((b, t, C_MV), F32)] * 2,
        scratch_shapes=[pltpu.VMEM((ns, ML_DQK, ML_DV + LANES), F32),
                        pltpu.VMEM((ns, 8, LANES), F32)],
        compiler_params=_params(("parallel", "arbitrary"), 32),
        name="mlstm",
    )(mq, mk_t, mv, mg, mg_t, mq, mk_t, mv, mg, mg_t, bias_col, bias_row)


def _moe_input(x1, n2_ref, mod_ref):
    return _rms(x1, n2_ref[...]) * (1.0 + mod_ref[0, 4:5, :]) + mod_ref[0, 3:4, :]


def _mix_kernel(x_ref, att_ref, hf_ref, hb_ref, mo_ref, mlw_ref, wo_ref, mod_ref, n2_ref,
                wrh_ref, wrl_ref, br_ref, x1_ref, lg_ref):
    ml = hf_ref[0] + hb_ref[0]
    gate = jax.nn.sigmoid(mo_ref[0])
    mix = jnp.dot(att_ref[0], wo_ref[0:C_AQ, :], preferred_element_type=F32)
    for hh in range(ML_HEADS):
        sl = slice(hh * ML_DV, (hh + 1) * ML_DV)
        seg = (_rms(ml[:, sl], mlw_ref[:, sl]) * gate[:, sl]).astype(BF16)
        mix += jnp.dot(seg, wo_ref[C_AQ + hh * ML_DV:C_AQ + (hh + 1) * ML_DV, :], preferred_element_type=F32)
    x1 = x_ref[0] + mod_ref[0, 2:3, :] * mix
    x1_ref[0] = x1
    h2 = _moe_input(x1, n2_ref, mod_ref)
    h_hi = h2.astype(BF16)
    h_lo = (h2 - h_hi.astype(F32)).astype(BF16)
    nt = (((1,), (1,)), ((), ()))
    both = lax.dot_general(jnp.concatenate([wrh_ref[...], wrl_ref[...]], axis=0), h_hi, nt, preferred_element_type=F32)
    lg = both[:ROUTER_ROWS] + both[ROUTER_ROWS:] + lax.dot_general(wrh_ref[...], h_lo, nt, preferred_element_type=F32)
    lg_ref[0] = lg + br_ref[:, 0:1]


def _mix_and_router(x, att, h_fw, h_bw, mo, ml_norm_w, w_out, mod, norm2_w, wr_hi, wr_lo, b_router):
    b, t, d = x.shape
    tm = min(TM_MIX, t)
    row = lambda width: pl.BlockSpec((1, tm, width), lambda bi, i: (bi, i, 0))
    return pl.pallas_call(
        _mix_kernel,
        grid=(b, t // tm),
        in_specs=[row(d), row(C_AQ), row(C_MV), row(C_MV), row(C_MO),
                  _resident((1, C_MV)), _resident((d, d)),
                  pl.BlockSpec((1, 6, d), lambda bi, i: (bi, 0, 0)),
                  _resident((1, d)), _resident((ROUTER_ROWS, d)), _resident((ROUTER_ROWS, d)),
                  _resident((ROUTER_ROWS, LANES))],
        out_specs=[row(d), pl.BlockSpec((1, ROUTER_ROWS, tm), lambda bi, i: (bi, 0, i))],
        out_shape=[jax.ShapeDtypeStruct((b, t, d), F32), jax.ShapeDtypeStruct((b, ROUTER_ROWS, t), F32)],
        compiler_params=_params(("parallel", "arbitrary"), 48),
        name="mix_router",
    )(x, att, h_fw, h_bw, mo, ml_norm_w, w_out, mod, norm2_w, wr_hi, wr_lo, b_router)


def _first_argmax(rows):
    best = rows[0]
    idx = jnp.zeros_like(best)
    for j in range(1, len(rows)):
        better = rows[j] > best
        best = jnp.where(better, rows[j], best)
        idx = jnp.where(better, float(j), idx)
    return best, idx


def _softmax_rows(rows):
    mx = functools.reduce(jnp.maximum, rows)
    ex = [jnp.exp(r - mx) for r in rows]
    tot = functools.reduce(lambda a, c: a + c, ex)
    return [e / tot for e in ex]


def _route_kernel(lg_ref, idx_ref, wt_ref, cnt_ref, run_ref):
    @pl.when((pl.program_id(0) == 0) & (pl.program_id(1) == 0))
    def _():
        run_ref[...] = jnp.zeros_like(run_ref)

    tl = lg_ref.shape[2]
    lg = lg_ref[0]
    p_grp = _softmax_rows([lg[g:g + 1, :] for g in range(N_GROUPS)])
    p_g, g_idx = _first_argmax(p_grp)
    el = []
    for j in range(EXPERTS_PER_GROUP):
        sel = lg[N_GROUPS + (N_GROUPS - 1) * EXPERTS_PER_GROUP + j:N_GROUPS + (N_GROUPS - 1) * EXPERTS_PER_GROUP + j + 1, :]
        for g in range(N_GROUPS - 2, -1, -1):
            r = N_GROUPS + g * EXPERTS_PER_GROUP + j
            sel = jnp.where(g_idx == float(g), lg[r:r + 1, :], sel)
        el.append(sel)
    pe = _softmax_rows(el)
    w1, i1 = _first_argmax(pe)
    rest = [jnp.where(i1 == float(j), -1.0, pe[j]) for j in range(EXPERTS_PER_GROUP)]
    w2, i2 = _first_argmax(rest)
    tot = w1 + w2
    wt1 = w1 / tot * p_g
    wt2 = w2 / tot * p_g
    lo = jnp.minimum(i1, i2)
    hi = jnp.maximum(i1, i2)
    pair = jnp.where(hi == 3.0, lo, jnp.where(lo == 1.0, 3.0, jnp.where(hi == 1.0, 4.0, 5.0)))
    slot_a = functools.reduce(lambda acc, p: jnp.where(pair == float(p), float(PAIR_SLOT_A[p]), acc),
                              range(1, N_PAIRS), jnp.full_like(pair, float(PAIR_SLOT_A[0])))
    first_in_a = slot_a == i1
    wa = jnp.where(first_in_a, wt1, wt2)
    wb = jnp.where(first_in_a, wt2, wt1)
    bucket = (g_idx * float(N_PAIRS) + pair).astype(I32)

    brow = lax.broadcasted_iota(I32, (BUCKET_ROWS, tl), 0)
    hit = brow == bucket
    onehot = hit.astype(F32)
    before = lax.broadcasted_iota(I32, (tl, tl), 0) <= lax.broadcasted_iota(I32, (tl, tl), 1)
    incl = jnp.dot(onehot.astype(BF16), before.astype(BF16), preferred_element_type=F32)
    base = run_ref[:, 0:1] + incl - onehot
    rank = jnp.sum(jnp.where(hit, base, 0.0), axis=0, keepdims=True).astype(I32)
    total = run_ref[...] + jnp.sum(onehot, axis=-1, keepdims=True)
    run_ref[...] = total
    cnt_ref[...] = total.astype(I32)

    idx_ref[0] = jnp.concatenate([bucket, rank, jnp.zeros((6, tl), I32)], axis=0)
    wt_ref[0] = jnp.concatenate([wa, wb, jnp.zeros((6, tl), F32)], axis=0)


def _route(logits_t):
    b, _, t = logits_t.shape
    tl = min(TL_ROUTE, t)
    blk = lambda rows: pl.BlockSpec((1, rows, tl), lambda bi, i: (bi, 0, i))
    return pl.pallas_call(
        _route_kernel,
        grid=(b, t // tl),
        in_specs=[blk(ROUTER_ROWS)],
        out_specs=[blk(8), blk(8), pl.BlockSpec((BUCKET_ROWS, LANES), lambda bi, i: (0, 0))],
        out_shape=[jax.ShapeDtypeStruct((b, 8, t), I32), jax.ShapeDtypeStruct((b, 8, t), F32),
                   jax.ShapeDtypeStruct((BUCKET_ROWS, LANES), I32)],
        scratch_shapes=[pltpu.VMEM((BUCKET_ROWS, LANES), F32)],
        compiler_params=_params(("arbitrary", "arbitrary"), 32),
        name="route",
    )(logits_t)


def _row_copy(src_ref, src_row, dst_ref, dst_row, sem):
    return pltpu.make_async_copy(src_ref.at[pl.ds(src_row, 1), :], dst_ref.at[pl.ds(dst_row, 1), :], sem)


def _dispatch_kernel(padlo_ref, padhi_ref, dest_ref, x1_ref, wtail_ref, n2_ref, mod_ref, xs_ref, h_ref, zero_ref, sem):
    td, d = x1_ref.shape
    i = pl.program_id(0)
    slot = i % 2
    rows = h_ref.at[slot]
    rows[:, 0:d] = _moe_input(x1_ref[...], n2_ref, mod_ref)
    rows[:, d:] = wtail_ref[...]

    def start(t, carry):
        _row_copy(rows, t, xs_ref, dest_ref[0, 0, t], sem.at[slot]).start()
        return carry

    lax.fori_loop(0, td, start, 0, unroll=DMA_UNROLL)

    def drain(which):
        pltpu.make_async_copy(h_ref.at[which], xs_ref.at[pl.ds(0, td), :], sem.at[which]).wait()

    @pl.when(i > 0)
    def _():
        drain(1 - slot)

    @pl.when(i == pl.num_programs(0) - 1)
    def _():
        drain(slot)
        zero_ref[...] = jnp.zeros_like(zero_ref)
        zsem = sem.at[0]

        def zero_start(r, carry):
            _row_copy(zero_ref, 0, xs_ref, r, zsem).start()
            return carry

        def zero_wait(r, carry):
            _row_copy(zero_ref, 0, xs_ref, r, zsem).wait()
            return carry

        for e in range(N_BUCKETS):
            lax.fori_loop(padlo_ref[e], padhi_ref[e], zero_start, 0)
        for e in range(N_BUCKETS):
            lax.fori_loop(padlo_ref[e], padhi_ref[e], zero_wait, 0)


def _dispatch(x1, dest, w_tail, norm2_w, mod, pad_lo, pad_hi, m_pad):
    n, d = x1.shape
    b, _, t = dest.shape
    td = min(TD_DISPATCH, t)
    per = t // td
    width = d + LANES
    return pl.pallas_call(
        _dispatch_kernel,
        grid_spec=pltpu.PrefetchScalarGridSpec(
            num_scalar_prefetch=2,
            grid=(n // td,),
            in_specs=[pl.BlockSpec((1, 1, td), lambda i, *_: (i // per, 0, i % per), memory_space=pltpu.SMEM),
                      pl.BlockSpec((td, d), lambda i, *_: (i, 0)),
                      pl.BlockSpec((td, LANES), lambda i, *_: (i, 0)),
                      pl.BlockSpec((1, d), lambda i, *_: (0, 0)),
                      pl.BlockSpec((1, 6, d), lambda i, *_: (i // per, 0, 0))],
            out_specs=pl.BlockSpec(memory_space=pl.ANY),
            scratch_shapes=[pltpu.VMEM((2, td, width), F32), pltpu.VMEM((8, width), F32),
                            pltpu.SemaphoreType.DMA((2,))]),
        out_shape=jax.ShapeDtypeStruct((m_pad, width), F32),
        compiler_params=_params(("arbitrary",), 32),
        name="dispatch",
    )(pad_lo, pad_hi, dest, x1, w_tail, norm2_w, mod)


def _moe_kernel(ea_ref, eb_ref, nused_ref, x_ref, wga_ref, wua_ref, wda_ref, wgb_ref, wub_ref, wdb_ref, o_ref):
    r = pl.program_id(0)
    d = o_ref.shape[1]

    @pl.when(r < nused_ref[0])
    def _():
        xb = x_ref[:, 0:d].astype(BF16)

        def expert(wg_ref, wu_ref, wd_ref):
            g = jnp.dot(xb, wg_ref[0], preferred_element_type=F32)
            u = jnp.dot(xb, wu_ref[0], preferred_element_type=F32)
            h = (g * jax.nn.sigmoid(g) * u).astype(BF16)
            return jnp.dot(h, wd_ref[0], preferred_element_type=F32)

        tail = x_ref[:, d:]
        o_ref[...] = (expert(wga_ref, wua_ref, wda_ref) * _lane_pick(tail, 0)
                      + expert(wgb_ref, wub_ref, wdb_ref) * _lane_pick(tail, 1))

    @pl.when(r >= nused_ref[0])
    def _():
        o_ref[...] = jnp.zeros_like(o_ref)


def _moe_experts(xs, blk_ea, blk_eb, nused, w_gate, w_up, w_down):
    m_pad, width = xs.shape
    d = width - LANES
    f = w_gate.shape[2]
    tm = TM_MOE
    last = lambda r, nu: jnp.minimum(r, nu[0] - 1)
    slot = lambda shape, pick, bufs: pl.BlockSpec(shape, lambda r, ea, eb, nu: (pick(ea, eb)[last(r, nu)], 0, 0),
                                                  pipeline_mode=pl.Buffered(bufs))
    slot_a = lambda shape: slot(shape, lambda ea, eb: ea, 2)
    slot_b = lambda shape: slot(shape, lambda ea, eb: eb, 1)
    return pl.pallas_call(
        _moe_kernel,
        grid_spec=pltpu.PrefetchScalarGridSpec(
            num_scalar_prefetch=3,
            grid=(m_pad // tm,),
            in_specs=[pl.BlockSpec((tm, width), lambda r, ea, eb, nu: (last(r, nu), 0)),
                      slot_a((1, d, f)), slot_a((1, d, f)), slot_a((1, f, d)),
                      slot_b((1, d, f)), slot_b((1, d, f)), slot_b((1, f, d))],
            out_specs=pl.BlockSpec((tm, d), lambda r, ea, eb, nu: (r, 0))),
        out_shape=jax.ShapeDtypeStruct((m_pad, d), F32),
        compiler_params=_params(("arbitrary",), 58),
        name="moe_experts",
    )(blk_ea, blk_eb, nused, xs, w_gate, w_up, w_down, w_gate, w_up, w_down)


def _combine_kernel(dest_ref, dnext_ref, x1_ref, mod_ref, o_ref, y_ref, g_ref, sem):
    tc = x1_ref.shape[0]
    i = pl.program_id(0)
    slot = i % 2

    def gather(idx_ref, which):
        def start(t, carry):
            _row_copy(o_ref, idx_ref[0, 0, t], g_ref.at[which], t, sem.at[which]).start()
            return carry

        lax.fori_loop(0, tc, start, 0, unroll=DMA_UNROLL)

    @pl.when(i == 0)
    def _():
        gather(dest_ref, slot)

    @pl.when(i + 1 < pl.num_programs(0))
    def _():
        gather(dnext_ref, 1 - slot)

    pltpu.make_async_copy(o_ref.at[pl.ds(0, tc), :], g_ref.at[slot], sem.at[slot]).wait()
    y_ref[...] = x1_ref[...] + mod_ref[0, 5:6, :] * g_ref[slot]


def _combine(x1, dest, mod, o_rows):
    n, d = x1.shape
    b, _, t = dest.shape
    tc = min(TC_COMBINE, t)
    per = t // tc
    last = n // tc - 1
    tile = lambda i: (i // per, 0, i % per)
    return pl.pallas_call(
        _combine_kernel,
        grid=(n // tc,),
        in_specs=[pl.BlockSpec((1, 1, tc), tile, memory_space=pltpu.SMEM),
                  pl.BlockSpec((1, 1, tc), lambda i: tile(jnp.minimum(i + 1, last)), memory_space=pltpu.SMEM),
                  pl.BlockSpec((tc, d), lambda i: (i, 0)),
                  pl.BlockSpec((1, 6, d), lambda i: (i // per, 0, 0)),
                  pl.BlockSpec(memory_space=pl.ANY)],
        out_specs=pl.BlockSpec((tc, d), lambda i: (i, 0)),
        scratch_shapes=[pltpu.VMEM((2, tc, d), F32), pltpu.SemaphoreType.DMA((2,))],
        out_shape=jax.ShapeDtypeStruct((n, d), F32),
        compiler_params=_params(("arbitrary",), 32),
        name="combine",
    )(dest, dest, x1, mod, o_rows)


def _rope_tables(t):
    rows = t // GRID_W
    row = jnp.repeat(jnp.arange(rows, dtype=F32), GRID_W)
    col = jnp.tile(jnp.arange(GRID_W, dtype=F32), rows)
    freqs = ROPE_THETA ** (-jnp.arange(ROPE_PAIRS, dtype=F32) / ROPE_PAIRS)
    ar = row[:, None] * freqs
    ac = col[:, None] * freqs
    cos_t = jnp.concatenate([jnp.cos(ar), jnp.cos(ar), jnp.cos(ac), jnp.cos(ac)], axis=-1)
    sin_t = jnp.concatenate([-jnp.sin(ar), jnp.sin(ar), -jnp.sin(ac), jnp.sin(ac)], axis=-1)
    return cos_t, sin_t


def _encoder_layer(x, mod, p):
    b, t, d = x.shape
    n = b * t
    cos_t, sin_t = _rope_tables(t)
    aq, ak, av_t, mq, mk_t, mv, mo, mg = _in_projection(
        x, mod, p["norm1_w"], p["w_main"], p["w_v_t"], p["w_gates"], p["q_norm_w"], p["k_norm_w"], cos_t, sin_t)
    att = _attention(aq, ak, av_t)
    mg_t = jnp.swapaxes(mg[:, :, :C_MG], 1, 2)
    h_fw, h_bw = _mlstm(mq, mk_t, mv, mg, mg_t, p["gate_bias_col"], p["gate_bias_row"])
    x1, logits_t = _mix_and_router(x, att, h_fw, h_bw, mo, p["ml_norm_w"], p["w_out"], mod,
                                       p["norm2_w"], p["wr_hi"], p["wr_lo"], p["b_router"])
    idx, wts, counts = _route(logits_t)

    tm = TM_MOE
    counts = counts[:N_BUCKETS, 0]
    padded = (counts + tm - 1) // tm * tm
    pend = jnp.cumsum(padded)
    pstart = (pend - padded).astype(I32)
    nb = (n + tm - 1) // tm + N_BUCKETS
    m_pad = nb * tm
    block_row0 = jnp.arange(nb, dtype=I32) * tm
    blk_bucket = jnp.minimum(jnp.sum(pend[None, :] <= block_row0[:, None], axis=1), N_BUCKETS - 1)
    group_base = np.repeat(np.arange(N_GROUPS) * EXPERTS_PER_GROUP, N_PAIRS)
    expert_a = jnp.asarray(group_base + np.tile(PAIR_SLOT_A, N_GROUPS), I32)
    expert_b = jnp.asarray(group_base + np.tile(PAIR_SLOT_B, N_GROUPS), I32)
    blk_onehot = blk_bucket[:, None] == jnp.arange(N_BUCKETS)
    blk_ea = jnp.sum(jnp.where(blk_onehot, expert_a, 0), axis=1).astype(I32)
    blk_eb = jnp.sum(jnp.where(blk_onehot, expert_b, 0), axis=1).astype(I32)
    nused = (pend[-1:] // tm).astype(I32)
    onehot = idx[:, 0:1, :, None] == jnp.arange(N_BUCKETS, dtype=I32)
    dest = jnp.sum(jnp.where(onehot, pstart, 0), axis=-1) + idx[:, 1:2, :]

    pad_lo = (pstart + counts).astype(I32)
    pad_hi = jnp.concatenate([pstart[1:], jnp.full((1,), m_pad, I32)])
    x1 = x1.reshape(n, d)
    w_tail = jnp.zeros((n, LANES), F32).at[:, :2].set(jnp.swapaxes(wts[:, :2, :], 1, 2).reshape(n, 2))
    xs = _dispatch(x1, dest, w_tail, p["norm2_w"], mod, pad_lo, pad_hi, m_pad)
    o_rows = _moe_experts(xs, blk_ea, blk_eb, nused, p["w_gate"], p["w_up"], p["w_down"])
    y = _combine(x1, dest, mod, o_rows)
    return y.reshape(b, t, d)


def kernel(x_prompt, x_sample, c_prompt, c_sample, norm1_w, norm2_w, w_ada, b_ada, w_in, q_norm_w, k_norm_w, b_igate, b_fgate, ml_norm_w, w_out, w_gr, b_gr, w_er, b_er, w_gate, w_up, w_down):
    depth = norm1_w.shape[0]
    d = x_prompt.shape[-1]
    bp = x_prompt.shape[0]
    bs = x_sample.shape[0]
    rows = -(-(bp + bs) // 8) * 8
    y_prompt, y_sample = x_prompt, x_sample
    for l in range(depth):
        c_pad = jnp.zeros((rows, d), F32).at[:bp].set(c_prompt).at[bp:bp + bs].set(c_sample)
        mod = _ada_modulation(c_pad, w_ada[l], b_ada[l]).reshape(rows, 6, d)
        gate_bias = jnp.concatenate([b_igate[l].reshape(-1), b_fgate[l].reshape(-1)])
        w_router = jnp.concatenate([w_gr[l], w_er[l]], axis=1).T
        w_router = jnp.zeros((ROUTER_ROWS, d), F32).at[:N_GROUPS + N_EXPERTS].set(w_router)
        wr_hi = w_router.astype(BF16)
        b_router = jnp.zeros((ROUTER_ROWS,), F32).at[:N_GROUPS + N_EXPERTS].set(jnp.concatenate([b_gr[l], b_er[l]]))
        p = {
            "norm1_w": norm1_w[l].reshape(1, d),
            "norm2_w": norm2_w[l].reshape(1, d),
            "w_main": w_in[l][:, :C_MAIN].astype(BF16),
            "w_v_t": jnp.concatenate([w_in[l][:, C_AQ + C_AK:C_AQ + C_AK + C_AV],
                                      w_in[l][:, C_AQ + C_AK + C_AV + C_MQ:C_AQ + C_AK + C_AV + C_MQ + C_MK]],
                                     axis=1).T.astype(BF16),
            "w_gates": jnp.zeros((d, LANES), BF16).at[:, :C_MG].set(w_in[l][:, C_MAIN:].astype(BF16)),
            "q_norm_w": q_norm_w[l].reshape(1, HEAD_DIM),
            "k_norm_w": k_norm_w[l].reshape(1, HEAD_DIM),
            "gate_bias_col": jnp.zeros((1, LANES), F32).at[0, :C_MG].set(gate_bias),
            "gate_bias_row": jnp.broadcast_to(gate_bias[:, None], (C_MG, ML_CHUNK)),
            "ml_norm_w": ml_norm_w[l].reshape(1, C_MV),
            "w_out": w_out[l].astype(BF16),
            "wr_hi": wr_hi,
            "wr_lo": (w_router - wr_hi.astype(F32)).astype(BF16),
            "b_router": jnp.broadcast_to(b_router[:, None], (ROUTER_ROWS, LANES)),
            "w_gate": w_gate[l].astype(BF16),
            "w_up": w_up[l].astype(BF16),
            "w_down": w_down[l].astype(BF16),
        }
        y_prompt = _encoder_layer(y_prompt, mod[:bp], p)
        y_sample = _encoder_layer(y_sample, mod[bp:bp + bs], p)
    return (y_prompt, y_sample)
```

```python
import functools

import jax
import jax.numpy as jnp
import numpy as np
from jax import lax
from jax.experimental import pallas as pl
from jax.experimental.pallas import tpu as pltpu

F32 = jnp.float32
BF16 = jnp.bfloat16
I32 = jnp.int32

GRID_W = 64
HEAD_DIM = 128
ATT_HEADS = 8
ATT_KV_HEADS = 2
ATT_GROUP = ATT_HEADS // ATT_KV_HEADS
ROPE_THETA = 10000.0
ROPE_PAIRS = HEAD_DIM // 4
ML_HEADS = 4
ML_DV = 256
ML_DQK = 128
ML_CHUNK = 128
N_GROUPS = 4
EXPERTS_PER_GROUP = 4
N_EXPERTS = N_GROUPS * EXPERTS_PER_GROUP
EPS = 1e-6
Q_SCALE = HEAD_DIM ** -0.5 * float(np.log2(np.e))

C_AQ = ATT_HEADS * HEAD_DIM
C_AK = ATT_KV_HEADS * HEAD_DIM
C_AV = ATT_KV_HEADS * HEAD_DIM
C_MQ = ML_HEADS * ML_DQK
C_MK = ML_HEADS * ML_DQK
C_MV = ML_HEADS * ML_DV
C_MO = ML_HEADS * ML_DV
C_MG = 4 * ML_HEADS
C_MAIN = C_AQ + C_AK + C_AV + C_MQ + C_MK + C_MV + C_MO

N_PAIRS = 6
PAIR_SLOT_A = (0, 1, 2, 2, 0, 0)
PAIR_SLOT_B = (3, 3, 3, 1, 1, 2)
N_BUCKETS = N_GROUPS * N_PAIRS
BUCKET_ROWS = 32

LANES = 128
ROUTER_ROWS = 32
MIB = 1024 * 1024

TM_PROJ = 256
TQ_ATT = 256
TM_MIX = 512
TL_ROUTE = 512
TD_DISPATCH = 256
TM_MOE = 256
TC_COMBINE = 256
ML_SEQS = 4
ATT_SOFTMAX_LAG = 2
ATT_VALUES_LAG = 1
ATT_SLOTS = 4
ROW_CHUNK = 32
DMA_UNROLL = 8


def _params(semantics, vmem_mib):
    return pltpu.CompilerParams(dimension_semantics=semantics, vmem_limit_bytes=vmem_mib * MIB)


def _resident(shape):
    nd = len(shape)
    return pl.BlockSpec(shape, lambda *_: (0,) * nd, pipeline_mode=pl.Buffered(1))


def _ada_kernel(c_ref, w_ref, b_ref, o_ref):
    c = c_ref[...]
    s = (c * jax.nn.sigmoid(c)).astype(BF16)
    o_ref[...] = jnp.dot(s, w_ref[...].astype(BF16), preferred_element_type=F32) + b_ref[...]


def _ada_modulation(c_pad, w_ada, b_ada):
    rows, d = c_pad.shape
    n = w_ada.shape[1]
    tn = 1024
    return pl.pallas_call(
        _ada_kernel,
        grid=(n // tn,),
        in_specs=[pl.BlockSpec((rows, d), lambda j: (0, 0)),
                  pl.BlockSpec((d, tn), lambda j: (0, j)),
                  pl.BlockSpec((1, tn), lambda j: (0, j))],
        out_specs=pl.BlockSpec((rows, tn), lambda j: (0, j)),
        out_shape=jax.ShapeDtypeStruct((rows, n), F32),
        compiler_params=_params(("arbitrary",), 40),
        name="ada_modulation",
    )(c_pad, w_ada, b_ada.reshape(1, n))


def _rms(x, w):
    return x * lax.rsqrt(jnp.mean(x * x, axis=-1, keepdims=True) + EPS) * w


def _inproj_kernel(x_ref, mod_ref, n1_ref, w_ref, wvt_ref, wg_ref, qn_ref, kn_ref, cos_ref, sin_ref,
                   aq_ref, ak_ref, avt_ref, mq_ref, mkt_ref, mv_ref, mo_ref, mg_ref):
    x = x_ref[0]
    h = _rms(x, n1_ref[...]) * (1.0 + mod_ref[0, 1:2, :]) + mod_ref[0, 0:1, :]
    hb = h.astype(BF16)

    def proj(c0, width):
        return jnp.dot(hb, w_ref[:, c0:c0 + width], preferred_element_type=F32)

    cos = cos_ref[...]
    sin = sin_ref[...]
    lane = lax.broadcasted_iota(I32, (1, HEAD_DIM), 1)
    first = (lane % (2 * ROPE_PAIRS)) < ROPE_PAIRS

    def norm_rope(p, w):
        pn = _rms(p, w)
        partner = jnp.where(first, pltpu.roll(pn, HEAD_DIM - ROPE_PAIRS, 1), pltpu.roll(pn, ROPE_PAIRS, 1))
        return pn * cos + partner * sin

    c0 = 0
    for half in range(2):
        p = proj(c0, C_AQ // 2)
        for hh in range(ATT_HEADS // 2):
            col = half * (C_AQ // 2) + hh * HEAD_DIM
            qh = norm_rope(p[:, hh * HEAD_DIM:(hh + 1) * HEAD_DIM], qn_ref[...])
            aq_ref[0, :, col:col + HEAD_DIM] = (qh * Q_SCALE).astype(BF16)
        c0 += C_AQ // 2
    p = proj(c0, C_AK)
    for hh in range(ATT_KV_HEADS):
        ak_ref[0, :, hh * HEAD_DIM:(hh + 1) * HEAD_DIM] = norm_rope(p[:, hh * HEAD_DIM:(hh + 1) * HEAD_DIM], kn_ref[...]).astype(BF16)
    nt = (((1,), (1,)), ((), ()))
    avt_ref[0] = lax.dot_general(wvt_ref[0:C_AV, :], hb, nt, preferred_element_type=F32).astype(BF16)
    mkt_ref[0] = lax.dot_general(wvt_ref[C_AV:, :], hb, nt, preferred_element_type=F32).astype(BF16)
    c0 += C_AK + C_AV
    mq_ref[0] = (proj(c0, C_MQ) * (ML_DQK ** -0.5)).astype(BF16)
    c0 += C_MQ + C_MK
    for half in range(2):
        mv_ref[0, :, half * 512:(half + 1) * 512] = proj(c0, 512).astype(BF16)
        c0 += 512
    for half in range(2):
        mo_ref[0, :, half * 512:(half + 1) * 512] = proj(c0, 512)
        c0 += 512
    mg_ref[0] = jnp.dot(hb, wg_ref[...], preferred_element_type=F32)


def _in_projection(x, mod, norm1_w, w_main, w_v_t, w_gates, q_norm_w, k_norm_w, cos_t, sin_t):
    b, t, d = x.shape
    tm = min(TM_PROJ, t)
    row = lambda width: pl.BlockSpec((1, tm, width), lambda bi, i: (bi, i, 0))
    outs = ((C_AQ, False), (C_AK, False), (C_AV, True), (C_MQ, False), (C_MK, True), (C_MV, False),
            (C_MO, False), (LANES, False))
    dtypes = (BF16, BF16, BF16, BF16, BF16, BF16, F32, F32)
    col = lambda width: pl.BlockSpec((1, width, tm), lambda bi, i: (bi, 0, i))
    return pl.pallas_call(
        _inproj_kernel,
        grid=(b, t // tm),
        in_specs=[row(d),
                  pl.BlockSpec((1, 6, d), lambda bi, i: (bi, 0, 0)),
                  _resident((1, d)),
                  _resident((d, C_MAIN)),
                  _resident((C_AV + C_MK, d)),
                  _resident((d, LANES)),
                  _resident((1, HEAD_DIM)),
                  _resident((1, HEAD_DIM)),
                  pl.BlockSpec((tm, HEAD_DIM), lambda bi, i: (i, 0)),
                  pl.BlockSpec((tm, HEAD_DIM), lambda bi, i: (i, 0))],
        out_specs=[col(w) if tr else row(w) for w, tr in outs],
        out_shape=[jax.ShapeDtypeStruct((b, w, t) if tr else (b, t, w), dt) for (w, tr), dt in zip(outs, dtypes)],
        compiler_params=_params(("parallel", "arbitrary"), 48),
        name="in_projection",
    )(x, mod, norm1_w, w_main, w_v_t, w_gates, q_norm_w, k_norm_w, cos_t, sin_t)


def _attention_kernel(q_ref, qn_ref, k_ref, vt_ref, o_ref, s_ref, p_ref, l_ref):
    def scores(g, src_ref=q_ref):
        kv = g // ATT_GROUP
        q = src_ref[0, :, g * HEAD_DIM:(g + 1) * HEAD_DIM]
        k = k_ref[0, :, kv * HEAD_DIM:(kv + 1) * HEAD_DIM]
        s_ref[g % ATT_SLOTS] = lax.dot_general(k, q, (((1,), (1,)), ((), ())), preferred_element_type=F32)

    def softmax(g):
        s = s_ref[g % ATT_SLOTS]
        m = jnp.max(s, axis=0, keepdims=True)
        p = jnp.exp2(s - m)
        l_ref[g] = jnp.broadcast_to(jnp.sum(p, axis=0, keepdims=True), l_ref.shape[1:])
        p_ref[g % ATT_SLOTS] = p.astype(BF16)

    def values(g):
        kv = g // ATT_GROUP
        vt = vt_ref[0, kv * HEAD_DIM:(kv + 1) * HEAD_DIM, :]
        o_t = jnp.dot(vt, p_ref[g % ATT_SLOTS], preferred_element_type=F32)
        o_ref[0, :, g * HEAD_DIM:(g + 1) * HEAD_DIM] = (o_t * (1.0 / l_ref[g][0:1, :])).T.astype(BF16)

    n = ATT_HEADS
    lag = ATT_SOFTMAX_LAG

    @pl.when(pl.program_id(1) == 0)
    def _():
        for g in range(lag):
            scores(g)

    for g in range(n):
        if g + lag < n:
            scores(g + lag)
        else:
            scores(g + lag - n, qn_ref)
        softmax(g)
        if g >= ATT_VALUES_LAG:
            values(g - ATT_VALUES_LAG)
    for g in range(n - ATT_VALUES_LAG, n):
        values(g)


def _attention(aq, ak, av_t):
    b, t, _ = aq.shape
    tq = min(TQ_ATT, t)
    nt = t // tq
    return pl.pallas_call(
        _attention_kernel,
        grid=(b, nt),
        in_specs=[pl.BlockSpec((1, tq, C_AQ), lambda bi, i: (bi, i, 0)),
                  pl.BlockSpec((1, tq, C_AQ), lambda bi, i: (bi, jnp.minimum(i + 1, nt - 1), 0)),
                  pl.BlockSpec((1, t, C_AK), lambda bi, i: (bi, 0, 0)),
                  pl.BlockSpec((1, C_AV, t), lambda bi, i: (bi, 0, 0))],
        out_specs=pl.BlockSpec((1, tq, C_AQ), lambda bi, i: (bi, i, 0)),
        out_shape=jax.ShapeDtypeStruct((b, t, C_AQ), BF16),
        scratch_shapes=[pltpu.VMEM((ATT_SLOTS, t, tq), F32), pltpu.VMEM((ATT_SLOTS, t, tq), BF16),
                        pltpu.VMEM((ATT_HEADS, 8, tq), F32)],
        compiler_params=_params(("parallel", "arbitrary"), 48),
        name="attention",
    )(aq, aq, ak, av_t)


def _log_sigmoid(x):
    return jnp.minimum(x, 0.0) - jnp.log1p(jnp.exp(-jnp.abs(x)))


def _split3(a):
    a1 = a.astype(BF16)
    r1 = a - a1.astype(F32)
    a2 = r1.astype(BF16)
    a3 = (r1 - a2.astype(F32)).astype(BF16)
    return a1, a2, a3


def _lane_pick(a, idx):
    lane = lax.broadcasted_iota(I32, a.shape, 1)
    return jnp.sum(jnp.where(lane == idx, a, 0.0), axis=-1, keepdims=True)


def _mlstm_kernel(qf, kf, vf, gcf, grf, qb, kb, vb, gcb, grb, bc_ref, br_ref, hf_ref, hb_ref,
                  ct_ref, m_ref):
    @pl.when(pl.program_id(1) == 0)
    def _():
        ct_ref[...] = jnp.zeros_like(ct_ref)
        m_ref[...] = jnp.zeros_like(m_ref)

    L = ML_CHUNK
    row = lax.broadcasted_iota(I32, (L, L), 0)
    col = lax.broadcasted_iota(I32, (L, L), 1)
    lower = (col <= row)
    upper = (col >= row)
    lower_b = lower.astype(BF16)
    upper_b = upper.astype(BF16)

    gates = {}
    for bb in range(qf.shape[0]):
        for d, (gc_ref, gr_ref) in enumerate(((gcf, grf), (gcb, grb))):
            cum_cols = upper_b if d else lower_b
            cum_rows = lower_b if d else upper_b
            pre_c = gc_ref[bb] + bc_ref[...]
            pre_r = gr_ref[bb] + br_ref[...]
            b_cols = sum(jnp.dot(cum_cols, part, preferred_element_type=F32) for part in _split3(_log_sigmoid(pre_c)))
            b_rows = sum(jnp.dot(part, cum_rows, preferred_element_type=F32) for part in _split3(_log_sigmoid(pre_r)))
            gates[bb, d] = (pre_r, b_cols, b_rows)

    chains = [(bb, d, hh) for bb in range(qf.shape[0]) for d in range(2) for hh in range(ML_HEADS)]
    refs = ((qf, kf, vf, hf_ref), (qb, kb, vb, hb_ref))
    ones_block = jnp.ones((L, LANES), BF16)


    st = {}
    for (bb, d, hh) in chains:
        q_ref, kt_ref, v_ref, _ = refs[d]
        pre_r, b_cols, b_rows = gates[bb, d]
        ci = d * ML_HEADS + hh
        cf = 2 * ML_HEADS + d * ML_HEADS + hh
        sidx = (bb * 2 + d) * ML_HEADS + hh
        i_row = pre_r[ci:ci + 1, :]
        b_row = b_rows[cf:cf + 1, :]
        m = m_ref[sidx][0:1, 0:1]
        c_vis = jnp.where(upper if d else lower, i_row - b_row, -jnp.inf)
        shift = -jnp.maximum(m, jnp.max(c_vis, axis=-1, keepdims=True))
        dec = jnp.exp(c_vis + shift)
        a_inter = jnp.exp(m + shift)
        floor = jnp.exp(shift - _lane_pick(b_cols, cf))
        q = q_ref[bb, :, hh * ML_DQK:(hh + 1) * ML_DQK]
        kt = kt_ref[bb, hh * ML_DQK:(hh + 1) * ML_DQK, :]
        v_aug = jnp.concatenate([v_ref[bb, :, hh * ML_DV:(hh + 1) * ML_DV], ones_block], axis=1)
        qk = jnp.dot(q, kt, preferred_element_type=F32)
        ct = ct_ref[sidx]
        inter = jnp.dot(q, ct.astype(BF16), preferred_element_type=F32)
        b_end = _lane_pick(b_row, 0 if d else L - 1)
        g_row = b_end - b_row + i_row
        m_new = jnp.maximum(b_end + m, jnp.max(g_row, axis=-1, keepdims=True))
        decay = jnp.exp(b_end + m - m_new)
        w_row = jnp.exp(g_row - m_new)
        upd = jnp.dot((kt.astype(F32) * w_row).astype(BF16), v_aug, preferred_element_type=F32)
        st[bb, d, hh] = (sidx, dec, a_inter, floor, v_aug, qk, ct, inter, m_new, decay, upd)

    sv = {}
    for key in chains:
        sidx, dec, a_inter, floor, v_aug, qk, ct, inter, m_new, decay, upd = st[key]
        sv[key] = jnp.dot((qk * dec).astype(BF16), v_aug, preferred_element_type=F32)

    for key in chains:
        bb, d, hh = key
        sidx, dec, a_inter, floor, v_aug, qk, ct, inter, m_new, decay, upd = st[key]
        both = a_inter * inter + sv[key]
        scale = 1.0 / jnp.maximum(jnp.abs(both[:, ML_DV:]), floor)
        h_ref = refs[d][3]
        for blk in range(ML_DV // LANES):
            lo = hh * ML_DV + blk * LANES
            h_ref[bb, :, lo:lo + LANES] = both[:, blk * LANES:(blk + 1) * LANES] * scale
        ct_ref[sidx] = decay * ct + upd
        m_ref[sidx] = jnp.broadcast_to(m_new, (8, LANES))


def _mlstm(mq, mk_t, mv, mg, mg_t, bias_col, bias_row):
    b, t, _ = mq.shape
    L = ML_CHUNK
    nc = t // L
    sb = ML_SEQS if b % ML_SEQS == 0 else 1
    fw = lambda width: pl.BlockSpec((sb, L, width), lambda bi, c: (bi, c, 0))
    bw = lambda width: pl.BlockSpec((sb, L, width), lambda bi, c: (bi, nc - 1 - c, 0))
    fw_t = lambda rows: pl.BlockSpec((sb, rows, L), lambda bi, c: (bi, 0, c))
    bw_t = lambda rows: pl.BlockSpec((sb, rows, L), lambda bi, c: (bi, 0, nc - 1 - c))
    ns = sb * 2 * ML_HEADS
    return pl.pallas_call(
        _mlstm_kernel,
        grid=(b // sb, nc),
        in_specs=[fw(C_MQ), fw_t(C_MK), fw(C_MV), fw(LANES), fw_t(C_MG),
                  bw(C_MQ), bw_t(C_MK), bw(C_MV), bw(LANES), bw_t(C_MG),
                  _resident((1, LANES)), _resident((C_MG, L))],
        out_specs=[fw(C_MV), bw(C_MV)],
        out_shape=[jax.ShapeDtypeStruct((b, t, C_MV), F32)] * 2,
        scratch_shapes=[pltpu.VMEM((ns, ML_DQK, ML_DV + LANES), F32),
                        pltpu.VMEM((ns, 8, LANES), F32)],
        compiler_params=_params(("parallel", "arbitrary"), 48),
        name="mlstm",
    )(mq, mk_t, mv, mg, mg_t, mq, mk_t, mv, mg, mg_t, bias_col, bias_row)


def _moe_input(x1, n2_ref, mod_ref):
    return _rms(x1, n2_ref[...]) * (1.0 + mod_ref[0, 4:5, :]) + mod_ref[0, 3:4, :]


def _mix_kernel(x_ref, att_ref, hf_ref, hb_ref, mo_ref, mlw_ref, wo_ref, mod_ref, n2_ref,
                wrh_ref, wrl_ref, br_ref, x1_ref, lg_ref):
    ml = hf_ref[0] + hb_ref[0]
    gate = jax.nn.sigmoid(mo_ref[0])
    mix = jnp.dot(att_ref[0], wo_ref[0:C_AQ, :], preferred_element_type=F32)
    for hh in range(ML_HEADS):
        sl = slice(hh * ML_DV, (hh + 1) * ML_DV)
        seg = (_rms(ml[:, sl], mlw_ref[:, sl]) * gate[:, sl]).astype(BF16)
        mix += jnp.dot(seg, wo_ref[C_AQ + hh * ML_DV:C_AQ + (hh + 1) * ML_DV, :], preferred_element_type=F32)
    x1 = x_ref[0] + mod_ref[0, 2:3, :] * mix
    x1_ref[0] = x1
    h2 = _moe_input(x1, n2_ref, mod_ref)
    h_hi = h2.astype(BF16)
    h_lo = (h2 - h_hi.astype(F32)).astype(BF16)
    nt = (((1,), (1,)), ((), ()))
    both = lax.dot_general(jnp.concatenate([wrh_ref[...], wrl_ref[...]], axis=0), h_hi, nt, preferred_element_type=F32)
    lg = both[:ROUTER_ROWS] + both[ROUTER_ROWS:] + lax.dot_general(wrh_ref[...], h_lo, nt, preferred_element_type=F32)
    lg_ref[0] = lg + br_ref[:, 0:1]


def _mix_and_router(x, att, h_fw, h_bw, mo, ml_norm_w, w_out, mod, norm2_w, wr_hi, wr_lo, b_router):
    b, t, d = x.shape
    tm = min(TM_MIX, t)
    row = lambda width: pl.BlockSpec((1, tm, width), lambda bi, i: (bi, i, 0))
    return pl.pallas_call(
        _mix_kernel,
        grid=(b, t // tm),
        in_specs=[row(d), row(C_AQ), row(C_MV), row(C_MV), row(C_MO),
                  _resident((1, C_MV)), _resident((d, d)),
                  pl.BlockSpec((1, 6, d), lambda bi, i: (bi, 0, 0)),
                  _resident((1, d)), _resident((ROUTER_ROWS, d)), _resident((ROUTER_ROWS, d)),
                  _resident((ROUTER_ROWS, LANES))],
        out_specs=[row(d), pl.BlockSpec((1, ROUTER_ROWS, tm), lambda bi, i: (bi, 0, i))],
        out_shape=[jax.ShapeDtypeStruct((b, t, d), F32), jax.ShapeDtypeStruct((b, ROUTER_ROWS, t), F32)],
        compiler_params=_params(("parallel", "arbitrary"), 48),
        name="mix_router",
    )(x, att, h_fw, h_bw, mo, ml_norm_w, w_out, mod, norm2_w, wr_hi, wr_lo, b_router)


def _first_argmax(rows):
    best = rows[0]
    idx = jnp.zeros_like(best)
    for j in range(1, len(rows)):
        better = rows[j] > best
        best = jnp.where(better, rows[j], best)
        idx = jnp.where(better, float(j), idx)
    return best, idx


def _softmax_rows(rows):
    mx = functools.reduce(jnp.maximum, rows)
    ex = [jnp.exp(r - mx) for r in rows]
    tot = functools.reduce(lambda a, c: a + c, ex)
    return [e / tot for e in ex]


def _route_kernel(lg_ref, idx_ref, wt_ref, cnt_ref, run_ref):
    @pl.when((pl.program_id(0) == 0) & (pl.program_id(1) == 0))
    def _():
        run_ref[...] = jnp.zeros_like(run_ref)

    tl = lg_ref.shape[2]
    lg = lg_ref[0]
    p_grp = _softmax_rows([lg[g:g + 1, :] for g in range(N_GROUPS)])
    p_g, g_idx = _first_argmax(p_grp)
    el = []
    for j in range(EXPERTS_PER_GROUP):
        sel = lg[N_GROUPS + (N_GROUPS - 1) * EXPERTS_PER_GROUP + j:N_GROUPS + (N_GROUPS - 1) * EXPERTS_PER_GROUP + j + 1, :]
        for g in range(N_GROUPS - 2, -1, -1):
            r = N_GROUPS + g * EXPERTS_PER_GROUP + j
            sel = jnp.where(g_idx == float(g), lg[r:r + 1, :], sel)
        el.append(sel)
    pe = _softmax_rows(el)
    w1, i1 = _first_argmax(pe)
    rest = [jnp.where(i1 == float(j), -1.0, pe[j]) for j in range(EXPERTS_PER_GROUP)]
    w2, i2 = _first_argmax(rest)
    tot = w1 + w2
    wt1 = w1 / tot * p_g
    wt2 = w2 / tot * p_g
    lo = jnp.minimum(i1, i2)
    hi = jnp.maximum(i1, i2)
    pair = jnp.where(hi == 3.0, lo, jnp.where(lo == 1.0, 3.0, jnp.where(hi == 1.0, 4.0, 5.0)))
    slot_a = functools.reduce(lambda acc, p: jnp.where(pair == float(p), float(PAIR_SLOT_A[p]), acc),
                              range(1, N_PAIRS), jnp.full_like(pair, float(PAIR_SLOT_A[0])))
    first_in_a = slot_a == i1
    wa = jnp.where(first_in_a, wt1, wt2)
    wb = jnp.where(first_in_a, wt2, wt1)
    bucket = (g_idx * float(N_PAIRS) + pair).astype(I32)

    brow = lax.broadcasted_iota(I32, (BUCKET_ROWS, tl), 0)
    hit = brow == bucket
    onehot = hit.astype(F32)
    before = lax.broadcasted_iota(I32, (tl, tl), 0) <= lax.broadcasted_iota(I32, (tl, tl), 1)
    incl = jnp.dot(onehot.astype(BF16), before.astype(BF16), preferred_element_type=F32)
    base = run_ref[:, 0:1] + incl - onehot
    rank = jnp.sum(jnp.where(hit, base, 0.0), axis=0, keepdims=True).astype(I32)
    total = run_ref[...] + jnp.sum(onehot, axis=-1, keepdims=True)
    run_ref[...] = total
    cnt_ref[...] = total.astype(I32)

    idx_ref[0] = jnp.concatenate([bucket, rank, jnp.zeros((6, tl), I32)], axis=0)
    wt_ref[0] = jnp.concatenate([wa, wb, jnp.zeros((6, tl), F32)], axis=0)


def _route(logits_t):
    b, _, t = logits_t.shape
    tl = min(TL_ROUTE, t)
    blk = lambda rows: pl.BlockSpec((1, rows, tl), lambda bi, i: (bi, 0, i))
    return pl.pallas_call(
        _route_kernel,
        grid=(b, t // tl),
        in_specs=[blk(ROUTER_ROWS)],
        out_specs=[blk(8), blk(8), pl.BlockSpec((BUCKET_ROWS, LANES), lambda bi, i: (0, 0))],
        out_shape=[jax.ShapeDtypeStruct((b, 8, t), I32), jax.ShapeDtypeStruct((b, 8, t), F32),
                   jax.ShapeDtypeStruct((BUCKET_ROWS, LANES), I32)],
        scratch_shapes=[pltpu.VMEM((BUCKET_ROWS, LANES), F32)],
        compiler_params=_params(("arbitrary", "arbitrary"), 32),
        name="route",
    )(logits_t)


def _row_copy(src_ref, src_row, dst_ref, dst_row, sem):
    return pltpu.make_async_copy(src_ref.at[pl.ds(src_row, 1), :], dst_ref.at[pl.ds(dst_row, 1), :], sem)


def _dispatch_kernel(padlo_ref, padhi_ref, dest_ref, x1_ref, wtail_ref, n2_ref, mod_ref, xs_ref, h_ref, zero_ref, sem):
    td, d = x1_ref.shape
    i = pl.program_id(0)
    slot = i % 2
    rows = h_ref.at[slot]
    for c0 in range(0, td, ROW_CHUNK):
        sl = slice(c0, c0 + ROW_CHUNK)
        rows[sl, 0:d] = _moe_input(x1_ref[sl, :], n2_ref, mod_ref)
        rows[sl, d:] = wtail_ref[sl, :]
        for t in range(c0, c0 + ROW_CHUNK):
            _row_copy(rows, t, xs_ref, dest_ref[0, 0, t], sem.at[slot]).start()

    def drain(which):
        pltpu.make_async_copy(h_ref.at[which], xs_ref.at[pl.ds(0, td), :], sem.at[which]).wait()

    @pl.when(i > 0)
    def _():
        drain(1 - slot)

    @pl.when(i == pl.num_programs(0) - 1)
    def _():
        drain(slot)
        zero_ref[...] = jnp.zeros_like(zero_ref)
        zsem = sem.at[0]

        sub = zero_ref.shape[0]

        def one_row(r):
            return _row_copy(zero_ref, 0, xs_ref, r, zsem)

        def aligned_rows(r):
            return pltpu.make_async_copy(zero_ref, xs_ref.at[pl.ds(pl.multiple_of(r, sub), sub), :], zsem)

        def spans(e):
            lo, hi = padlo_ref[e], padhi_ref[e]
            mid = jnp.minimum(hi, (lo + sub - 1) // sub * sub)
            return lo, mid, (hi - mid) // sub

        for wait in (False, True):
            for e in range(N_BUCKETS):
                lo, mid, nblk = spans(e)

                def head(r, carry):
                    one_row(r).wait() if wait else one_row(r).start()
                    return carry

                def body(c, carry):
                    cp = aligned_rows(mid + c * sub)
                    cp.wait() if wait else cp.start()
                    return carry

                lax.fori_loop(lo, mid, head, 0)
                lax.fori_loop(0, nblk, body, 0)


def _dispatch(x1, dest, w_tail, norm2_w, mod, pad_lo, pad_hi, m_pad):
    n, d = x1.shape
    b, _, t = dest.shape
    td = min(TD_DISPATCH, t)
    per = t // td
    width = d + LANES
    return pl.pallas_call(
        _dispatch_kernel,
        grid_spec=pltpu.PrefetchScalarGridSpec(
            num_scalar_prefetch=2,
            grid=(n // td,),
            in_specs=[pl.BlockSpec((1, 1, td), lambda i, *_: (i // per, 0, i % per), memory_space=pltpu.SMEM),
                      pl.BlockSpec((td, d), lambda i, *_: (i, 0)),
                      pl.BlockSpec((td, LANES), lambda i, *_: (i, 0)),
                      pl.BlockSpec((1, d), lambda i, *_: (0, 0)),
                      pl.BlockSpec((1, 6, d), lambda i, *_: (i // per, 0, 0))],
            out_specs=pl.BlockSpec(memory_space=pl.ANY),
            scratch_shapes=[pltpu.VMEM((2, td, width), F32), pltpu.VMEM((8, width), F32),
                            pltpu.SemaphoreType.DMA((2,))]),
        out_shape=jax.ShapeDtypeStruct((m_pad, width), F32),
        compiler_params=_params(("arbitrary",), 32),
        name="dispatch",
    )(pad_lo, pad_hi, dest, x1, w_tail, norm2_w, mod)


def _moe_kernel(ea_ref, eb_ref, nused_ref, x_ref, wga_ref, wua_ref, wda_ref, wgb_ref, wub_ref, wdb_ref, o_ref):
    r = pl.program_id(0)
    d = o_ref.shape[1]

    @pl.when(r < nused_ref[0])
    def _():
        xb = x_ref[:, 0:d].astype(BF16)

        def expert(wg_ref, wu_ref, wd_ref):
            g = jnp.dot(xb, wg_ref[0], preferred_element_type=F32)
            u = jnp.dot(xb, wu_ref[0], preferred_element_type=F32)
            h = (g * jax.nn.sigmoid(g) * u).astype(BF16)
            return jnp.dot(h, wd_ref[0], preferred_element_type=F32)

        tail = x_ref[:, d:]
        o_ref[...] = (expert(wga_ref, wua_ref, wda_ref) * _lane_pick(tail, 0)
                      + expert(wgb_ref, wub_ref, wdb_ref) * _lane_pick(tail, 1))

    @pl.when(r >= nused_ref[0])
    def _():
        o_ref[...] = jnp.zeros_like(o_ref)


def _moe_experts(xs, blk_ea, blk_eb, nused, w_gate, w_up, w_down):
    m_pad, width = xs.shape
    d = width - LANES
    f = w_gate.shape[2]
    tm = TM_MOE
    last = lambda r, nu: jnp.minimum(r, nu[0] - 1)
    slot = lambda shape, pick, bufs: pl.BlockSpec(shape, lambda r, ea, eb, nu: (pick(ea, eb)[last(r, nu)], 0, 0),
                                                  pipeline_mode=pl.Buffered(bufs))
    slot_a = lambda shape: slot(shape, lambda ea, eb: ea, 2)
    slot_b = lambda shape: slot(shape, lambda ea, eb: eb, 1)
    return pl.pallas_call(
        _moe_kernel,
        grid_spec=pltpu.PrefetchScalarGridSpec(
            num_scalar_prefetch=3,
            grid=(m_pad // tm,),
            in_specs=[pl.BlockSpec((tm, width), lambda r, ea, eb, nu: (last(r, nu), 0)),
                      slot_a((1, d, f)), slot_a((1, d, f)), slot_a((1, f, d)),
                      slot_b((1, d, f)), slot_b((1, d, f)), slot_b((1, f, d))],
            out_specs=pl.BlockSpec((tm, d), lambda r, ea, eb, nu: (r, 0))),
        out_shape=jax.ShapeDtypeStruct((m_pad, d), F32),
        compiler_params=_params(("arbitrary",), 58),
        name="moe_experts",
    )(blk_ea, blk_eb, nused, xs, w_gate, w_up, w_down, w_gate, w_up, w_down)


def _combine_kernel(dest_ref, dnext_ref, x1_ref, mod_ref, o_ref, y_ref, g_ref, sem):
    tc = x1_ref.shape[0]
    i = pl.program_id(0)
    slot = i % 2

    def drain(which):
        pltpu.make_async_copy(o_ref.at[pl.ds(0, tc), :], g_ref.at[which], sem.at[which]).wait()

    @pl.when(i == 0)
    def _():
        def start(t, carry):
            _row_copy(o_ref, dest_ref[0, 0, t], g_ref.at[slot], t, sem.at[slot]).start()
            return carry

        lax.fori_loop(0, tc, start, 0, unroll=DMA_UNROLL)

    drain(slot)
    rows = g_ref.at[slot]
    for c0 in range(0, tc, ROW_CHUNK):
        sl = slice(c0, c0 + ROW_CHUNK)
        y_ref[sl, :] = x1_ref[sl, :] + mod_ref[0, 5:6, :] * rows[sl, :]
        for t in range(c0, c0 + ROW_CHUNK):
            _row_copy(o_ref, dnext_ref[0, 0, t], g_ref.at[1 - slot], t, sem.at[1 - slot]).start()

    @pl.when(i == pl.num_programs(0) - 1)
    def _():
        drain(1 - slot)


def _combine(x1, dest, mod, o_rows):
    n, d = x1.shape
    b, _, t = dest.shape
    tc = min(TC_COMBINE, t)
    per = t // tc
    last = n // tc - 1
    tile = lambda i: (i // per, 0, i % per)
    return pl.pallas_call(
        _combine_kernel,
        grid=(n // tc,),
        in_specs=[pl.BlockSpec((1, 1, tc), tile, memory_space=pltpu.SMEM),
                  pl.BlockSpec((1, 1, tc), lambda i: tile(jnp.minimum(i + 1, last)), memory_space=pltpu.SMEM),
                  pl.BlockSpec((tc, d), lambda i: (i, 0)),
                  pl.BlockSpec((1, 6, d), lambda i: (i // per, 0, 0)),
                  pl.BlockSpec(memory_space=pl.ANY)],
        out_specs=pl.BlockSpec((tc, d), lambda i: (i, 0)),
        scratch_shapes=[pltpu.VMEM((2, tc, d), F32), pltpu.SemaphoreType.DMA((2,))],
        out_shape=jax.ShapeDtypeStruct((n, d), F32),
        compiler_params=_params(("arbitrary",), 32),
        name="combine",
    )(dest, dest, x1, mod, o_rows)


def _rope_tables(t):
    rows = t // GRID_W
    row = jnp.repeat(jnp.arange(rows, dtype=F32), GRID_W)
    col = jnp.tile(jnp.arange(GRID_W, dtype=F32), rows)
    freqs = ROPE_THETA ** (-jnp.arange(ROPE_PAIRS, dtype=F32) / ROPE_PAIRS)
    ar = row[:, None] * freqs
    ac = col[:, None] * freqs
    cos_t = jnp.concatenate([jnp.cos(ar), jnp.cos(ar), jnp.cos(ac), jnp.cos(ac)], axis=-1)
    sin_t = jnp.concatenate([-jnp.sin(ar), jnp.sin(ar), -jnp.sin(ac), jnp.sin(ac)], axis=-1)
    return cos_t, sin_t


def _encoder_layer(x, mod, p):
    b, t, d = x.shape
    n = b * t
    cos_t, sin_t = _rope_tables(t)
    aq, ak, av_t, mq, mk_t, mv, mo, mg = _in_projection(
        x, mod, p["norm1_w"], p["w_main"], p["w_v_t"], p["w_gates"], p["q_norm_w"], p["k_norm_w"], cos_t, sin_t)
    att = _attention(aq, ak, av_t)
    mg_t = jnp.swapaxes(mg[:, :, :C_MG], 1, 2)
    h_fw, h_bw = _mlstm(mq, mk_t, mv, mg, mg_t, p["gate_bias_col"], p["gate_bias_row"])
    x1, logits_t = _mix_and_router(x, att, h_fw, h_bw, mo, p["ml_norm_w"], p["w_out"], mod,
                                       p["norm2_w"], p["wr_hi"], p["wr_lo"], p["b_router"])
    idx, wts, counts = _route(logits_t)

    tm = TM_MOE
    counts = counts[:N_BUCKETS, 0]
    padded = (counts + tm - 1) // tm * tm
    pend = jnp.cumsum(padded)
    pstart = (pend - padded).astype(I32)
    nb = (n + tm - 1) // tm + N_BUCKETS
    m_pad = nb * tm
    block_row0 = jnp.arange(nb, dtype=I32) * tm
    blk_bucket = jnp.minimum(jnp.sum(pend[None, :] <= block_row0[:, None], axis=1), N_BUCKETS - 1)
    group_base = np.repeat(np.arange(N_GROUPS) * EXPERTS_PER_GROUP, N_PAIRS)
    expert_a = jnp.asarray(group_base + np.tile(PAIR_SLOT_A, N_GROUPS), I32)
    expert_b = jnp.asarray(group_base + np.tile(PAIR_SLOT_B, N_GROUPS), I32)
    blk_onehot = blk_bucket[:, None] == jnp.arange(N_BUCKETS)
    blk_ea = jnp.sum(jnp.where(blk_onehot, expert_a, 0), axis=1).astype(I32)
    blk_eb = jnp.sum(jnp.where(blk_onehot, expert_b, 0), axis=1).astype(I32)
    nused = (pend[-1:] // tm).astype(I32)
    onehot = idx[:, 0:1, :, None] == jnp.arange(N_BUCKETS, dtype=I32)
    dest = jnp.sum(jnp.where(onehot, pstart, 0), axis=-1) + idx[:, 1:2, :]

    pad_lo = (pstart + counts).astype(I32)
    pad_hi = jnp.concatenate([pstart[1:], jnp.full((1,), m_pad, I32)])
    x1 = x1.reshape(n, d)
    w_tail = jnp.zeros((n, LANES), F32).at[:, :2].set(jnp.swapaxes(wts[:, :2, :], 1, 2).reshape(n, 2))
    xs = _dispatch(x1, dest, w_tail, p["norm2_w"], mod, pad_lo, pad_hi, m_pad)
    o_rows = _moe_experts(xs, blk_ea, blk_eb, nused, p["w_gate"], p["w_up"], p["w_down"])
    y = _combine(x1, dest, mod, o_rows)
    return y.reshape(b, t, d)


def kernel(x_prompt, x_sample, c_prompt, c_sample, norm1_w, norm2_w, w_ada, b_ada, w_in, q_norm_w, k_norm_w, b_igate, b_fgate, ml_norm_w, w_out, w_gr, b_gr, w_er, b_er, w_gate, w_up, w_down):
    depth = norm1_w.shape[0]
    d = x_prompt.shape[-1]
    bp = x_prompt.shape[0]
    bs = x_sample.shape[0]
    rows = -(-(bp + bs) // 8) * 8
    y_prompt, y_sample = x_prompt, x_sample
    for l in range(depth):
        c_pad = jnp.zeros((rows, d), F32).at[:bp].set(c_prompt).at[bp:bp + bs].set(c_sample)
        mod = _ada_modulation(c_pad, w_ada[l], b_ada[l]).reshape(rows, 6, d)
        gate_bias = jnp.concatenate([b_igate[l].reshape(-1), b_fgate[l].reshape(-1)])
        w_router = jnp.concatenate([w_gr[l], w_er[l]], axis=1).T
        w_router = jnp.zeros((ROUTER_ROWS, d), F32).at[:N_GROUPS + N_EXPERTS].set(w_router)
        wr_hi = w_router.astype(BF16)
        b_router = jnp.zeros((ROUTER_ROWS,), F32).at[:N_GROUPS + N_EXPERTS].set(jnp.concatenate([b_gr[l], b_er[l]]))
        p = {
            "norm1_w": norm1_w[l].reshape(1, d),
            "norm2_w": norm2_w[l].reshape(1, d),
            "w_main": w_in[l][:, :C_MAIN].astype(BF16),
            "w_v_t": jnp.concatenate([w_in[l][:, C_AQ + C_AK:C_AQ + C_AK + C_AV],
                                      w_in[l][:, C_AQ + C_AK + C_AV + C_MQ:C_AQ + C_AK + C_AV + C_MQ + C_MK]],
                                     axis=1).T.astype(BF16),
            "w_gates": jnp.zeros((d, LANES), BF16).at[:, :C_MG].set(w_in[l][:, C_MAIN:].astype(BF16)),
            "q_norm_w": q_norm_w[l].reshape(1, HEAD_DIM),
            "k_norm_w": k_norm_w[l].reshape(1, HEAD_DIM),
            "gate_bias_col": jnp.zeros((1, LANES), F32).at[0, :C_MG].set(gate_bias),
            "gate_bias_row": jnp.broadcast_to(gate_bias[:, None], (C_MG, ML_CHUNK)),
            "ml_norm_w": ml_norm_w[l].reshape(1, C_MV),
            "w_out": w_out[l].astype(BF16),
            "wr_hi": wr_hi,
            "wr_lo": (w_router - wr_hi.astype(F32)).astype(BF16),
            "b_router": jnp.broadcast_to(b_router[:, None], (ROUTER_ROWS, LANES)),
            "w_gate": w_gate[l].astype(BF16),
            "w_up": w_up[l].astype(BF16),
            "w_down": w_down[l].astype(BF16),
        }
        y_prompt = _encoder_layer(y_prompt, mod[:bp], p)
        y_sample = _encoder_layer(y_sample, mod[bp:bp + bs], p)
    return (y_prompt, y_sample)
```

```python
import functools

import jax
import jax.numpy as jnp
import numpy as np
from jax import lax
from jax.experimental import pallas as pl
from jax.experimental.pallas import tpu as pltpu

F32 = jnp.float32
BF16 = jnp.bfloat16
I32 = jnp.int32

GRID_W = 64
HEAD_DIM = 128
ATT_HEADS = 8
ATT_KV_HEADS = 2
ATT_GROUP = ATT_HEADS // ATT_KV_HEADS
ROPE_THETA = 10000.0
ROPE_PAIRS = HEAD_DIM // 4
ML_HEADS = 4
ML_DV = 256
ML_DQK = 128
ML_CHUNK = 128
N_GROUPS = 4
EXPERTS_PER_GROUP = 4
N_EXPERTS = N_GROUPS * EXPERTS_PER_GROUP
EPS = 1e-6
Q_SCALE = HEAD_DIM ** -0.5 * float(np.log2(np.e))

C_AQ = ATT_HEADS * HEAD_DIM
C_AK = ATT_KV_HEADS * HEAD_DIM
C_AV = ATT_KV_HEADS * HEAD_DIM
C_MQ = ML_HEADS * ML_DQK
C_MK = ML_HEADS * ML_DQK
C_MV = ML_HEADS * ML_DV
C_MO = ML_HEADS * ML_DV
C_MG = 4 * ML_HEADS
C_MAIN = C_AQ + C_AK + C_AV + C_MQ + C_MK + C_MV + C_MO

N_PAIRS = 6
PAIR_SLOT_A = (0, 1, 2, 2, 0, 0)
PAIR_SLOT_B = (3, 3, 3, 1, 1, 2)
N_BUCKETS = N_GROUPS * N_PAIRS
BUCKET_ROWS = 32

LANES = 128
ROUTER_ROWS = 32
MIB = 1024 * 1024

TM_PROJ = 256
TQ_ATT = 256
TM_MIX = 512
TL_ROUTE = 512
TD_DISPATCH = 256
TM_MOE = 256
TC_COMBINE = 256
ML_SEQS = 4
ATT_SOFTMAX_LAG = 2
ATT_VALUES_LAG = 2
ATT_SLOTS = 4
ROW_CHUNK = 32
DMA_UNROLL = 8


def _params(semantics, vmem_mib):
    return pltpu.CompilerParams(dimension_semantics=semantics, vmem_limit_bytes=vmem_mib * MIB)


def _resident(shape):
    nd = len(shape)
    return pl.BlockSpec(shape, lambda *_: (0,) * nd, pipeline_mode=pl.Buffered(1))


def _ada_kernel(c_ref, w_ref, b_ref, o_ref):
    c = c_ref[...]
    s = (c * jax.nn.sigmoid(c)).astype(BF16)
    o_ref[...] = jnp.dot(s, w_ref[...].astype(BF16), preferred_element_type=F32) + b_ref[...]


def _ada_modulation(c_pad, w_ada, b_ada):
    rows, d = c_pad.shape
    n = w_ada.shape[1]
    tn = 1024
    return pl.pallas_call(
        _ada_kernel,
        grid=(n // tn,),
        in_specs=[pl.BlockSpec((rows, d), lambda j: (0, 0)),
                  pl.BlockSpec((d, tn), lambda j: (0, j)),
                  pl.BlockSpec((1, tn), lambda j: (0, j))],
        out_specs=pl.BlockSpec((rows, tn), lambda j: (0, j)),
        out_shape=jax.ShapeDtypeStruct((rows, n), F32),
        compiler_params=_params(("arbitrary",), 40),
        name="ada_modulation",
    )(c_pad, w_ada, b_ada.reshape(1, n))


def _rms(x, w):
    return x * lax.rsqrt(jnp.mean(x * x, axis=-1, keepdims=True) + EPS) * w


def _inproj_kernel(x_ref, mod_ref, n1_ref, w_ref, wvt_ref, wg_ref, qn_ref, kn_ref, cos_ref, sin_ref,
                   aq_ref, ak_ref, avt_ref, mq_ref, mkt_ref, mv_ref, mo_ref, mg_ref):
    x = x_ref[0]
    h = _rms(x, n1_ref[...]) * (1.0 + mod_ref[0, 1:2, :]) + mod_ref[0, 0:1, :]
    hb = h.astype(BF16)

    def proj(c0, width):
        return jnp.dot(hb, w_ref[:, c0:c0 + width], preferred_element_type=F32)

    cos = cos_ref[...]
    sin = sin_ref[...]
    lane = lax.broadcasted_iota(I32, (1, HEAD_DIM), 1)
    first = (lane % (2 * ROPE_PAIRS)) < ROPE_PAIRS

    def norm_rope(p, w):
        pn = _rms(p, w)
        partner = jnp.where(first, pltpu.roll(pn, HEAD_DIM - ROPE_PAIRS, 1), pltpu.roll(pn, ROPE_PAIRS, 1))
        return pn * cos + partner * sin

    c0 = 0
    for half in range(2):
        p = proj(c0, C_AQ // 2)
        for hh in range(ATT_HEADS // 2):
            col = half * (C_AQ // 2) + hh * HEAD_DIM
            qh = norm_rope(p[:, hh * HEAD_DIM:(hh + 1) * HEAD_DIM], qn_ref[...])
            aq_ref[0, :, col:col + HEAD_DIM] = (qh * Q_SCALE).astype(BF16)
        c0 += C_AQ // 2
    p = proj(c0, C_AK)
    for hh in range(ATT_KV_HEADS):
        ak_ref[0, :, hh * HEAD_DIM:(hh + 1) * HEAD_DIM] = norm_rope(p[:, hh * HEAD_DIM:(hh + 1) * HEAD_DIM], kn_ref[...]).astype(BF16)
    nt = (((1,), (1,)), ((), ()))
    avt_ref[0] = lax.dot_general(wvt_ref[0:C_AV, :], hb, nt, preferred_element_type=F32).astype(BF16)
    mkt_ref[0] = lax.dot_general(wvt_ref[C_AV:, :], hb, nt, preferred_element_type=F32).astype(BF16)
    c0 += C_AK + C_AV
    mq_ref[0] = (proj(c0, C_MQ) * (ML_DQK ** -0.5)).astype(BF16)
    c0 += C_MQ + C_MK
    for half in range(2):
        mv_ref[0, :, half * 512:(half + 1) * 512] = proj(c0, 512).astype(BF16)
        c0 += 512
    for half in range(2):
        mo_ref[0, :, half * 512:(half + 1) * 512] = proj(c0, 512)
        c0 += 512
    mg_ref[0] = jnp.dot(hb, wg_ref[...], preferred_element_type=F32)


def _in_projection(x, mod, norm1_w, w_main, w_v_t, w_gates, q_norm_w, k_norm_w, cos_t, sin_t):
    b, t, d = x.shape
    tm = min(TM_PROJ, t)
    row = lambda width: pl.BlockSpec((1, tm, width), lambda bi, i: (bi, i, 0))
    outs = ((C_AQ, False), (C_AK, False), (C_AV, True), (C_MQ, False), (C_MK, True), (C_MV, False),
            (C_MO, False), (LANES, False))
    dtypes = (BF16, BF16, BF16, BF16, BF16, BF16, F32, F32)
    col = lambda width: pl.BlockSpec((1, width, tm), lambda bi, i: (bi, 0, i))
    return pl.pallas_call(
        _inproj_kernel,
        grid=(b, t // tm),
        in_specs=[row(d),
                  pl.BlockSpec((1, 6, d), lambda bi, i: (bi, 0, 0)),
                  _resident((1, d)),
                  _resident((d, C_MAIN)),
                  _resident((C_AV + C_MK, d)),
                  _resident((d, LANES)),
                  _resident((1, HEAD_DIM)),
                  _resident((1, HEAD_DIM)),
                  pl.BlockSpec((tm, HEAD_DIM), lambda bi, i: (i, 0)),
                  pl.BlockSpec((tm, HEAD_DIM), lambda bi, i: (i, 0))],
        out_specs=[col(w) if tr else row(w) for w, tr in outs],
        out_shape=[jax.ShapeDtypeStruct((b, w, t) if tr else (b, t, w), dt) for (w, tr), dt in zip(outs, dtypes)],
        compiler_params=_params(("parallel", "arbitrary"), 48),
        name="in_projection",
    )(x, mod, norm1_w, w_main, w_v_t, w_gates, q_norm_w, k_norm_w, cos_t, sin_t)


def _attention_kernel(q_ref, qn_ref, k_ref, vt_ref, o_ref, s_ref, p_ref, l_ref):
    def scores(g, src_ref=q_ref):
        kv = g // ATT_GROUP
        q = src_ref[0, :, g * HEAD_DIM:(g + 1) * HEAD_DIM]
        k = k_ref[0, :, kv * HEAD_DIM:(kv + 1) * HEAD_DIM]
        s_ref[g % ATT_SLOTS] = lax.dot_general(k, q, (((1,), (1,)), ((), ())), preferred_element_type=F32)

    def softmax(g):
        s = s_ref[g % ATT_SLOTS]
        m = jnp.max(s, axis=0, keepdims=True)
        p = jnp.exp2(s - m)
        l_ref[g] = jnp.broadcast_to(jnp.sum(p, axis=0, keepdims=True), l_ref.shape[1:])
        p_ref[g % ATT_SLOTS] = p.astype(BF16)

    def values(g):
        kv = g // ATT_GROUP
        vt = vt_ref[0, kv * HEAD_DIM:(kv + 1) * HEAD_DIM, :]
        o_t = jnp.dot(vt, p_ref[g % ATT_SLOTS], preferred_element_type=F32)
        o_ref[0, :, g * HEAD_DIM:(g + 1) * HEAD_DIM] = (o_t * (1.0 / l_ref[g][0:1, :])).T.astype(BF16)

    n = ATT_HEADS
    lag = ATT_SOFTMAX_LAG

    @pl.when(pl.program_id(1) == 0)
    def _():
        for g in range(lag):
            scores(g)

    for g in range(n):
        if g + lag < n:
            scores(g + lag)
        else:
            scores(g + lag - n, qn_ref)
        softmax(g)
        if g >= ATT_VALUES_LAG:
            values(g - ATT_VALUES_LAG)
    for g in range(n - ATT_VALUES_LAG, n):
        values(g)


def _attention(aq, ak, av_t):
    b, t, _ = aq.shape
    tq = min(TQ_ATT, t)
    nt = t // tq
    return pl.pallas_call(
        _attention_kernel,
        grid=(b, nt),
        in_specs=[pl.BlockSpec((1, tq, C_AQ), lambda bi, i: (bi, i, 0)),
                  pl.BlockSpec((1, tq, C_AQ), lambda bi, i: (bi, jnp.minimum(i + 1, nt - 1), 0)),
                  pl.BlockSpec((1, t, C_AK), lambda bi, i: (bi, 0, 0)),
                  pl.BlockSpec((1, C_AV, t), lambda bi, i: (bi, 0, 0))],
        out_specs=pl.BlockSpec((1, tq, C_AQ), lambda bi, i: (bi, i, 0)),
        out_shape=jax.ShapeDtypeStruct((b, t, C_AQ), BF16),
        scratch_shapes=[pltpu.VMEM((ATT_SLOTS, t, tq), F32), pltpu.VMEM((ATT_SLOTS, t, tq), BF16),
                        pltpu.VMEM((ATT_HEADS, 8, tq), F32)],
        compiler_params=_params(("parallel", "arbitrary"), 48),
        name="attention",
    )(aq, aq, ak, av_t)


def _log_sigmoid(x):
    return jnp.minimum(x, 0.0) - jnp.log1p(jnp.exp(-jnp.abs(x)))


def _split3(a):
    a1 = a.astype(BF16)
    r1 = a - a1.astype(F32)
    a2 = r1.astype(BF16)
    a3 = (r1 - a2.astype(F32)).astype(BF16)
    return a1, a2, a3


def _lane_pick(a, idx):
    lane = lax.broadcasted_iota(I32, a.shape, 1)
    return jnp.sum(jnp.where(lane == idx, a, 0.0), axis=-1, keepdims=True)


def _mlstm_kernel(qf, kf, vf, gcf, grf, qb, kb, vb, gcb, grb, bc_ref, br_ref, hf_ref, hb_ref,
                  ct_ref, m_ref):
    @pl.when(pl.program_id(1) == 0)
    def _():
        ct_ref[...] = jnp.zeros_like(ct_ref)
        m_ref[...] = jnp.zeros_like(m_ref)

    L = ML_CHUNK
    row = lax.broadcasted_iota(I32, (L, L), 0)
    col = lax.broadcasted_iota(I32, (L, L), 1)
    lower = (col <= row)
    upper = (col >= row)
    lower_b = lower.astype(BF16)
    upper_b = upper.astype(BF16)

    gates = {}
    for bb in range(qf.shape[0]):
        for d, (gc_ref, gr_ref) in enumerate(((gcf, grf), (gcb, grb))):
            cum_cols = upper_b if d else lower_b
            cum_rows = lower_b if d else upper_b
            pre_c = gc_ref[bb] + bc_ref[...]
            pre_r = gr_ref[bb] + br_ref[...]
            b_cols = sum(jnp.dot(cum_cols, part, preferred_element_type=F32) for part in _split3(_log_sigmoid(pre_c)))
            b_rows = sum(jnp.dot(part, cum_rows, preferred_element_type=F32) for part in _split3(_log_sigmoid(pre_r)))
            gates[bb, d] = (pre_r, b_cols, b_rows)

    chains = [(bb, d, hh) for bb in range(qf.shape[0]) for d in range(2) for hh in range(ML_HEADS)]
    refs = ((qf, kf, vf, hf_ref), (qb, kb, vb, hb_ref))
    ones_block = jnp.ones((L, LANES), BF16)


    st = {}
    for (bb, d, hh) in chains:
        q_ref, kt_ref, v_ref, _ = refs[d]
        pre_r, b_cols, b_rows = gates[bb, d]
        ci = d * ML_HEADS + hh
        cf = 2 * ML_HEADS + d * ML_HEADS + hh
        sidx = (bb * 2 + d) * ML_HEADS + hh
        i_row = pre_r[ci:ci + 1, :]
        b_row = b_rows[cf:cf + 1, :]
        m = m_ref[sidx][0:1, 0:1]
        c_vis = jnp.where(upper if d else lower, i_row - b_row, -jnp.inf)
        shift = -jnp.maximum(m, jnp.max(c_vis, axis=-1, keepdims=True))
        dec = jnp.exp(c_vis + shift)
        a_inter = jnp.exp(m + shift)
        floor = jnp.exp(shift - _lane_pick(b_cols, cf))
        q = q_ref[bb, :, hh * ML_DQK:(hh + 1) * ML_DQK]
        kt = kt_ref[bb, hh * ML_DQK:(hh + 1) * ML_DQK, :]
        v_aug = jnp.concatenate([v_ref[bb, :, hh * ML_DV:(hh + 1) * ML_DV], ones_block], axis=1)
        qk = jnp.dot(q, kt, preferred_element_type=F32)
        ct = ct_ref[sidx]
        inter = jnp.dot(q, ct.astype(BF16), preferred_element_type=F32)
        b_end = _lane_pick(b_row, 0 if d else L - 1)
        g_row = b_end - b_row + i_row
        m_new = jnp.maximum(b_end + m, jnp.max(g_row, axis=-1, keepdims=True))
        decay = jnp.exp(b_end + m - m_new)
        w_row = jnp.exp(g_row - m_new)
        upd = jnp.dot((kt.astype(F32) * w_row).astype(BF16), v_aug, preferred_element_type=F32)
        st[bb, d, hh] = (sidx, dec, a_inter, floor, v_aug, qk, ct, inter, m_new, decay, upd)

    sv = {}
    for key in chains:
        sidx, dec, a_inter, floor, v_aug, qk, ct, inter, m_new, decay, upd = st[key]
        sv[key] = jnp.dot((qk * dec).astype(BF16), v_aug, preferred_element_type=F32)

    for key in chains:
        bb, d, hh = key
        sidx, dec, a_inter, floor, v_aug, qk, ct, inter, m_new, decay, upd = st[key]
        both = a_inter * inter + sv[key]
        scale = 1.0 / jnp.maximum(jnp.abs(both[:, ML_DV:]), floor)
        h_ref = refs[d][3]
        for blk in range(ML_DV // LANES):
            lo = hh * ML_DV + blk * LANES
            h_ref[bb, :, lo:lo + LANES] = both[:, blk * LANES:(blk + 1) * LANES] * scale
        ct_ref[sidx] = decay * ct + upd
        m_ref[sidx] = jnp.broadcast_to(m_new, (8, LANES))


def _mlstm(mq, mk_t, mv, mg, mg_t, bias_col, bias_row):
    b, t, _ = mq.shape
    L = ML_CHUNK
    nc = t // L
    sb = ML_SEQS if b % ML_SEQS == 0 else 1
    fw = lambda width: pl.BlockSpec((sb, L, width), lambda bi, c: (bi, c, 0))
    bw = lambda width: pl.BlockSpec((sb, L, width), lambda bi, c: (bi, nc - 1 - c, 0))
    fw_t = lambda rows: pl.BlockSpec((sb, rows, L), lambda bi, c: (bi, 0, c))
    bw_t = lambda rows: pl.BlockSpec((sb, rows, L), lambda bi, c: (bi, 0, nc - 1 - c))
    ns = sb * 2 * ML_HEADS
    return pl.pallas_call(
        _mlstm_kernel,
        grid=(b // sb, nc),
        in_specs=[fw(C_MQ), fw_t(C_MK), fw(C_MV), fw(LANES), fw_t(C_MG),
                  bw(C_MQ), bw_t(C_MK), bw(C_MV), bw(LANES), bw_t(C_MG),
                  _resident((1, LANES)), _resident((C_MG, L))],
        out_specs=[fw(C_MV), bw(C_MV)],
        out_shape=[jax.ShapeDtypeStruct((b, t, C_MV), F32)] * 2,
        scratch_shapes=[pltpu.VMEM((ns, ML_DQK, ML_DV + LANES), F32),
                        pltpu.VMEM((ns, 8, LANES), F32)],
        compiler_params=_params(("parallel", "arbitrary"), 48),
        name="mlstm",
    )(mq, mk_t, mv, mg, mg_t, mq, mk_t, mv, mg, mg_t, bias_col, bias_row)


def _moe_input(x1, n2_ref, mod_ref):
    return _rms(x1, n2_ref[...]) * (1.0 + mod_ref[0, 4:5, :]) + mod_ref[0, 3:4, :]


def _mix_kernel(x_ref, att_ref, hf_ref, hb_ref, mo_ref, mlw_ref, wo_ref, mod_ref, n2_ref,
                wrh_ref, wrl_ref, br_ref, *rest):
    x1_ref, lg_ref = rest[-2:]
    ml = hf_ref[0] + hb_ref[0]
    gate = jax.nn.sigmoid(mo_ref[0])
    mix = jnp.dot(att_ref[0], wo_ref[0:C_AQ, :], preferred_element_type=F32)
    for hh in range(ML_HEADS):
        sl = slice(hh * ML_DV, (hh + 1) * ML_DV)
        seg = (_rms(ml[:, sl], mlw_ref[:, sl]) * gate[:, sl]).astype(BF16)
        mix += jnp.dot(seg, wo_ref[C_AQ + hh * ML_DV:C_AQ + (hh + 1) * ML_DV, :], preferred_element_type=F32)
    x1 = x_ref[0] + mod_ref[0, 2:3, :] * mix
    x1_ref[0] = x1
    h2 = _moe_input(x1, n2_ref, mod_ref)
    h_hi = h2.astype(BF16)
    h_lo = (h2 - h_hi.astype(F32)).astype(BF16)
    nt = (((1,), (1,)), ((), ()))
    both = lax.dot_general(jnp.concatenate([wrh_ref[...], wrl_ref[...]], axis=0), h_hi, nt, preferred_element_type=F32)
    lg = both[:ROUTER_ROWS] + both[ROUTER_ROWS:] + lax.dot_general(wrh_ref[...], h_lo, nt, preferred_element_type=F32)
    lg_ref[0] = lg + br_ref[:, 0:1]


def _mix_and_router(x, att, h_fw, h_bw, mo, ml_norm_w, w_out, mod, norm2_w, wr_hi, wr_lo, b_router,
                    seq0, total_seqs, shared=None):
    b, t, d = x.shape
    tm = min(TM_MIX, t)
    row = lambda width: pl.BlockSpec((1, tm, width), lambda bi, i: (bi, i, 0))
    operands = [x, att, h_fw, h_bw, mo, ml_norm_w, w_out, mod, norm2_w, wr_hi, wr_lo, b_router]
    in_specs = [row(d), row(C_AQ), row(C_MV), row(C_MV), row(C_MO),
                _resident((1, C_MV)), _resident((d, d)),
                pl.BlockSpec((1, 6, d), lambda bi, i: (bi, 0, 0)),
                _resident((1, d)), _resident((ROUTER_ROWS, d)), _resident((ROUTER_ROWS, d)),
                _resident((ROUTER_ROWS, LANES))]
    aliases = {}
    if shared is not None:
        aliases = {len(operands): 0, len(operands) + 1: 1}
        operands += list(shared)
        in_specs += [pl.BlockSpec(memory_space=pl.ANY)] * 2
    return pl.pallas_call(
        _mix_kernel,
        grid=(b, t // tm),
        in_specs=in_specs,
        out_specs=[pl.BlockSpec((1, tm, d), lambda bi, i: (seq0 + bi, i, 0)),
                   pl.BlockSpec((1, ROUTER_ROWS, tm), lambda bi, i: (seq0 + bi, 0, i))],
        out_shape=[jax.ShapeDtypeStruct((total_seqs, t, d), F32),
                   jax.ShapeDtypeStruct((total_seqs, ROUTER_ROWS, t), F32)],
        input_output_aliases=aliases,
        compiler_params=_params(("parallel", "arbitrary"), 48),
        name="mix_router",
    )(*operands)


def _first_argmax(rows):
    best = rows[0]
    idx = jnp.zeros_like(best)
    for j in range(1, len(rows)):
        better = rows[j] > best
        best = jnp.where(better, rows[j], best)
        idx = jnp.where(better, float(j), idx)
    return best, idx


def _softmax_rows(rows):
    mx = functools.reduce(jnp.maximum, rows)
    ex = [jnp.exp(r - mx) for r in rows]
    tot = functools.reduce(lambda a, c: a + c, ex)
    return [e / tot for e in ex]


def _route_kernel(lg_ref, idx_ref, wt_ref, cnt_ref, run_ref):
    @pl.when((pl.program_id(0) == 0) & (pl.program_id(1) == 0))
    def _():
        run_ref[...] = jnp.zeros_like(run_ref)

    tl = lg_ref.shape[2]
    lg = lg_ref[0]
    p_grp = _softmax_rows([lg[g:g + 1, :] for g in range(N_GROUPS)])
    p_g, g_idx = _first_argmax(p_grp)
    el = []
    for j in range(EXPERTS_PER_GROUP):
        sel = lg[N_GROUPS + (N_GROUPS - 1) * EXPERTS_PER_GROUP + j:N_GROUPS + (N_GROUPS - 1) * EXPERTS_PER_GROUP + j + 1, :]
        for g in range(N_GROUPS - 2, -1, -1):
            r = N_GROUPS + g * EXPERTS_PER_GROUP + j
            sel = jnp.where(g_idx == float(g), lg[r:r + 1, :], sel)
        el.append(sel)
    pe = _softmax_rows(el)
    w1, i1 = _first_argmax(pe)
    rest = [jnp.where(i1 == float(j), -1.0, pe[j]) for j in range(EXPERTS_PER_GROUP)]
    w2, i2 = _first_argmax(rest)
    tot = w1 + w2
    wt1 = w1 / tot * p_g
    wt2 = w2 / tot * p_g
    lo = jnp.minimum(i1, i2)
    hi = jnp.maximum(i1, i2)
    pair = jnp.where(hi == 3.0, lo, jnp.where(lo == 1.0, 3.0, jnp.where(hi == 1.0, 4.0, 5.0)))
    slot_a = functools.reduce(lambda acc, p: jnp.where(pair == float(p), float(PAIR_SLOT_A[p]), acc),
                              range(1, N_PAIRS), jnp.full_like(pair, float(PAIR_SLOT_A[0])))
    first_in_a = slot_a == i1
    wa = jnp.where(first_in_a, wt1, wt2)
    wb = jnp.where(first_in_a, wt2, wt1)
    bucket = (g_idx * float(N_PAIRS) + pair).astype(I32)

    brow = lax.broadcasted_iota(I32, (BUCKET_ROWS, tl), 0)
    hit = brow == bucket
    onehot = hit.astype(F32)
    before = lax.broadcasted_iota(I32, (tl, tl), 0) <= lax.broadcasted_iota(I32, (tl, tl), 1)
    incl = jnp.dot(onehot.astype(BF16), before.astype(BF16), preferred_element_type=F32)
    base = run_ref[:, 0:1] + incl - onehot
    rank = jnp.sum(jnp.where(hit, base, 0.0), axis=0, keepdims=True).astype(I32)
    total = run_ref[...] + jnp.sum(onehot, axis=-1, keepdims=True)
    run_ref[...] = total
    cnt_ref[...] = total.astype(I32)

    idx_ref[0] = jnp.concatenate([bucket, rank, jnp.zeros((6, tl), I32)], axis=0)
    wt_ref[0] = jnp.concatenate([wa, wb, jnp.zeros((6, tl), F32)], axis=0)


def _route(logits_t):
    b, _, t = logits_t.shape
    tl = min(TL_ROUTE, t)
    blk = lambda rows: pl.BlockSpec((1, rows, tl), lambda bi, i: (bi, 0, i))
    return pl.pallas_call(
        _route_kernel,
        grid=(b, t // tl),
        in_specs=[blk(ROUTER_ROWS)],
        out_specs=[blk(8), blk(8), pl.BlockSpec((BUCKET_ROWS, LANES), lambda bi, i: (0, 0))],
        out_shape=[jax.ShapeDtypeStruct((b, 8, t), I32), jax.ShapeDtypeStruct((b, 8, t), F32),
                   jax.ShapeDtypeStruct((BUCKET_ROWS, LANES), I32)],
        scratch_shapes=[pltpu.VMEM((BUCKET_ROWS, LANES), F32)],
        compiler_params=_params(("arbitrary", "arbitrary"), 32),
        name="route",
    )(logits_t)


def _row_copy(src_ref, src_row, dst_ref, dst_row, sem):
    return pltpu.make_async_copy(src_ref.at[pl.ds(src_row, 1), :], dst_ref.at[pl.ds(dst_row, 1), :], sem)


def _dispatch_kernel(padlo_ref, padhi_ref, dest_ref, x1_ref, wtail_ref, n2_ref, mod_ref, xs_ref, h_ref, zero_ref, sem):
    td, d = x1_ref.shape
    i = pl.program_id(0)
    slot = i % 2
    rows = h_ref.at[slot]
    for c0 in range(0, td, ROW_CHUNK):
        sl = slice(c0, c0 + ROW_CHUNK)
        rows[sl, 0:d] = _moe_input(x1_ref[sl, :], n2_ref, mod_ref)
        rows[sl, d:] = wtail_ref[sl, :]
        for t in range(c0, c0 + ROW_CHUNK):
            _row_copy(rows, t, xs_ref, dest_ref[0, 0, t], sem.at[slot]).start()

    def drain(which):
        pltpu.make_async_copy(h_ref.at[which], xs_ref.at[pl.ds(0, td), :], sem.at[which]).wait()

    @pl.when(i > 0)
    def _():
        drain(1 - slot)

    @pl.when(i == pl.num_programs(0) - 1)
    def _():
        drain(slot)
        zero_ref[...] = jnp.zeros_like(zero_ref)
        zsem = sem.at[0]

        sub = zero_ref.shape[0]

        def one_row(r):
            return _row_copy(zero_ref, 0, xs_ref, r, zsem)

        def aligned_rows(r):
            return pltpu.make_async_copy(zero_ref, xs_ref.at[pl.ds(pl.multiple_of(r, sub), sub), :], zsem)

        def spans(e):
            lo, hi = padlo_ref[e], padhi_ref[e]
            mid = jnp.minimum(hi, (lo + sub - 1) // sub * sub)
            return lo, mid, (hi - mid) // sub

        for wait in (False, True):
            for e in range(N_BUCKETS):
                lo, mid, nblk = spans(e)

                def head(r, carry):
                    one_row(r).wait() if wait else one_row(r).start()
                    return carry

                def body(c, carry):
                    cp = aligned_rows(mid + c * sub)
                    cp.wait() if wait else cp.start()
                    return carry

                lax.fori_loop(lo, mid, head, 0)
                lax.fori_loop(0, nblk, body, 0)


def _dispatch(x1, dest, w_tail, norm2_w, mod, pad_lo, pad_hi, m_pad):
    n, d = x1.shape
    b, _, t = dest.shape
    td = min(TD_DISPATCH, t)
    per = t // td
    width = d + LANES
    return pl.pallas_call(
        _dispatch_kernel,
        grid_spec=pltpu.PrefetchScalarGridSpec(
            num_scalar_prefetch=2,
            grid=(n // td,),
            in_specs=[pl.BlockSpec((1, 1, td), lambda i, *_: (i // per, 0, i % per), memory_space=pltpu.SMEM),
                      pl.BlockSpec((td, d), lambda i, *_: (i, 0)),
                      pl.BlockSpec((td, LANES), lambda i, *_: (i, 0)),
                      pl.BlockSpec((1, d), lambda i, *_: (0, 0)),
                      pl.BlockSpec((1, 6, d), lambda i, *_: (i // per, 0, 0))],
            out_specs=pl.BlockSpec(memory_space=pl.ANY),
            scratch_shapes=[pltpu.VMEM((2, td, width), F32), pltpu.VMEM((8, width), F32),
                            pltpu.SemaphoreType.DMA((2,))]),
        out_shape=jax.ShapeDtypeStruct((m_pad, width), F32),
        compiler_params=_params(("arbitrary",), 32),
        name="dispatch",
    )(pad_lo, pad_hi, dest, x1, w_tail, norm2_w, mod)


def _moe_kernel(ea_ref, eb_ref, nused_ref, x_ref, wga_ref, wua_ref, wda_ref, wgb_ref, wub_ref, wdb_ref, o_ref):
    r = pl.program_id(0)
    d = o_ref.shape[1]

    @pl.when(r < nused_ref[0])
    def _():
        xb = x_ref[:, 0:d].astype(BF16)

        def expert(wg_ref, wu_ref, wd_ref):
            g = jnp.dot(xb, wg_ref[0], preferred_element_type=F32)
            u = jnp.dot(xb, wu_ref[0], preferred_element_type=F32)
            h = (g * jax.nn.sigmoid(g) * u).astype(BF16)
            return jnp.dot(h, wd_ref[0], preferred_element_type=F32)

        tail = x_ref[:, d:]
        o_ref[...] = (expert(wga_ref, wua_ref, wda_ref) * _lane_pick(tail, 0)
                      + expert(wgb_ref, wub_ref, wdb_ref) * _lane_pick(tail, 1))

    @pl.when(r >= nused_ref[0])
    def _():
        o_ref[...] = jnp.zeros_like(o_ref)


def _moe_experts(xs, blk_ea, blk_eb, nused, w_gate, w_up, w_down):
    m_pad, width = xs.shape
    d = width - LANES
    f = w_gate.shape[2]
    tm = TM_MOE
    last = lambda r, nu: jnp.minimum(r, nu[0] - 1)
    slot = lambda shape, pick, bufs: pl.BlockSpec(shape, lambda r, ea, eb, nu: (pick(ea, eb)[last(r, nu)], 0, 0),
                                                  pipeline_mode=pl.Buffered(bufs))
    slot_a = lambda shape: slot(shape, lambda ea, eb: ea, 2)
    slot_b = lambda shape: slot(shape, lambda ea, eb: eb, 1)
    return pl.pallas_call(
        _moe_kernel,
        grid_spec=pltpu.PrefetchScalarGridSpec(
            num_scalar_prefetch=3,
            grid=(m_pad // tm,),
            in_specs=[pl.BlockSpec((tm, width), lambda r, ea, eb, nu: (last(r, nu), 0)),
                      slot_a((1, d, f)), slot_a((1, d, f)), slot_a((1, f, d)),
                      slot_b((1, d, f)), slot_b((1, d, f)), slot_b((1, f, d))],
            out_specs=pl.BlockSpec((tm, d), lambda r, ea, eb, nu: (r, 0))),
        out_shape=jax.ShapeDtypeStruct((m_pad, d), F32),
        compiler_params=_params(("arbitrary",), 58),
        name="moe_experts",
    )(blk_ea, blk_eb, nused, xs, w_gate, w_up, w_down, w_gate, w_up, w_down)


def _combine_kernel(dest_ref, dnext_ref, x1_ref, mod_ref, o_ref, y_ref, g_ref, sem):
    tc = x1_ref.shape[0]
    i = pl.program_id(0)
    slot = i % 2

    def drain(which):
        pltpu.make_async_copy(o_ref.at[pl.ds(0, tc), :], g_ref.at[which], sem.at[which]).wait()

    @pl.when(i == 0)
    def _():
        def start(t, carry):
            _row_copy(o_ref, dest_ref[0, 0, t], g_ref.at[slot], t, sem.at[slot]).start()
            return carry

        lax.fori_loop(0, tc, start, 0, unroll=DMA_UNROLL)

    drain(slot)
    rows = g_ref.at[slot]
    for c0 in range(0, tc, ROW_CHUNK):
        sl = slice(c0, c0 + ROW_CHUNK)
        y_ref[sl, :] = x1_ref[sl, :] + mod_ref[0, 5:6, :] * rows[sl, :]
        for t in range(c0, c0 + ROW_CHUNK):
            _row_copy(o_ref, dnext_ref[0, 0, t], g_ref.at[1 - slot], t, sem.at[1 - slot]).start()

    @pl.when(i == pl.num_programs(0) - 1)
    def _():
        drain(1 - slot)


def _combine(x1, dest, mod, o_rows, seq0, nseq):
    _, d = x1.shape
    _, _, t = dest.shape
    tc = min(TC_COMBINE, t)
    per = t // tc
    ntiles = nseq * per
    first = seq0 * per
    tile = lambda g: (g // per, 0, g % per)
    return pl.pallas_call(
        _combine_kernel,
        grid=(ntiles,),
        in_specs=[pl.BlockSpec((1, 1, tc), lambda i: tile(first + i), memory_space=pltpu.SMEM),
                  pl.BlockSpec((1, 1, tc), lambda i: tile(first + jnp.minimum(i + 1, ntiles - 1)),
                               memory_space=pltpu.SMEM),
                  pl.BlockSpec((tc, d), lambda i: (first + i, 0)),
                  pl.BlockSpec((1, 6, d), lambda i: ((first + i) // per, 0, 0)),
                  pl.BlockSpec(memory_space=pl.ANY)],
        out_specs=pl.BlockSpec((tc, d), lambda i: (i, 0)),
        scratch_shapes=[pltpu.VMEM((2, tc, d), F32), pltpu.SemaphoreType.DMA((2,))],
        out_shape=jax.ShapeDtypeStruct((nseq * t, d), F32),
        compiler_params=_params(("arbitrary",), 32),
        name="combine",
    )(dest, dest, x1, mod, o_rows)


def _rope_tables(t):
    rows = t // GRID_W
    row = jnp.repeat(jnp.arange(rows, dtype=F32), GRID_W)
    col = jnp.tile(jnp.arange(GRID_W, dtype=F32), rows)
    freqs = ROPE_THETA ** (-jnp.arange(ROPE_PAIRS, dtype=F32) / ROPE_PAIRS)
    ar = row[:, None] * freqs
    ac = col[:, None] * freqs
    cos_t = jnp.concatenate([jnp.cos(ar), jnp.cos(ar), jnp.cos(ac), jnp.cos(ac)], axis=-1)
    sin_t = jnp.concatenate([-jnp.sin(ar), jnp.sin(ar), -jnp.sin(ac), jnp.sin(ac)], axis=-1)
    return cos_t, sin_t


def _token_mixing(x, mod, p, seq0, total_seqs, shared):
    b, t, d = x.shape
    cos_t, sin_t = _rope_tables(t)
    aq, ak, av_t, mq, mk_t, mv, mo, mg = _in_projection(
        x, mod, p["norm1_w"], p["w_main"], p["w_v_t"], p["w_gates"], p["q_norm_w"], p["k_norm_w"], cos_t, sin_t)
    att = _attention(aq, ak, av_t)
    mg_t = jnp.swapaxes(mg[:, :, :C_MG], 1, 2)
    h_fw, h_bw = _mlstm(mq, mk_t, mv, mg, mg_t, p["gate_bias_col"], p["gate_bias_row"])
    return _mix_and_router(x, att, h_fw, h_bw, mo, p["ml_norm_w"], p["w_out"], mod,
                           p["norm2_w"], p["wr_hi"], p["wr_lo"], p["b_router"], seq0, total_seqs, shared)


def _channel_mixing(x1, logits_t, mod, p, groups):
    b, t, d = x1.shape
    n = b * t
    idx, wts, counts = _route(logits_t)

    tm = TM_MOE
    counts = counts[:N_BUCKETS, 0]
    padded = (counts + tm - 1) // tm * tm
    pend = jnp.cumsum(padded)
    pstart = (pend - padded).astype(I32)
    nb = (n + tm - 1) // tm + N_BUCKETS
    m_pad = nb * tm
    block_row0 = jnp.arange(nb, dtype=I32) * tm
    blk_bucket = jnp.minimum(jnp.sum(pend[None, :] <= block_row0[:, None], axis=1), N_BUCKETS - 1)
    group_base = np.repeat(np.arange(N_GROUPS) * EXPERTS_PER_GROUP, N_PAIRS)
    expert_a = jnp.asarray(group_base + np.tile(PAIR_SLOT_A, N_GROUPS), I32)
    expert_b = jnp.asarray(group_base + np.tile(PAIR_SLOT_B, N_GROUPS), I32)
    blk_onehot = blk_bucket[:, None] == jnp.arange(N_BUCKETS)
    blk_ea = jnp.sum(jnp.where(blk_onehot, expert_a, 0), axis=1).astype(I32)
    blk_eb = jnp.sum(jnp.where(blk_onehot, expert_b, 0), axis=1).astype(I32)
    nused = (pend[-1:] // tm).astype(I32)
    onehot = idx[:, 0:1, :, None] == jnp.arange(N_BUCKETS, dtype=I32)
    dest = jnp.sum(jnp.where(onehot, pstart, 0), axis=-1) + idx[:, 1:2, :]

    pad_lo = (pstart + counts).astype(I32)
    pad_hi = jnp.concatenate([pstart[1:], jnp.full((1,), m_pad, I32)])
    x1 = x1.reshape(n, d)
    w_tail = jnp.zeros((n, LANES), F32).at[:, :2].set(jnp.swapaxes(wts[:, :2, :], 1, 2).reshape(n, 2))
    xs = _dispatch(x1, dest, w_tail, p["norm2_w"], mod, pad_lo, pad_hi, m_pad)
    o_rows = _moe_experts(xs, blk_ea, blk_eb, nused, p["w_gate"], p["w_up"], p["w_down"])
    return [_combine(x1, dest, mod, o_rows, seq0, nseq).reshape(nseq, t, d) for seq0, nseq in groups]


def kernel(x_prompt, x_sample, c_prompt, c_sample, norm1_w, norm2_w, w_ada, b_ada, w_in, q_norm_w, k_norm_w, b_igate, b_fgate, ml_norm_w, w_out, w_gr, b_gr, w_er, b_er, w_gate, w_up, w_down):
    assert x_prompt.shape[1:] == x_sample.shape[1:], "the request groups share one token buffer per sequence length"
    depth = norm1_w.shape[0]
    d = x_prompt.shape[-1]
    bp = x_prompt.shape[0]
    bs = x_sample.shape[0]
    rows = -(-(bp + bs) // 8) * 8
    y_prompt, y_sample = x_prompt, x_sample
    for l in range(depth):
        c_pad = jnp.zeros((rows, d), F32).at[:bp].set(c_prompt).at[bp:bp + bs].set(c_sample)
        mod = _ada_modulation(c_pad, w_ada[l], b_ada[l]).reshape(rows, 6, d)
        gate_bias = jnp.concatenate([b_igate[l].reshape(-1), b_fgate[l].reshape(-1)])
        w_router = jnp.concatenate([w_gr[l], w_er[l]], axis=1).T
        w_router = jnp.zeros((ROUTER_ROWS, d), F32).at[:N_GROUPS + N_EXPERTS].set(w_router)
        wr_hi = w_router.astype(BF16)
        b_router = jnp.zeros((ROUTER_ROWS,), F32).at[:N_GROUPS + N_EXPERTS].set(jnp.concatenate([b_gr[l], b_er[l]]))
        p = {
            "norm1_w": norm1_w[l].reshape(1, d),
            "norm2_w": norm2_w[l].reshape(1, d),
            "w_main": w_in[l][:, :C_MAIN].astype(BF16),
            "w_v_t": jnp.concatenate([w_in[l][:, C_AQ + C_AK:C_AQ + C_AK + C_AV],
                                      w_in[l][:, C_AQ + C_AK + C_AV + C_MQ:C_AQ + C_AK + C_AV + C_MQ + C_MK]],
                                     axis=1).T.astype(BF16),
            "w_gates": jnp.zeros((d, LANES), BF16).at[:, :C_MG].set(w_in[l][:, C_MAIN:].astype(BF16)),
            "q_norm_w": q_norm_w[l].reshape(1, HEAD_DIM),
            "k_norm_w": k_norm_w[l].reshape(1, HEAD_DIM),
            "gate_bias_col": jnp.zeros((1, LANES), F32).at[0, :C_MG].set(gate_bias),
            "gate_bias_row": jnp.broadcast_to(gate_bias[:, None], (C_MG, ML_CHUNK)),
            "ml_norm_w": ml_norm_w[l].reshape(1, C_MV),
            "w_out": w_out[l].astype(BF16),
            "wr_hi": wr_hi,
            "wr_lo": (w_router - wr_hi.astype(F32)).astype(BF16),
            "b_router": jnp.broadcast_to(b_router[:, None], (ROUTER_ROWS, LANES)),
            "w_gate": w_gate[l].astype(BF16),
            "w_up": w_up[l].astype(BF16),
            "w_down": w_down[l].astype(BF16),
        }
        shared = _token_mixing(y_prompt, mod[:bp], p, 0, bp + bs, None)
        x1, logits_t = _token_mixing(y_sample, mod[bp:bp + bs], p, bp, bp + bs, shared)
        y_prompt, y_sample = _channel_mixing(x1, logits_t, mod, p, ((0, bp), (bp, bs)))
    return (y_prompt, y_sample)
```

```python
import functools

import jax
import jax.numpy as jnp
import numpy as np
from jax import lax
from jax.experimental import pallas as pl
from jax.experimental.pallas import tpu as pltpu

F32 = jnp.float32
BF16 = jnp.bfloat16
I32 = jnp.int32

GRID_W = 64
HEAD_DIM = 128
ATT_HEADS = 8
ATT_KV_HEADS = 2
ATT_GROUP = ATT_HEADS // ATT_KV_HEADS
ROPE_THETA = 10000.0
ROPE_PAIRS = HEAD_DIM // 4
ML_HEADS = 4
ML_DV = 256
ML_DQK = 128
ML_CHUNK = 128
N_GROUPS = 4
EXPERTS_PER_GROUP = 4
N_EXPERTS = N_GROUPS * EXPERTS_PER_GROUP
EPS = 1e-6
Q_SCALE = HEAD_DIM ** -0.5 * float(np.log2(np.e))

C_AQ = ATT_HEADS * HEAD_DIM
C_AK = ATT_KV_HEADS * HEAD_DIM
C_AV = ATT_KV_HEADS * HEAD_DIM
C_MQ = ML_HEADS * ML_DQK
C_MK = ML_HEADS * ML_DQK
C_MV = ML_HEADS * ML_DV
C_MO = ML_HEADS * ML_DV
C_MG = 4 * ML_HEADS
C_MAIN = C_AQ + C_AK + C_AV + C_MQ + C_MK + C_MV + C_MO

N_PAIRS = 6
PAIR_SLOT_A = (0, 1, 2, 2, 0, 0)
PAIR_SLOT_B = (3, 3, 3, 1, 1, 2)
N_BUCKETS = N_GROUPS * N_PAIRS
BUCKET_ROWS = 32

LANES = 128
ROUTER_ROWS = 32
MIB = 1024 * 1024

TM_PROJ = 256
TQ_ATT = 256
TM_MIX = 512
TL_ROUTE = 512
TD_DISPATCH = 256
TM_MOE = 256
TC_COMBINE = 256
ML_SEQS = 4
ATT_SOFTMAX_LAG = 2
ATT_VALUES_LAG = 1
ATT_SLOTS = 4
ROW_CHUNK = 32
DMA_UNROLL = 8


def _params(semantics, vmem_mib):
    return pltpu.CompilerParams(dimension_semantics=semantics, vmem_limit_bytes=vmem_mib * MIB)


def _resident(shape):
    nd = len(shape)
    return pl.BlockSpec(shape, lambda *_: (0,) * nd, pipeline_mode=pl.Buffered(1))


def _ada_kernel(c_ref, w_ref, b_ref, o_ref):
    c = c_ref[...]
    s = (c * jax.nn.sigmoid(c)).astype(BF16)
    o_ref[...] = jnp.dot(s, w_ref[...].astype(BF16), preferred_element_type=F32) + b_ref[...]


def _ada_modulation(c_pad, w_ada, b_ada):
    rows, d = c_pad.shape
    n = w_ada.shape[1]
    tn = 1024
    return pl.pallas_call(
        _ada_kernel,
        grid=(n // tn,),
        in_specs=[pl.BlockSpec((rows, d), lambda j: (0, 0)),
                  pl.BlockSpec((d, tn), lambda j: (0, j)),
                  pl.BlockSpec((1, tn), lambda j: (0, j))],
        out_specs=pl.BlockSpec((rows, tn), lambda j: (0, j)),
        out_shape=jax.ShapeDtypeStruct((rows, n), F32),
        compiler_params=_params(("arbitrary",), 40),
        name="ada_modulation",
    )(c_pad, w_ada, b_ada.reshape(1, n))


def _rms(x, w):
    return x * lax.rsqrt(jnp.mean(x * x, axis=-1, keepdims=True) + EPS) * w


def _inproj_kernel(x_ref, mod_ref, n1_ref, w_ref, wvt_ref, wg_ref, qn_ref, kn_ref, cos_ref, sin_ref,
                   aq_ref, ak_ref, avt_ref, mq_ref, mkt_ref, mv_ref, mo_ref, mg_ref):
    x = x_ref[0]
    h = _rms(x, n1_ref[...]) * (1.0 + mod_ref[0, 1:2, :]) + mod_ref[0, 0:1, :]
    hb = h.astype(BF16)

    def proj(c0, width):
        return jnp.dot(hb, w_ref[:, c0:c0 + width], preferred_element_type=F32)

    cos = cos_ref[...]
    sin = sin_ref[...]
    lane = lax.broadcasted_iota(I32, (1, HEAD_DIM), 1)
    first = (lane % (2 * ROPE_PAIRS)) < ROPE_PAIRS

    def norm_rope(p, w):
        pn = _rms(p, w)
        partner = jnp.where(first, pltpu.roll(pn, HEAD_DIM - ROPE_PAIRS, 1), pltpu.roll(pn, ROPE_PAIRS, 1))
        return pn * cos + partner * sin

    c0 = 0
    for half in range(2):
        p = proj(c0, C_AQ // 2)
        for hh in range(ATT_HEADS // 2):
            col = half * (C_AQ // 2) + hh * HEAD_DIM
            qh = norm_rope(p[:, hh * HEAD_DIM:(hh + 1) * HEAD_DIM], qn_ref[...])
            aq_ref[0, :, col:col + HEAD_DIM] = (qh * Q_SCALE).astype(BF16)
        c0 += C_AQ // 2
    p = proj(c0, C_AK)
    for hh in range(ATT_KV_HEADS):
        ak_ref[0, :, hh * HEAD_DIM:(hh + 1) * HEAD_DIM] = norm_rope(p[:, hh * HEAD_DIM:(hh + 1) * HEAD_DIM], kn_ref[...]).astype(BF16)
    nt = (((1,), (1,)), ((), ()))
    avt_ref[0] = lax.dot_general(wvt_ref[0:C_AV, :], hb, nt, preferred_element_type=F32).astype(BF16)
    mkt_ref[0] = lax.dot_general(wvt_ref[C_AV:, :], hb, nt, preferred_element_type=F32).astype(BF16)
    c0 += C_AK + C_AV
    mq_ref[0] = (proj(c0, C_MQ) * (ML_DQK ** -0.5)).astype(BF16)
    c0 += C_MQ + C_MK
    for half in range(2):
        mv_ref[0, :, half * 512:(half + 1) * 512] = proj(c0, 512).astype(BF16)
        c0 += 512
    for half in range(2):
        mo_ref[0, :, half * 512:(half + 1) * 512] = proj(c0, 512)
        c0 += 512
    mg_ref[0] = jnp.dot(hb, wg_ref[...], preferred_element_type=F32)


def _in_projection(x, mod, norm1_w, w_main, w_v_t, w_gates, q_norm_w, k_norm_w, cos_t, sin_t):
    b, t, d = x.shape
    tm = min(TM_PROJ, t)
    row = lambda width: pl.BlockSpec((1, tm, width), lambda bi, i: (bi, i, 0))
    outs = ((C_AQ, False), (C_AK, False), (C_AV, True), (C_MQ, False), (C_MK, True), (C_MV, False),
            (C_MO, False), (LANES, False))
    dtypes = (BF16, BF16, BF16, BF16, BF16, BF16, F32, F32)
    col = lambda width: pl.BlockSpec((1, width, tm), lambda bi, i: (bi, 0, i))
    return pl.pallas_call(
        _inproj_kernel,
        grid=(b, t // tm),
        in_specs=[row(d),
                  pl.BlockSpec((1, 6, d), lambda bi, i: (bi, 0, 0)),
                  _resident((1, d)),
                  _resident((d, C_MAIN)),
                  _resident((C_AV + C_MK, d)),
                  _resident((d, LANES)),
                  _resident((1, HEAD_DIM)),
                  _resident((1, HEAD_DIM)),
                  pl.BlockSpec((tm, HEAD_DIM), lambda bi, i: (i, 0)),
                  pl.BlockSpec((tm, HEAD_DIM), lambda bi, i: (i, 0))],
        out_specs=[col(w) if tr else row(w) for w, tr in outs],
        out_shape=[jax.ShapeDtypeStruct((b, w, t) if tr else (b, t, w), dt) for (w, tr), dt in zip(outs, dtypes)],
        compiler_params=_params(("parallel", "arbitrary"), 48),
        name="in_projection",
    )(x, mod, norm1_w, w_main, w_v_t, w_gates, q_norm_w, k_norm_w, cos_t, sin_t)


def _attention_kernel(q_ref, qn_ref, k_ref, vt_ref, o_ref, s_ref, p_ref, l_ref):
    def scores(g, src_ref=q_ref):
        kv = g // ATT_GROUP
        q = src_ref[0, :, g * HEAD_DIM:(g + 1) * HEAD_DIM]
        k = k_ref[0, :, kv * HEAD_DIM:(kv + 1) * HEAD_DIM]
        s_ref[g % ATT_SLOTS] = lax.dot_general(k, q, (((1,), (1,)), ((), ())), preferred_element_type=F32)

    def softmax(g):
        s = s_ref[g % ATT_SLOTS]
        m = jnp.max(s, axis=0, keepdims=True)
        p = jnp.exp2(s - m)
        l_ref[g] = jnp.broadcast_to(jnp.sum(p, axis=0, keepdims=True), l_ref.shape[1:])
        p_ref[g % ATT_SLOTS] = p.astype(BF16)

    def values(g):
        kv = g // ATT_GROUP
        vt = vt_ref[0, kv * HEAD_DIM:(kv + 1) * HEAD_DIM, :]
        o_t = jnp.dot(vt, p_ref[g % ATT_SLOTS], preferred_element_type=F32)
        o_ref[0, :, g * HEAD_DIM:(g + 1) * HEAD_DIM] = (o_t * (1.0 / l_ref[g][0:1, :])).T.astype(BF16)

    n = ATT_HEADS
    lag = ATT_SOFTMAX_LAG

    @pl.when(pl.program_id(1) == 0)
    def _():
        for g in range(lag):
            scores(g)

    for g in range(n):
        if g + lag < n:
            scores(g + lag)
        else:
            scores(g + lag - n, qn_ref)
        softmax(g)
        if g >= ATT_VALUES_LAG:
            values(g - ATT_VALUES_LAG)
    for g in range(n - ATT_VALUES_LAG, n):
        values(g)


def _attention(aq, ak, av_t):
    b, t, _ = aq.shape
    tq = min(TQ_ATT, t)
    nt = t // tq
    return pl.pallas_call(
        _attention_kernel,
        grid=(b, nt),
        in_specs=[pl.BlockSpec((1, tq, C_AQ), lambda bi, i: (bi, i, 0)),
                  pl.BlockSpec((1, tq, C_AQ), lambda bi, i: (bi, jnp.minimum(i + 1, nt - 1), 0)),
                  pl.BlockSpec((1, t, C_AK), lambda bi, i: (bi, 0, 0)),
                  pl.BlockSpec((1, C_AV, t), lambda bi, i: (bi, 0, 0))],
        out_specs=pl.BlockSpec((1, tq, C_AQ), lambda bi, i: (bi, i, 0)),
        out_shape=jax.ShapeDtypeStruct((b, t, C_AQ), BF16),
        scratch_shapes=[pltpu.VMEM((ATT_SLOTS, t, tq), F32), pltpu.VMEM((ATT_SLOTS, t, tq), BF16),
                        pltpu.VMEM((ATT_HEADS, 8, tq), F32)],
        compiler_params=_params(("parallel", "arbitrary"), 48),
        name="attention",
    )(aq, aq, ak, av_t)


def _log_sigmoid(x):
    return jnp.minimum(x, 0.0) - jnp.log1p(jnp.exp(-jnp.abs(x)))


def _split3(a):
    a1 = a.astype(BF16)
    r1 = a - a1.astype(F32)
    a2 = r1.astype(BF16)
    a3 = (r1 - a2.astype(F32)).astype(BF16)
    return a1, a2, a3


def _lane_pick(a, idx):
    lane = lax.broadcasted_iota(I32, a.shape, 1)
    return jnp.sum(jnp.where(lane == idx, a, 0.0), axis=-1, keepdims=True)


def _mlstm_kernel(qf, kf, vf, gcf, grf, qb, kb, vb, gcb, grb, bc_ref, br_ref, hf_ref, hb_ref,
                  ct_ref, m_ref):
    @pl.when(pl.program_id(1) == 0)
    def _():
        ct_ref[...] = jnp.zeros_like(ct_ref)
        m_ref[...] = jnp.zeros_like(m_ref)

    L = ML_CHUNK
    row = lax.broadcasted_iota(I32, (L, L), 0)
    col = lax.broadcasted_iota(I32, (L, L), 1)
    lower = (col <= row)
    upper = (col >= row)
    lower_b = lower.astype(BF16)
    upper_b = upper.astype(BF16)

    gates = {}
    for bb in range(qf.shape[0]):
        for d, (gc_ref, gr_ref) in enumerate(((gcf, grf), (gcb, grb))):
            cum_cols = upper_b if d else lower_b
            cum_rows = lower_b if d else upper_b
            pre_c = gc_ref[bb] + bc_ref[...]
            pre_r = gr_ref[bb] + br_ref[...]
            b_cols = sum(jnp.dot(cum_cols, part, preferred_element_type=F32) for part in _split3(_log_sigmoid(pre_c)))
            b_rows = sum(jnp.dot(part, cum_rows, preferred_element_type=F32) for part in _split3(_log_sigmoid(pre_r)))
            gates[bb, d] = (pre_r, b_cols, b_rows)

    chains = [(bb, d, hh) for bb in range(qf.shape[0]) for d in range(2) for hh in range(ML_HEADS)]
    refs = ((qf, kf, vf, hf_ref), (qb, kb, vb, hb_ref))
    ones_block = jnp.ones((L, LANES), BF16)


    st = {}
    for (bb, d, hh) in chains:
        q_ref, kt_ref, v_ref, _ = refs[d]
        pre_r, b_cols, b_rows = gates[bb, d]
        ci = d * ML_HEADS + hh
        cf = 2 * ML_HEADS + d * ML_HEADS + hh
        sidx = (bb * 2 + d) * ML_HEADS + hh
        i_row = pre_r[ci:ci + 1, :]
        b_row = b_rows[cf:cf + 1, :]
        m = m_ref[sidx][0:1, 0:1]
        c_vis = jnp.where(upper if d else lower, i_row - b_row, -jnp.inf)
        shift = -jnp.maximum(m, jnp.max(c_vis, axis=-1, keepdims=True))
        dec = jnp.exp(c_vis + shift)
        a_inter = jnp.exp(m + shift)
        floor = jnp.exp(shift - _lane_pick(b_cols, cf))
        q = q_ref[bb, :, hh * ML_DQK:(hh + 1) * ML_DQK]
        kt = kt_ref[bb, hh * ML_DQK:(hh + 1) * ML_DQK, :]
        v_aug = jnp.concatenate([v_ref[bb, :, hh * ML_DV:(hh + 1) * ML_DV], ones_block], axis=1)
        qk = jnp.dot(q, kt, preferred_element_type=F32)
        ct = ct_ref[sidx]
        inter = jnp.dot(q, ct.astype(BF16), preferred_element_type=F32)
        b_end = _lane_pick(b_row, 0 if d else L - 1)
        g_row = b_end - b_row + i_row
        m_new = jnp.maximum(b_end + m, jnp.max(g_row, axis=-1, keepdims=True))
        decay = jnp.exp(b_end + m - m_new)
        w_row = jnp.exp(g_row - m_new)
        upd = jnp.dot((kt.astype(F32) * w_row).astype(BF16), v_aug, preferred_element_type=F32)
        st[bb, d, hh] = (sidx, dec, a_inter, floor, v_aug, qk, ct, inter, m_new, decay, upd)

    sv = {}
    for key in chains:
        sidx, dec, a_inter, floor, v_aug, qk, ct, inter, m_new, decay, upd = st[key]
        sv[key] = jnp.dot((qk * dec).astype(BF16), v_aug, preferred_element_type=F32)

    for key in chains:
        bb, d, hh = key
        sidx, dec, a_inter, floor, v_aug, qk, ct, inter, m_new, decay, upd = st[key]
        both = a_inter * inter + sv[key]
        scale = 1.0 / jnp.maximum(jnp.abs(both[:, ML_DV:]), floor)
        h_ref = refs[d][3]
        for blk in range(ML_DV // LANES):
            lo = hh * ML_DV + blk * LANES
            h_ref[bb, :, lo:lo + LANES] = both[:, blk * LANES:(blk + 1) * LANES] * scale
        ct_ref[sidx] = decay * ct + upd
        m_ref[sidx] = jnp.broadcast_to(m_new, (8, LANES))


def _mlstm(mq, mk_t, mv, mg, mg_t, bias_col, bias_row):
    b, t, _ = mq.shape
    L = ML_CHUNK
    nc = t // L
    sb = ML_SEQS if b % ML_SEQS == 0 else 1
    fw = lambda width: pl.BlockSpec((sb, L, width), lambda bi, c: (bi, c, 0))
    bw = lambda width: pl.BlockSpec((sb, L, width), lambda bi, c: (bi, nc - 1 - c, 0))
    fw_t = lambda rows: pl.BlockSpec((sb, rows, L), lambda bi, c: (bi, 0, c))
    bw_t = lambda rows: pl.BlockSpec((sb, rows, L), lambda bi, c: (bi, 0, nc - 1 - c))
    ns = sb * 2 * ML_HEADS
    return pl.pallas_call(
        _mlstm_kernel,
        grid=(b // sb, nc),
        in_specs=[fw(C_MQ), fw_t(C_MK), fw(C_MV), fw(LANES), fw_t(C_MG),
                  bw(C_MQ), bw_t(C_MK), bw(C_MV), bw(LANES), bw_t(C_MG),
                  _resident((1, LANES)), _resident((C_MG, L))],
        out_specs=[fw(C_MV), bw(C_MV)],
        out_shape=[jax.ShapeDtypeStruct((b, t, C_MV), F32)] * 2,
        scratch_shapes=[pltpu.VMEM((ns, ML_DQK, ML_DV + LANES), F32),
                        pltpu.VMEM((ns, 8, LANES), F32)],
        compiler_params=_params(("parallel", "arbitrary"), 48),
        name="mlstm",
    )(mq, mk_t, mv, mg, mg_t, mq, mk_t, mv, mg, mg_t, bias_col, bias_row)


def _moe_input(x1, n2_ref, mod_ref):
    return _rms(x1, n2_ref[...]) * (1.0 + mod_ref[0, 4:5, :]) + mod_ref[0, 3:4, :]


def _mix_kernel(x_ref, att_ref, hf_ref, hb_ref, mo_ref, mlw_ref, wo_ref, mod_ref, n2_ref,
                wrh_ref, wrl_ref, br_ref, x1_ref, lg_ref):
    ml = hf_ref[0] + hb_ref[0]
    gate = jax.nn.sigmoid(mo_ref[0])
    mix = jnp.dot(att_ref[0], wo_ref[0:C_AQ, :], preferred_element_type=F32)
    for hh in range(ML_HEADS):
        sl = slice(hh * ML_DV, (hh + 1) * ML_DV)
        seg = (_rms(ml[:, sl], mlw_ref[:, sl]) * gate[:, sl]).astype(BF16)
        mix += jnp.dot(seg, wo_ref[C_AQ + hh * ML_DV:C_AQ + (hh + 1) * ML_DV, :], preferred_element_type=F32)
    x1 = x_ref[0] + mod_ref[0, 2:3, :] * mix
    x1_ref[0] = x1
    h2 = _moe_input(x1, n2_ref, mod_ref)
    h_hi = h2.astype(BF16)
    h_lo = (h2 - h_hi.astype(F32)).astype(BF16)
    nt = (((1,), (1,)), ((), ()))
    both = lax.dot_general(jnp.concatenate([wrh_ref[...], wrl_ref[...]], axis=0), h_hi, nt, preferred_element_type=F32)
    lg = both[:ROUTER_ROWS] + both[ROUTER_ROWS:] + lax.dot_general(wrh_ref[...], h_lo, nt, preferred_element_type=F32)
    lg_ref[0] = lg + br_ref[:, 0:1]


def _mix_and_router(x, att, h_fw, h_bw, mo, ml_norm_w, w_out, mod, norm2_w, wr_hi, wr_lo, b_router):
    b, t, d = x.shape
    tm = min(TM_MIX, t)
    row = lambda width: pl.BlockSpec((1, tm, width), lambda bi, i: (bi, i, 0))
    return pl.pallas_call(
        _mix_kernel,
        grid=(b, t // tm),
        in_specs=[row(d), row(C_AQ), row(C_MV), row(C_MV), row(C_MO),
                  _resident((1, C_MV)), _resident((d, d)),
                  pl.BlockSpec((1, 6, d), lambda bi, i: (bi, 0, 0)),
                  _resident((1, d)), _resident((ROUTER_ROWS, d)), _resident((ROUTER_ROWS, d)),
                  _resident((ROUTER_ROWS, LANES))],
        out_specs=[row(d), pl.BlockSpec((1, ROUTER_ROWS, tm), lambda bi, i: (bi, 0, i))],
        out_shape=[jax.ShapeDtypeStruct((b, t, d), F32), jax.ShapeDtypeStruct((b, ROUTER_ROWS, t), F32)],
        compiler_params=_params(("parallel", "arbitrary"), 48),
        name="mix_router",
    )(x, att, h_fw, h_bw, mo, ml_norm_w, w_out, mod, norm2_w, wr_hi, wr_lo, b_router)


def _first_argmax(rows):
    best = rows[0]
    idx = jnp.zeros_like(best)
    for j in range(1, len(rows)):
        better = rows[j] > best
        best = jnp.where(better, rows[j], best)
        idx = jnp.where(better, float(j), idx)
    return best, idx


def _softmax_rows(rows):
    mx = functools.reduce(jnp.maximum, rows)
    ex = [jnp.exp(r - mx) for r in rows]
    tot = functools.reduce(lambda a, c: a + c, ex)
    return [e / tot for e in ex]


def _group_of(seq, seq_starts):
    out = []
    for gi, lo in enumerate(seq_starts[:-1]):
        hi = seq_starts[gi + 1]
        out.append(((seq >= lo) & (seq < hi), jnp.clip(seq - lo, 0, hi - lo - 1)))
    return out


def _select_group(seq, seq_starts, values):
    picked = values[-1]
    for (active, _), v in list(zip(_group_of(seq, seq_starts), values))[-2::-1]:
        picked = jnp.where(active, v, picked)
    return picked


def _route_kernel(seq_starts, *refs):
    lg_refs = refs[:len(seq_starts) - 1]
    idx_ref, wt_ref, cnt_ref, run_ref = refs[len(seq_starts) - 1:]

    @pl.when((pl.program_id(0) == 0) & (pl.program_id(1) == 0))
    def _():
        run_ref[...] = jnp.zeros_like(run_ref)

    tl = idx_ref.shape[2]
    lg = _select_group(pl.program_id(0), seq_starts, [r[0] for r in lg_refs])
    p_grp = _softmax_rows([lg[g:g + 1, :] for g in range(N_GROUPS)])
    p_g, g_idx = _first_argmax(p_grp)
    el = []
    for j in range(EXPERTS_PER_GROUP):
        sel = lg[N_GROUPS + (N_GROUPS - 1) * EXPERTS_PER_GROUP + j:N_GROUPS + (N_GROUPS - 1) * EXPERTS_PER_GROUP + j + 1, :]
        for g in range(N_GROUPS - 2, -1, -1):
            r = N_GROUPS + g * EXPERTS_PER_GROUP + j
            sel = jnp.where(g_idx == float(g), lg[r:r + 1, :], sel)
        el.append(sel)
    pe = _softmax_rows(el)
    w1, i1 = _first_argmax(pe)
    rest = [jnp.where(i1 == float(j), -1.0, pe[j]) for j in range(EXPERTS_PER_GROUP)]
    w2, i2 = _first_argmax(rest)
    tot = w1 + w2
    wt1 = w1 / tot * p_g
    wt2 = w2 / tot * p_g
    lo = jnp.minimum(i1, i2)
    hi = jnp.maximum(i1, i2)
    pair = jnp.where(hi == 3.0, lo, jnp.where(lo == 1.0, 3.0, jnp.where(hi == 1.0, 4.0, 5.0)))
    slot_a = functools.reduce(lambda acc, p: jnp.where(pair == float(p), float(PAIR_SLOT_A[p]), acc),
                              range(1, N_PAIRS), jnp.full_like(pair, float(PAIR_SLOT_A[0])))
    first_in_a = slot_a == i1
    wa = jnp.where(first_in_a, wt1, wt2)
    wb = jnp.where(first_in_a, wt2, wt1)
    bucket = (g_idx * float(N_PAIRS) + pair).astype(I32)

    brow = lax.broadcasted_iota(I32, (BUCKET_ROWS, tl), 0)
    hit = brow == bucket
    onehot = hit.astype(F32)
    before = lax.broadcasted_iota(I32, (tl, tl), 0) <= lax.broadcasted_iota(I32, (tl, tl), 1)
    incl = jnp.dot(onehot.astype(BF16), before.astype(BF16), preferred_element_type=F32)
    base = run_ref[:, 0:1] + incl - onehot
    rank = jnp.sum(jnp.where(hit, base, 0.0), axis=0, keepdims=True).astype(I32)
    total = run_ref[...] + jnp.sum(onehot, axis=-1, keepdims=True)
    run_ref[...] = total
    cnt_ref[...] = total.astype(I32)

    idx_ref[0] = jnp.concatenate([bucket, rank, jnp.zeros((6, tl), I32)], axis=0)
    wt_ref[0] = jnp.concatenate([wa, wb, jnp.zeros((6, tl), F32)], axis=0)


def _group_spec(block, seq_starts, gi, seq_of, inner_of, place):
    lo, hi = seq_starts[gi], seq_starts[gi + 1]

    def index_map(*ids):
        seq, inner, n_inner = seq_of(*ids), inner_of(*ids), place[1]
        local = jnp.clip(seq - lo, 0, hi - lo - 1)
        inner = jnp.where(seq < lo, 0, jnp.where(seq >= hi, n_inner - 1, inner))
        return place[0](local, inner)

    return pl.BlockSpec(block, index_map)


def _route(logits_list, seq_starts):
    b = seq_starts[-1]
    t = logits_list[0].shape[2]
    tl = min(TL_ROUTE, t)
    nt = t // tl
    blk = lambda rows: pl.BlockSpec((1, rows, tl), lambda bi, i: (bi, 0, i))
    lg_specs = [_group_spec((1, ROUTER_ROWS, tl), seq_starts, gi, lambda bi, i: bi, lambda bi, i: i,
                            (lambda s, i: (s, 0, i), nt)) for gi in range(len(logits_list))]
    return pl.pallas_call(
        functools.partial(_route_kernel, seq_starts),
        grid=(b, nt),
        in_specs=lg_specs,
        out_specs=[blk(8), blk(8), pl.BlockSpec((BUCKET_ROWS, LANES), lambda bi, i: (0, 0))],
        out_shape=[jax.ShapeDtypeStruct((b, 8, t), I32), jax.ShapeDtypeStruct((b, 8, t), F32),
                   jax.ShapeDtypeStruct((BUCKET_ROWS, LANES), I32)],
        scratch_shapes=[pltpu.VMEM((BUCKET_ROWS, LANES), F32)],
        compiler_params=_params(("arbitrary", "arbitrary"), 32),
        name="route",
    )(*logits_list)


def _row_copy(src_ref, src_row, dst_ref, dst_row, sem):
    return pltpu.make_async_copy(src_ref.at[pl.ds(src_row, 1), :], dst_ref.at[pl.ds(dst_row, 1), :], sem)


def _dispatch_kernel(seq_starts, tiles_per_seq, padlo_ref, padhi_ref, dest_ref, *refs):
    x1_refs = refs[:len(seq_starts) - 1]
    wtail_ref, n2_ref, mod_ref, xs_ref, h_ref, zero_ref, sem = refs[len(seq_starts) - 1:]
    _, td, d = x1_refs[0].shape
    i = pl.program_id(0)
    seq = i // tiles_per_seq
    slot = i % 2
    rows = h_ref.at[slot]
    for c0 in range(0, td, ROW_CHUNK):
        sl = slice(c0, c0 + ROW_CHUNK)
        x1 = _select_group(seq, seq_starts, [r[0, sl, :] for r in x1_refs])
        rows[sl, 0:d] = _moe_input(x1, n2_ref, mod_ref)
        rows[sl, d:] = wtail_ref[sl, :]
        for t in range(c0, c0 + ROW_CHUNK):
            _row_copy(rows, t, xs_ref, dest_ref[0, 0, t], sem.at[slot]).start()

    def drain(which):
        pltpu.make_async_copy(h_ref.at[which], xs_ref.at[pl.ds(0, td), :], sem.at[which]).wait()

    @pl.when(i > 0)
    def _():
        drain(1 - slot)

    @pl.when(i == pl.num_programs(0) - 1)
    def _():
        drain(slot)
        zero_ref[...] = jnp.zeros_like(zero_ref)
        zsem = sem.at[0]

        sub = zero_ref.shape[0]

        def one_row(r):
            return _row_copy(zero_ref, 0, xs_ref, r, zsem)

        def aligned_rows(r):
            return pltpu.make_async_copy(zero_ref, xs_ref.at[pl.ds(pl.multiple_of(r, sub), sub), :], zsem)

        def spans(e):
            lo, hi = padlo_ref[e], padhi_ref[e]
            mid = jnp.minimum(hi, (lo + sub - 1) // sub * sub)
            return lo, mid, (hi - mid) // sub

        for wait in (False, True):
            for e in range(N_BUCKETS):
                lo, mid, nblk = spans(e)

                def head(r, carry):
                    one_row(r).wait() if wait else one_row(r).start()
                    return carry

                def body(c, carry):
                    cp = aligned_rows(mid + c * sub)
                    cp.wait() if wait else cp.start()
                    return carry

                lax.fori_loop(lo, mid, head, 0)
                lax.fori_loop(0, nblk, body, 0)


def _dispatch(x1_list, seq_starts, dest, w_tail, norm2_w, mod, pad_lo, pad_hi, m_pad):
    d = x1_list[0].shape[2]
    b, _, t = dest.shape
    n = b * t
    td = min(TD_DISPATCH, t)
    per = t // td
    width = d + LANES
    x1_specs = [_group_spec((1, td, d), seq_starts, gi, lambda i, *_: i // per, lambda i, *_: i % per,
                            (lambda s, j: (s, j, 0), per)) for gi in range(len(x1_list))]
    return pl.pallas_call(
        functools.partial(_dispatch_kernel, seq_starts, per),
        grid_spec=pltpu.PrefetchScalarGridSpec(
            num_scalar_prefetch=2,
            grid=(n // td,),
            in_specs=[pl.BlockSpec((1, 1, td), lambda i, *_: (i // per, 0, i % per), memory_space=pltpu.SMEM),
                      *x1_specs,
                      pl.BlockSpec((td, LANES), lambda i, *_: (i, 0)),
                      pl.BlockSpec((1, d), lambda i, *_: (0, 0)),
                      pl.BlockSpec((1, 6, d), lambda i, *_: (i // per, 0, 0))],
            out_specs=pl.BlockSpec(memory_space=pl.ANY),
            scratch_shapes=[pltpu.VMEM((2, td, width), F32), pltpu.VMEM((8, width), F32),
                            pltpu.SemaphoreType.DMA((2,))]),
        out_shape=jax.ShapeDtypeStruct((m_pad, width), F32),
        compiler_params=_params(("arbitrary",), 32),
        name="dispatch",
    )(pad_lo, pad_hi, dest, *x1_list, w_tail, norm2_w, mod)


def _moe_kernel(ea_ref, eb_ref, nused_ref, x_ref, wga_ref, wua_ref, wda_ref, wgb_ref, wub_ref, wdb_ref, o_ref):
    r = pl.program_id(0)
    d = o_ref.shape[1]

    @pl.when(r < nused_ref[0])
    def _():
        xb = x_ref[:, 0:d].astype(BF16)

        def expert(wg_ref, wu_ref, wd_ref):
            g = jnp.dot(xb, wg_ref[0], preferred_element_type=F32)
            u = jnp.dot(xb, wu_ref[0], preferred_element_type=F32)
            h = (g * jax.nn.sigmoid(g) * u).astype(BF16)
            return jnp.dot(h, wd_ref[0], preferred_element_type=F32)

        tail = x_ref[:, d:]
        o_ref[...] = (expert(wga_ref, wua_ref, wda_ref) * _lane_pick(tail, 0)
                      + expert(wgb_ref, wub_ref, wdb_ref) * _lane_pick(tail, 1))

    @pl.when(r >= nused_ref[0])
    def _():
        o_ref[...] = jnp.zeros_like(o_ref)


def _moe_experts(xs, blk_ea, blk_eb, nused, w_gate, w_up, w_down):
    m_pad, width = xs.shape
    d = width - LANES
    f = w_gate.shape[2]
    tm = TM_MOE
    last = lambda r, nu: jnp.minimum(r, nu[0] - 1)
    slot = lambda shape, pick, bufs: pl.BlockSpec(shape, lambda r, ea, eb, nu: (pick(ea, eb)[last(r, nu)], 0, 0),
                                                  pipeline_mode=pl.Buffered(bufs))
    slot_a = lambda shape: slot(shape, lambda ea, eb: ea, 2)
    slot_b = lambda shape: slot(shape, lambda ea, eb: eb, 1)
    return pl.pallas_call(
        _moe_kernel,
        grid_spec=pltpu.PrefetchScalarGridSpec(
            num_scalar_prefetch=3,
            grid=(m_pad // tm,),
            in_specs=[pl.BlockSpec((tm, width), lambda r, ea, eb, nu: (last(r, nu), 0)),
                      slot_a((1, d, f)), slot_a((1, d, f)), slot_a((1, f, d)),
                      slot_b((1, d, f)), slot_b((1, d, f)), slot_b((1, f, d))],
            out_specs=pl.BlockSpec((tm, d), lambda r, ea, eb, nu: (r, 0))),
        out_shape=jax.ShapeDtypeStruct((m_pad, d), F32),
        compiler_params=_params(("arbitrary",), 58),
        name="moe_experts",
    )(blk_ea, blk_eb, nused, xs, w_gate, w_up, w_down, w_gate, w_up, w_down)


def _combine_kernel(dest_ref, dnext_ref, x1_ref, mod_ref, o_ref, y_ref, g_ref, sem):
    tc = x1_ref.shape[0]
    i = pl.program_id(0)
    slot = i % 2

    def drain(which):
        pltpu.make_async_copy(o_ref.at[pl.ds(0, tc), :], g_ref.at[which], sem.at[which]).wait()

    @pl.when(i == 0)
    def _():
        def start(t, carry):
            _row_copy(o_ref, dest_ref[0, 0, t], g_ref.at[slot], t, sem.at[slot]).start()
            return carry

        lax.fori_loop(0, tc, start, 0, unroll=DMA_UNROLL)

    drain(slot)
    rows = g_ref.at[slot]
    for c0 in range(0, tc, ROW_CHUNK):
        sl = slice(c0, c0 + ROW_CHUNK)
        y_ref[sl, :] = x1_ref[sl, :] + mod_ref[0, 5:6, :] * rows[sl, :]
        for t in range(c0, c0 + ROW_CHUNK):
            _row_copy(o_ref, dnext_ref[0, 0, t], g_ref.at[1 - slot], t, sem.at[1 - slot]).start()

    @pl.when(i == pl.num_programs(0) - 1)
    def _():
        drain(1 - slot)


def _combine(x1, dest, mod, o_rows):
    nseq, t, d = x1.shape
    x1 = x1.reshape(nseq * t, d)
    tc = min(TC_COMBINE, t)
    per = t // tc
    ntiles = nseq * per
    tile = lambda g: (g // per, 0, g % per)
    return pl.pallas_call(
        _combine_kernel,
        grid=(ntiles,),
        in_specs=[pl.BlockSpec((1, 1, tc), tile, memory_space=pltpu.SMEM),
                  pl.BlockSpec((1, 1, tc), lambda i: tile(jnp.minimum(i + 1, ntiles - 1)), memory_space=pltpu.SMEM),
                  pl.BlockSpec((tc, d), lambda i: (i, 0)),
                  pl.BlockSpec((1, 6, d), lambda i: (i // per, 0, 0)),
                  pl.BlockSpec(memory_space=pl.ANY)],
        out_specs=pl.BlockSpec((tc, d), lambda i: (i, 0)),
        scratch_shapes=[pltpu.VMEM((2, tc, d), F32), pltpu.SemaphoreType.DMA((2,))],
        out_shape=jax.ShapeDtypeStruct((nseq * t, d), F32),
        compiler_params=_params(("arbitrary",), 32),
        name="combine",
    )(dest, dest, x1, mod, o_rows)


def _rope_tables(t):
    rows = t // GRID_W
    row = jnp.repeat(jnp.arange(rows, dtype=F32), GRID_W)
    col = jnp.tile(jnp.arange(GRID_W, dtype=F32), rows)
    freqs = ROPE_THETA ** (-jnp.arange(ROPE_PAIRS, dtype=F32) / ROPE_PAIRS)
    ar = row[:, None] * freqs
    ac = col[:, None] * freqs
    cos_t = jnp.concatenate([jnp.cos(ar), jnp.cos(ar), jnp.cos(ac), jnp.cos(ac)], axis=-1)
    sin_t = jnp.concatenate([-jnp.sin(ar), jnp.sin(ar), -jnp.sin(ac), jnp.sin(ac)], axis=-1)
    return cos_t, sin_t


def _token_mixing(x, mod, p):
    b, t, d = x.shape
    cos_t, sin_t = _rope_tables(t)
    aq, ak, av_t, mq, mk_t, mv, mo, mg = _in_projection(
        x, mod, p["norm1_w"], p["w_main"], p["w_v_t"], p["w_gates"], p["q_norm_w"], p["k_norm_w"], cos_t, sin_t)
    att = _attention(aq, ak, av_t)
    mg_t = jnp.swapaxes(mg[:, :, :C_MG], 1, 2)
    h_fw, h_bw = _mlstm(mq, mk_t, mv, mg, mg_t, p["gate_bias_col"], p["gate_bias_row"])
    return _mix_and_router(x, att, h_fw, h_bw, mo, p["ml_norm_w"], p["w_out"], mod,
                           p["norm2_w"], p["wr_hi"], p["wr_lo"], p["b_router"])


def _channel_mixing(x1_list, logits_list, mod, p):
    seq_starts = tuple(int(v) for v in np.cumsum([0] + [x1.shape[0] for x1 in x1_list]))
    b = seq_starts[-1]
    _, t, d = x1_list[0].shape
    n = b * t
    idx, wts, counts = _route(logits_list, seq_starts)

    tm = TM_MOE
    counts = counts[:N_BUCKETS, 0]
    padded = (counts + tm - 1) // tm * tm
    pend = jnp.cumsum(padded)
    pstart = (pend - padded).astype(I32)
    nb = (n + tm - 1) // tm + N_BUCKETS
    m_pad = nb * tm
    block_row0 = jnp.arange(nb, dtype=I32) * tm
    blk_bucket = jnp.minimum(jnp.sum(pend[None, :] <= block_row0[:, None], axis=1), N_BUCKETS - 1)
    group_base = np.repeat(np.arange(N_GROUPS) * EXPERTS_PER_GROUP, N_PAIRS)
    expert_a = jnp.asarray(group_base + np.tile(PAIR_SLOT_A, N_GROUPS), I32)
    expert_b = jnp.asarray(group_base + np.tile(PAIR_SLOT_B, N_GROUPS), I32)
    blk_onehot = blk_bucket[:, None] == jnp.arange(N_BUCKETS)
    blk_ea = jnp.sum(jnp.where(blk_onehot, expert_a, 0), axis=1).astype(I32)
    blk_eb = jnp.sum(jnp.where(blk_onehot, expert_b, 0), axis=1).astype(I32)
    nused = (pend[-1:] // tm).astype(I32)
    onehot = idx[:, 0:1, :, None] == jnp.arange(N_BUCKETS, dtype=I32)
    dest = jnp.sum(jnp.where(onehot, pstart, 0), axis=-1) + idx[:, 1:2, :]

    pad_lo = (pstart + counts).astype(I32)
    pad_hi = jnp.concatenate([pstart[1:], jnp.full((1,), m_pad, I32)])
    w_tail = jnp.zeros((n, LANES), F32).at[:, :2].set(jnp.swapaxes(wts[:, :2, :], 1, 2).reshape(n, 2))
    xs = _dispatch(x1_list, seq_starts, dest, w_tail, p["norm2_w"], mod, pad_lo, pad_hi, m_pad)
    o_rows = _moe_experts(xs, blk_ea, blk_eb, nused, p["w_gate"], p["w_up"], p["w_down"])
    return [_combine(x1, dest[s0:s0 + x1.shape[0]], mod[s0:s0 + x1.shape[0]], o_rows).reshape(x1.shape)
            for x1, s0 in zip(x1_list, seq_starts)]


def kernel(x_prompt, x_sample, c_prompt, c_sample, norm1_w, norm2_w, w_ada, b_ada, w_in, q_norm_w, k_norm_w, b_igate, b_fgate, ml_norm_w, w_out, w_gr, b_gr, w_er, b_er, w_gate, w_up, w_down):
    assert x_prompt.shape[1:] == x_sample.shape[1:], "the request groups share one token buffer per sequence length"
    depth = norm1_w.shape[0]
    d = x_prompt.shape[-1]
    bp = x_prompt.shape[0]
    bs = x_sample.shape[0]
    rows = -(-(bp + bs) // 8) * 8
    y_prompt, y_sample = x_prompt, x_sample
    for l in range(depth):
        c_pad = jnp.zeros((rows, d), F32).at[:bp].set(c_prompt).at[bp:bp + bs].set(c_sample)
        mod = _ada_modulation(c_pad, w_ada[l], b_ada[l]).reshape(rows, 6, d)
        gate_bias = jnp.concatenate([b_igate[l].reshape(-1), b_fgate[l].reshape(-1)])
        w_router = jnp.concatenate([w_gr[l], w_er[l]], axis=1).T
        w_router = jnp.zeros((ROUTER_ROWS, d), F32).at[:N_GROUPS + N_EXPERTS].set(w_router)
        wr_hi = w_router.astype(BF16)
        b_router = jnp.zeros((ROUTER_ROWS,), F32).at[:N_GROUPS + N_EXPERTS].set(jnp.concatenate([b_gr[l], b_er[l]]))
        p = {
            "norm1_w": norm1_w[l].reshape(1, d),
            "norm2_w": norm2_w[l].reshape(1, d),
            "w_main": w_in[l][:, :C_MAIN].astype(BF16),
            "w_v_t": jnp.concatenate([w_in[l][:, C_AQ + C_AK:C_AQ + C_AK + C_AV],
                                      w_in[l][:, C_AQ + C_AK + C_AV + C_MQ:C_AQ + C_AK + C_AV + C_MQ + C_MK]],
                                     axis=1).T.astype(BF16),
            "w_gates": jnp.zeros((d, LANES), BF16).at[:, :C_MG].set(w_in[l][:, C_MAIN:].astype(BF16)),
            "q_norm_w": q_norm_w[l].reshape(1, HEAD_DIM),
            "k_norm_w": k_norm_w[l].reshape(1, HEAD_DIM),
            "gate_bias_col": jnp.zeros((1, LANES), F32).at[0, :C_MG].set(gate_bias),
            "gate_bias_row": jnp.broadcast_to(gate_bias[:, None], (C_MG, ML_CHUNK)),
            "ml_norm_w": ml_norm_w[l].reshape(1, C_MV),
            "w_out": w_out[l].astype(BF16),
            "wr_hi": wr_hi,
            "wr_lo": (w_router - wr_hi.astype(F32)).astype(BF16),
            "b_router": jnp.broadcast_to(b_router[:, None], (ROUTER_ROWS, LANES)),
            "w_gate": w_gate[l].astype(BF16),
            "w_up": w_up[l].astype(BF16),
            "w_down": w_down[l].astype(BF16),
        }
        x1_p, logits_p = _token_mixing(y_prompt, mod[:bp], p)
        x1_s, logits_s = _token_mixing(y_sample, mod[bp:bp + bs], p)
        y_prompt, y_sample = _channel_mixing([x1_p, x1_s], [logits_p, logits_s], mod, p)
    return (y_prompt, y_sample)
```

```python
import functools

import jax
import jax.numpy as jnp
import numpy as np
from jax import lax
from jax.experimental import pallas as pl
from jax.experimental.pallas import tpu as pltpu

F32 = jnp.float32
BF16 = jnp.bfloat16
I32 = jnp.int32

GRID_W = 64
HEAD_DIM = 128
ATT_HEADS = 8
ATT_KV_HEADS = 2
ATT_GROUP = ATT_HEADS // ATT_KV_HEADS
ROPE_THETA = 10000.0
ROPE_PAIRS = HEAD_DIM // 4
ML_HEADS = 4
ML_DV = 256
ML_DQK = 128
ML_CHUNK = 128
N_GROUPS = 4
EXPERTS_PER_GROUP = 4
N_EXPERTS = N_GROUPS * EXPERTS_PER_GROUP
EPS = 1e-6
Q_SCALE = HEAD_DIM ** -0.5 * float(np.log2(np.e))

C_AQ = ATT_HEADS * HEAD_DIM
C_AK = ATT_KV_HEADS * HEAD_DIM
C_AV = ATT_KV_HEADS * HEAD_DIM
C_MQ = ML_HEADS * ML_DQK
C_MK = ML_HEADS * ML_DQK
C_MV = ML_HEADS * ML_DV
C_MO = ML_HEADS * ML_DV
C_MG = 4 * ML_HEADS
C_MAIN = C_AQ + C_AK + C_AV + C_MQ + C_MK + C_MV + C_MO

N_PAIRS = 6
PAIR_SLOT_A = (0, 1, 2, 2, 0, 0)
PAIR_SLOT_B = (3, 3, 3, 1, 1, 2)
N_BUCKETS = N_GROUPS * N_PAIRS
BUCKET_ROWS = 32

LANES = 128
ROUTER_ROWS = 32
MIB = 1024 * 1024

TM_PROJ = 256
TQ_ATT = 256
TM_MIX = 512
TL_ROUTE = 512
TD_DISPATCH = 256
TM_MOE = 256
TC_COMBINE = 256
ML_SEQS = 4
ATT_SOFTMAX_LAG = 2
ATT_VALUES_LAG = 1
ATT_SLOTS = 4
ROW_CHUNK = 32
DMA_UNROLL = 8


def _params(semantics, vmem_mib):
    return pltpu.CompilerParams(dimension_semantics=semantics, vmem_limit_bytes=vmem_mib * MIB)


def _resident(shape):
    nd = len(shape)
    return pl.BlockSpec(shape, lambda *_: (0,) * nd, pipeline_mode=pl.Buffered(1))


def _ada_kernel(c_ref, w_ref, b_ref, o_ref):
    c = c_ref[...]
    s = (c * jax.nn.sigmoid(c)).astype(BF16)
    o_ref[...] = jnp.dot(s, w_ref[...].astype(BF16), preferred_element_type=F32) + b_ref[...]


def _ada_modulation(c_pad, w_ada, b_ada):
    rows, d = c_pad.shape
    n = w_ada.shape[1]
    tn = 1024
    return pl.pallas_call(
        _ada_kernel,
        grid=(n // tn,),
        in_specs=[pl.BlockSpec((rows, d), lambda j: (0, 0)),
                  pl.BlockSpec((d, tn), lambda j: (0, j)),
                  pl.BlockSpec((1, tn), lambda j: (0, j))],
        out_specs=pl.BlockSpec((rows, tn), lambda j: (0, j)),
        out_shape=jax.ShapeDtypeStruct((rows, n), F32),
        compiler_params=_params(("arbitrary",), 40),
        name="ada_modulation",
    )(c_pad, w_ada, b_ada.reshape(1, n))


def _rms(x, w):
    return x * lax.rsqrt(jnp.mean(x * x, axis=-1, keepdims=True) + EPS) * w


def _inproj_kernel(x_ref, mod_ref, n1_ref, w_ref, wvt_ref, wg_ref, qn_ref, kn_ref, cos_ref, sin_ref,
                   aq_ref, ak_ref, avt_ref, mq_ref, mkt_ref, mv_ref, mo_ref, mg_ref):
    x = x_ref[0]
    h = _rms(x, n1_ref[...]) * (1.0 + mod_ref[0, 1:2, :]) + mod_ref[0, 0:1, :]
    hb = h.astype(BF16)

    def proj(c0, width):
        return jnp.dot(hb, w_ref[:, c0:c0 + width], preferred_element_type=F32)

    cos = cos_ref[...]
    sin = sin_ref[...]
    lane = lax.broadcasted_iota(I32, (1, HEAD_DIM), 1)
    first = (lane % (2 * ROPE_PAIRS)) < ROPE_PAIRS

    def norm_rope(p, w):
        pn = _rms(p, w)
        partner = jnp.where(first, pltpu.roll(pn, HEAD_DIM - ROPE_PAIRS, 1), pltpu.roll(pn, ROPE_PAIRS, 1))
        return pn * cos + partner * sin

    c0 = 0
    for half in range(2):
        p = proj(c0, C_AQ // 2)
        for hh in range(ATT_HEADS // 2):
            col = half * (C_AQ // 2) + hh * HEAD_DIM
            qh = norm_rope(p[:, hh * HEAD_DIM:(hh + 1) * HEAD_DIM], qn_ref[...])
            aq_ref[0, :, col:col + HEAD_DIM] = (qh * Q_SCALE).astype(BF16)
        c0 += C_AQ // 2
    p = proj(c0, C_AK)
    for hh in range(ATT_KV_HEADS):
        ak_ref[0, :, hh * HEAD_DIM:(hh + 1) * HEAD_DIM] = norm_rope(p[:, hh * HEAD_DIM:(hh + 1) * HEAD_DIM], kn_ref[...]).astype(BF16)
    nt = (((1,), (1,)), ((), ()))
    avt_ref[0] = lax.dot_general(wvt_ref[0:C_AV, :], hb, nt, preferred_element_type=F32).astype(BF16)
    mkt_ref[0] = lax.dot_general(wvt_ref[C_AV:, :], hb, nt, preferred_element_type=F32).astype(BF16)
    c0 += C_AK + C_AV
    mq_ref[0] = (proj(c0, C_MQ) * (ML_DQK ** -0.5)).astype(BF16)
    c0 += C_MQ + C_MK
    for half in range(2):
        mv_ref[0, :, half * 512:(half + 1) * 512] = proj(c0, 512).astype(BF16)
        c0 += 512
    for half in range(2):
        mo_ref[0, :, half * 512:(half + 1) * 512] = proj(c0, 512)
        c0 += 512
    mg_ref[0] = jnp.dot(hb, wg_ref[...], preferred_element_type=F32)


def _in_projection(x, mod, norm1_w, w_main, w_v_t, w_gates, q_norm_w, k_norm_w, cos_t, sin_t):
    b, t, d = x.shape
    tm = min(TM_PROJ, t)
    row = lambda width: pl.BlockSpec((1, tm, width), lambda bi, i: (bi, i, 0))
    outs = ((C_AQ, False), (C_AK, False), (C_AV, True), (C_MQ, False), (C_MK, True), (C_MV, False),
            (C_MO, False), (LANES, False))
    dtypes = (BF16, BF16, BF16, BF16, BF16, BF16, F32, F32)
    col = lambda width: pl.BlockSpec((1, width, tm), lambda bi, i: (bi, 0, i))
    return pl.pallas_call(
        _inproj_kernel,
        grid=(b, t // tm),
        in_specs=[row(d),
                  pl.BlockSpec((1, 6, d), lambda bi, i: (bi, 0, 0)),
                  _resident((1, d)),
                  _resident((d, C_MAIN)),
                  _resident((C_AV + C_MK, d)),
                  _resident((d, LANES)),
                  _resident((1, HEAD_DIM)),
                  _resident((1, HEAD_DIM)),
                  pl.BlockSpec((tm, HEAD_DIM), lambda bi, i: (i, 0)),
                  pl.BlockSpec((tm, HEAD_DIM), lambda bi, i: (i, 0))],
        out_specs=[col(w) if tr else row(w) for w, tr in outs],
        out_shape=[jax.ShapeDtypeStruct((b, w, t) if tr else (b, t, w), dt) for (w, tr), dt in zip(outs, dtypes)],
        compiler_params=_params(("parallel", "arbitrary"), 48),
        name="in_projection",
    )(x, mod, norm1_w, w_main, w_v_t, w_gates, q_norm_w, k_norm_w, cos_t, sin_t)


def _attention_kernel(q_ref, qn_ref, k_ref, vt_ref, o_ref, s_ref, p_ref, l_ref):
    def scores(g, src_ref=q_ref):
        kv = g // ATT_GROUP
        q = src_ref[0, :, g * HEAD_DIM:(g + 1) * HEAD_DIM]
        k = k_ref[0, :, kv * HEAD_DIM:(kv + 1) * HEAD_DIM]
        s_ref[g % ATT_SLOTS] = lax.dot_general(k, q, (((1,), (1,)), ((), ())), preferred_element_type=F32)

    def softmax(g):
        s = s_ref[g % ATT_SLOTS]
        m = jnp.max(s, axis=0, keepdims=True)
        p = jnp.exp2(s - m)
        l_ref[g] = jnp.broadcast_to(jnp.sum(p, axis=0, keepdims=True), l_ref.shape[1:])
        p_ref[g % ATT_SLOTS] = p.astype(BF16)

    def values(g):
        kv = g // ATT_GROUP
        vt = vt_ref[0, kv * HEAD_DIM:(kv + 1) * HEAD_DIM, :]
        o_t = jnp.dot(vt, p_ref[g % ATT_SLOTS], preferred_element_type=F32)
        o_ref[0, :, g * HEAD_DIM:(g + 1) * HEAD_DIM] = (o_t * (1.0 / l_ref[g][0:1, :])).T.astype(BF16)

    n = ATT_HEADS
    lag = ATT_SOFTMAX_LAG

    @pl.when(pl.program_id(1) == 0)
    def _():
        for g in range(lag):
            scores(g)

    for g in range(n):
        if g + lag < n:
            scores(g + lag)
        else:
            scores(g + lag - n, qn_ref)
        softmax(g)
        if g >= ATT_VALUES_LAG:
            values(g - ATT_VALUES_LAG)
    for g in range(n - ATT_VALUES_LAG, n):
        values(g)


def _attention(aq, ak, av_t):
    b, t, _ = aq.shape
    tq = min(TQ_ATT, t)
    nt = t // tq
    return pl.pallas_call(
        _attention_kernel,
        grid=(b, nt),
        in_specs=[pl.BlockSpec((1, tq, C_AQ), lambda bi, i: (bi, i, 0)),
                  pl.BlockSpec((1, tq, C_AQ), lambda bi, i: (bi, jnp.minimum(i + 1, nt - 1), 0)),
                  pl.BlockSpec((1, t, C_AK), lambda bi, i: (bi, 0, 0)),
                  pl.BlockSpec((1, C_AV, t), lambda bi, i: (bi, 0, 0))],
        out_specs=pl.BlockSpec((1, tq, C_AQ), lambda bi, i: (bi, i, 0)),
        out_shape=jax.ShapeDtypeStruct((b, t, C_AQ), BF16),
        scratch_shapes=[pltpu.VMEM((ATT_SLOTS, t, tq), F32), pltpu.VMEM((ATT_SLOTS, t, tq), BF16),
                        pltpu.VMEM((ATT_HEADS, 8, tq), F32)],
        compiler_params=_params(("parallel", "arbitrary"), 48),
        name="attention",
    )(aq, aq, ak, av_t)


def _log_sigmoid(x):
    return jnp.minimum(x, 0.0) - jnp.log1p(jnp.exp(-jnp.abs(x)))


def _split3(a):
    a1 = a.astype(BF16)
    r1 = a - a1.astype(F32)
    a2 = r1.astype(BF16)
    a3 = (r1 - a2.astype(F32)).astype(BF16)
    return a1, a2, a3


def _lane_pick(a, idx):
    lane = lax.broadcasted_iota(I32, a.shape, 1)
    return jnp.sum(jnp.where(lane == idx, a, 0.0), axis=-1, keepdims=True)


def _mlstm_kernel(qf, kf, vf, gcf, grf, qb, kb, vb, gcb, grb, bc_ref, br_ref, hf_ref, hb_ref,
                  ct_ref, m_ref):
    @pl.when(pl.program_id(1) == 0)
    def _():
        ct_ref[...] = jnp.zeros_like(ct_ref)
        m_ref[...] = jnp.zeros_like(m_ref)

    L = ML_CHUNK
    row = lax.broadcasted_iota(I32, (L, L), 0)
    col = lax.broadcasted_iota(I32, (L, L), 1)
    lower = (col <= row)
    upper = (col >= row)
    lower_b = lower.astype(BF16)
    upper_b = upper.astype(BF16)

    gates = {}
    for bb in range(qf.shape[0]):
        for d, (gc_ref, gr_ref) in enumerate(((gcf, grf), (gcb, grb))):
            cum_cols = upper_b if d else lower_b
            cum_rows = lower_b if d else upper_b
            pre_c = gc_ref[bb] + bc_ref[...]
            pre_r = gr_ref[bb] + br_ref[...]
            b_cols = sum(jnp.dot(cum_cols, part, preferred_element_type=F32) for part in _split3(_log_sigmoid(pre_c)))
            b_rows = sum(jnp.dot(part, cum_rows, preferred_element_type=F32) for part in _split3(_log_sigmoid(pre_r)))
            gates[bb, d] = (pre_r, b_cols, b_rows)

    chains = [(bb, d, hh) for bb in range(qf.shape[0]) for d in range(2) for hh in range(ML_HEADS)]
    refs = ((qf, kf, vf, hf_ref), (qb, kb, vb, hb_ref))
    ones_block = jnp.ones((L, LANES), BF16)


    st = {}
    for (bb, d, hh) in chains:
        q_ref, kt_ref, v_ref, _ = refs[d]
        pre_r, b_cols, b_rows = gates[bb, d]
        ci = d * ML_HEADS + hh
        cf = 2 * ML_HEADS + d * ML_HEADS + hh
        sidx = (bb * 2 + d) * ML_HEADS + hh
        i_row = pre_r[ci:ci + 1, :]
        b_row = b_rows[cf:cf + 1, :]
        m = m_ref[sidx][0:1, 0:1]
        c_vis = jnp.where(upper if d else lower, i_row - b_row, -jnp.inf)
        shift = -jnp.maximum(m, jnp.max(c_vis, axis=-1, keepdims=True))
        dec = jnp.exp(c_vis + shift)
        a_inter = jnp.exp(m + shift)
        floor = jnp.exp(shift - _lane_pick(b_cols, cf))
        q = q_ref[bb, :, hh * ML_DQK:(hh + 1) * ML_DQK]
        kt = kt_ref[bb, hh * ML_DQK:(hh + 1) * ML_DQK, :]
        v_aug = jnp.concatenate([v_ref[bb, :, hh * ML_DV:(hh + 1) * ML_DV], ones_block], axis=1)
        qk = jnp.dot(q, kt, preferred_element_type=F32)
        ct = ct_ref[sidx]
        inter = jnp.dot(q, ct.astype(BF16), preferred_element_type=F32)
        b_end = _lane_pick(b_row, 0 if d else L - 1)
        g_row = b_end - b_row + i_row
        m_new = jnp.maximum(b_end + m, jnp.max(g_row, axis=-1, keepdims=True))
        decay = jnp.exp(b_end + m - m_new)
        w_row = jnp.exp(g_row - m_new)
        upd = jnp.dot((kt.astype(F32) * w_row).astype(BF16), v_aug, preferred_element_type=F32)
        st[bb, d, hh] = (sidx, dec, a_inter, floor, v_aug, qk, ct, inter, m_new, decay, upd)

    sv = {}
    for key in chains:
        sidx, dec, a_inter, floor, v_aug, qk, ct, inter, m_new, decay, upd = st[key]
        sv[key] = jnp.dot((qk * dec).astype(BF16), v_aug, preferred_element_type=F32)

    for key in chains:
        bb, d, hh = key
        sidx, dec, a_inter, floor, v_aug, qk, ct, inter, m_new, decay, upd = st[key]
        both = a_inter * inter + sv[key]
        scale = 1.0 / jnp.maximum(jnp.abs(both[:, ML_DV:]), floor)
        h_ref = refs[d][3]
        for blk in range(ML_DV // LANES):
            lo = hh * ML_DV + blk * LANES
            h_ref[bb, :, lo:lo + LANES] = both[:, blk * LANES:(blk + 1) * LANES] * scale
        ct_ref[sidx] = decay * ct + upd
        m_ref[sidx] = jnp.broadcast_to(m_new, (8, LANES))


def _mlstm(mq, mk_t, mv, mg, mg_t, bias_col, bias_row):
    b, t, _ = mq.shape
    L = ML_CHUNK
    nc = t // L
    sb = ML_SEQS if b % ML_SEQS == 0 else 1
    fw = lambda width: pl.BlockSpec((sb, L, width), lambda bi, c: (bi, c, 0))
    bw = lambda width: pl.BlockSpec((sb, L, width), lambda bi, c: (bi, nc - 1 - c, 0))
    fw_t = lambda rows: pl.BlockSpec((sb, rows, L), lambda bi, c: (bi, 0, c))
    bw_t = lambda rows: pl.BlockSpec((sb, rows, L), lambda bi, c: (bi, 0, nc - 1 - c))
    ns = sb * 2 * ML_HEADS
    return pl.pallas_call(
        _mlstm_kernel,
        grid=(b // sb, nc),
        in_specs=[fw(C_MQ), fw_t(C_MK), fw(C_MV), fw(LANES), fw_t(C_MG),
                  bw(C_MQ), bw_t(C_MK), bw(C_MV), bw(LANES), bw_t(C_MG),
                  _resident((1, LANES)), _resident((C_MG, L))],
        out_specs=[fw(C_MV), bw(C_MV)],
        out_shape=[jax.ShapeDtypeStruct((b, t, C_MV), F32)] * 2,
        scratch_shapes=[pltpu.VMEM((ns, ML_DQK, ML_DV + LANES), F32),
                        pltpu.VMEM((ns, 8, LANES), F32)],
        compiler_params=_params(("parallel", "arbitrary"), 48),
        name="mlstm",
    )(mq, mk_t, mv, mg, mg_t, mq, mk_t, mv, mg, mg_t, bias_col, bias_row)


def _moe_input(x1, n2_ref, mod_ref):
    return _rms(x1, n2_ref[...]) * (1.0 + mod_ref[0, 4:5, :]) + mod_ref[0, 3:4, :]


def _mix_kernel(x_ref, att_ref, hf_ref, hb_ref, mo_ref, mlw_ref, wo_ref, mod_ref, n2_ref,
                wrh_ref, wrl_ref, br_ref, x1_ref, lg_ref):
    ml = hf_ref[0] + hb_ref[0]
    gate = jax.nn.sigmoid(mo_ref[0])
    mix = jnp.dot(att_ref[0], wo_ref[0:C_AQ, :], preferred_element_type=F32)
    for hh in range(ML_HEADS):
        sl = slice(hh * ML_DV, (hh + 1) * ML_DV)
        seg = (_rms(ml[:, sl], mlw_ref[:, sl]) * gate[:, sl]).astype(BF16)
        mix += jnp.dot(seg, wo_ref[C_AQ + hh * ML_DV:C_AQ + (hh + 1) * ML_DV, :], preferred_element_type=F32)
    x1 = x_ref[0] + mod_ref[0, 2:3, :] * mix
    x1_ref[0] = x1
    h2 = _moe_input(x1, n2_ref, mod_ref)
    h_hi = h2.astype(BF16)
    h_lo = (h2 - h_hi.astype(F32)).astype(BF16)
    nt = (((1,), (1,)), ((), ()))
    both = lax.dot_general(jnp.concatenate([wrh_ref[...], wrl_ref[...]], axis=0), h_hi, nt, preferred_element_type=F32)
    lg = both[:ROUTER_ROWS] + both[ROUTER_ROWS:] + lax.dot_general(wrh_ref[...], h_lo, nt, preferred_element_type=F32)
    lg_ref[0] = lg + br_ref[:, 0:1]


def _mix_and_router(x, att, h_fw, h_bw, mo, ml_norm_w, w_out, mod, norm2_w, wr_hi, wr_lo, b_router):
    b, t, d = x.shape
    tm = min(TM_MIX, t)
    row = lambda width: pl.BlockSpec((1, tm, width), lambda bi, i: (bi, i, 0))
    return pl.pallas_call(
        _mix_kernel,
        grid=(b, t // tm),
        in_specs=[row(d), row(C_AQ), row(C_MV), row(C_MV), row(C_MO),
                  _resident((1, C_MV)), _resident((d, d)),
                  pl.BlockSpec((1, 6, d), lambda bi, i: (bi, 0, 0)),
                  _resident((1, d)), _resident((ROUTER_ROWS, d)), _resident((ROUTER_ROWS, d)),
                  _resident((ROUTER_ROWS, LANES))],
        out_specs=[row(d), pl.BlockSpec((1, ROUTER_ROWS, tm), lambda bi, i: (bi, 0, i))],
        out_shape=[jax.ShapeDtypeStruct((b, t, d), F32), jax.ShapeDtypeStruct((b, ROUTER_ROWS, t), F32)],
        compiler_params=_params(("parallel", "arbitrary"), 48),
        name="mix_router",
    )(x, att, h_fw, h_bw, mo, ml_norm_w, w_out, mod, norm2_w, wr_hi, wr_lo, b_router)


def _first_argmax(rows):
    best = rows[0]
    idx = jnp.zeros_like(best)
    for j in range(1, len(rows)):
        better = rows[j] > best
        best = jnp.where(better, rows[j], best)
        idx = jnp.where(better, float(j), idx)
    return best, idx


def _softmax_rows(rows):
    mx = functools.reduce(jnp.maximum, rows)
    ex = [jnp.exp(r - mx) for r in rows]
    tot = functools.reduce(lambda a, c: a + c, ex)
    return [e / tot for e in ex]


def _group_of(seq, seq_starts):
    out = []
    for gi, lo in enumerate(seq_starts[:-1]):
        hi = seq_starts[gi + 1]
        out.append(((seq >= lo) & (seq < hi), jnp.clip(seq - lo, 0, hi - lo - 1)))
    return out


def _select_group(seq, seq_starts, values):
    picked = values[-1]
    for (active, _), v in list(zip(_group_of(seq, seq_starts), values))[-2::-1]:
        picked = jnp.where(active, v, picked)
    return picked


def _route_kernel(seq_starts, *refs):
    lg_refs = refs[:len(seq_starts) - 1]
    idx_ref, wt_ref, cnt_ref, run_ref = refs[len(seq_starts) - 1:]

    @pl.when((pl.program_id(0) == 0) & (pl.program_id(1) == 0))
    def _():
        run_ref[...] = jnp.zeros_like(run_ref)

    tl = idx_ref.shape[2]
    lg = _select_group(pl.program_id(0), seq_starts, [r[0] for r in lg_refs])
    p_grp = _softmax_rows([lg[g:g + 1, :] for g in range(N_GROUPS)])
    p_g, g_idx = _first_argmax(p_grp)
    el = []
    for j in range(EXPERTS_PER_GROUP):
        sel = lg[N_GROUPS + (N_GROUPS - 1) * EXPERTS_PER_GROUP + j:N_GROUPS + (N_GROUPS - 1) * EXPERTS_PER_GROUP + j + 1, :]
        for g in range(N_GROUPS - 2, -1, -1):
            r = N_GROUPS + g * EXPERTS_PER_GROUP + j
            sel = jnp.where(g_idx == float(g), lg[r:r + 1, :], sel)
        el.append(sel)
    pe = _softmax_rows(el)
    w1, i1 = _first_argmax(pe)
    rest = [jnp.where(i1 == float(j), -1.0, pe[j]) for j in range(EXPERTS_PER_GROUP)]
    w2, i2 = _first_argmax(rest)
    tot = w1 + w2
    wt1 = w1 / tot * p_g
    wt2 = w2 / tot * p_g
    lo = jnp.minimum(i1, i2)
    hi = jnp.maximum(i1, i2)
    pair = jnp.where(hi == 3.0, lo, jnp.where(lo == 1.0, 3.0, jnp.where(hi == 1.0, 4.0, 5.0)))
    slot_a = functools.reduce(lambda acc, p: jnp.where(pair == float(p), float(PAIR_SLOT_A[p]), acc),
                              range(1, N_PAIRS), jnp.full_like(pair, float(PAIR_SLOT_A[0])))
    first_in_a = slot_a == i1
    wa = jnp.where(first_in_a, wt1, wt2)
    wb = jnp.where(first_in_a, wt2, wt1)
    bucket = (g_idx * float(N_PAIRS) + pair).astype(I32)

    brow = lax.broadcasted_iota(I32, (BUCKET_ROWS, tl), 0)
    hit = brow == bucket
    onehot = hit.astype(F32)
    before = lax.broadcasted_iota(I32, (tl, tl), 0) <= lax.broadcasted_iota(I32, (tl, tl), 1)
    incl = jnp.dot(onehot.astype(BF16), before.astype(BF16), preferred_element_type=F32)
    base = run_ref[:, 0:1] + incl - onehot
    rank = jnp.sum(jnp.where(hit, base, 0.0), axis=0, keepdims=True).astype(I32)
    total = run_ref[...] + jnp.sum(onehot, axis=-1, keepdims=True)
    run_ref[...] = total
    cnt_ref[...] = total.astype(I32)

    idx_ref[0] = jnp.concatenate([bucket, rank, jnp.zeros((6, tl), I32)], axis=0)
    wt_ref[0] = jnp.concatenate([wa, wb, jnp.zeros((6, tl), F32)], axis=0)


def _group_spec(block, seq_starts, gi, seq_of, inner_of, place):
    lo, hi = seq_starts[gi], seq_starts[gi + 1]

    def index_map(*ids):
        seq, inner, n_inner = seq_of(*ids), inner_of(*ids), place[1]
        local = jnp.clip(seq - lo, 0, hi - lo - 1)
        inner = jnp.where(seq < lo, 0, jnp.where(seq >= hi, n_inner - 1, inner))
        return place[0](local, inner)

    return pl.BlockSpec(block, index_map)


def _route(logits_list, seq_starts):
    b = seq_starts[-1]
    t = logits_list[0].shape[2]
    tl = min(TL_ROUTE, t)
    nt = t // tl
    blk = lambda rows: pl.BlockSpec((1, rows, tl), lambda bi, i: (bi, 0, i))
    lg_specs = [_group_spec((1, ROUTER_ROWS, tl), seq_starts, gi, lambda bi, i: bi, lambda bi, i: i,
                            (lambda s, i: (s, 0, i), nt)) for gi in range(len(logits_list))]
    return pl.pallas_call(
        functools.partial(_route_kernel, seq_starts),
        grid=(b, nt),
        in_specs=lg_specs,
        out_specs=[blk(8), blk(8), pl.BlockSpec((BUCKET_ROWS, LANES), lambda bi, i: (0, 0))],
        out_shape=[jax.ShapeDtypeStruct((b, 8, t), I32), jax.ShapeDtypeStruct((b, 8, t), F32),
                   jax.ShapeDtypeStruct((BUCKET_ROWS, LANES), I32)],
        scratch_shapes=[pltpu.VMEM((BUCKET_ROWS, LANES), F32)],
        compiler_params=_params(("arbitrary", "arbitrary"), 32),
        name="route",
    )(*logits_list)


def _row_copy(src_ref, src_row, dst_ref, dst_row, sem):
    return pltpu.make_async_copy(src_ref.at[pl.ds(src_row, 1), :], dst_ref.at[pl.ds(dst_row, 1), :], sem)


def _dispatch_kernel(seq_starts, tiles_per_seq, padlo_ref, padhi_ref, dest_ref, *refs):
    x1_refs = refs[:len(seq_starts) - 1]
    wtail_ref, n2_ref, mod_ref, xs_ref, h_ref, zero_ref, sem = refs[len(seq_starts) - 1:]
    _, td, d = x1_refs[0].shape
    i = pl.program_id(0)
    seq = i // tiles_per_seq
    slot = i % 2
    rows = h_ref.at[slot]
    for c0 in range(0, td, ROW_CHUNK):
        sl = slice(c0, c0 + ROW_CHUNK)
        x1 = _select_group(seq, seq_starts, [r[0, sl, :] for r in x1_refs])
        rows[sl, 0:d] = _moe_input(x1, n2_ref, mod_ref)
        rows[sl, d:] = wtail_ref[sl, :]
        for t in range(c0, c0 + ROW_CHUNK):
            _row_copy(rows, t, xs_ref, dest_ref[0, 0, t], sem.at[slot]).start()

    def drain(which):
        pltpu.make_async_copy(h_ref.at[which], xs_ref.at[pl.ds(0, td), :], sem.at[which]).wait()

    @pl.when(i > 0)
    def _():
        drain(1 - slot)

    @pl.when(i == seq_starts[-1] * tiles_per_seq - 1)
    def _():
        drain(slot)
        zero_ref[...] = jnp.zeros_like(zero_ref)
        zsem = sem.at[0]

        sub = zero_ref.shape[0]

        def one_row(r):
            return _row_copy(zero_ref, 0, xs_ref, r, zsem)

        def aligned_rows(r):
            return pltpu.make_async_copy(zero_ref, xs_ref.at[pl.ds(pl.multiple_of(r, sub), sub), :], zsem)

        def spans(e):
            lo, hi = padlo_ref[e], padhi_ref[e]
            mid = jnp.minimum(hi, (lo + sub - 1) // sub * sub)
            return lo, mid, (hi - mid) // sub

        for wait in (False, True):
            for e in range(N_BUCKETS):
                lo, mid, nblk = spans(e)

                def head(r, carry):
                    one_row(r).wait() if wait else one_row(r).start()
                    return carry

                def body(c, carry):
                    cp = aligned_rows(mid + c * sub)
                    cp.wait() if wait else cp.start()
                    return carry

                lax.fori_loop(lo, mid, head, 0)
                lax.fori_loop(0, nblk, body, 0)


def _dispatch(x1_list, seq_starts, dest, w_tail, norm2_w, mod, pad_lo, pad_hi, m_pad):
    d = x1_list[0].shape[2]
    b, _, t = dest.shape
    n = b * t
    td = min(TD_DISPATCH, t)
    per = t // td
    width = d + LANES
    x1_specs = [_group_spec((1, td, d), seq_starts, gi, lambda i, *_: i // per, lambda i, *_: i % per,
                            (lambda s, j: (s, j, 0), per)) for gi in range(len(x1_list))]
    return pl.pallas_call(
        functools.partial(_dispatch_kernel, seq_starts, per),
        grid_spec=pltpu.PrefetchScalarGridSpec(
            num_scalar_prefetch=2,
            grid=(n // td,),
            in_specs=[pl.BlockSpec((1, 1, td), lambda i, *_: (i // per, 0, i % per), memory_space=pltpu.SMEM),
                      *x1_specs,
                      pl.BlockSpec((td, LANES), lambda i, *_: (i, 0)),
                      pl.BlockSpec((1, d), lambda i, *_: (0, 0)),
                      pl.BlockSpec((1, 6, d), lambda i, *_: (i // per, 0, 0))],
            out_specs=pl.BlockSpec(memory_space=pl.ANY),
            scratch_shapes=[pltpu.VMEM((2, td, width), F32), pltpu.VMEM((8, width), F32),
                            pltpu.SemaphoreType.DMA((2,))]),
        out_shape=jax.ShapeDtypeStruct((m_pad, width), F32),
        compiler_params=_params(("arbitrary",), 32),
        name="dispatch",
    )(pad_lo, pad_hi, dest, *x1_list, w_tail, norm2_w, mod)


def _moe_kernel(ea_ref, eb_ref, nused_ref, x_ref, wga_ref, wua_ref, wda_ref, wgb_ref, wub_ref, wdb_ref, o_ref):
    r = pl.program_id(0)
    d = o_ref.shape[1]

    @pl.when(r < nused_ref[0])
    def _():
        xb = x_ref[:, 0:d].astype(BF16)

        def expert(wg_ref, wu_ref, wd_ref):
            g = jnp.dot(xb, wg_ref[0], preferred_element_type=F32)
            u = jnp.dot(xb, wu_ref[0], preferred_element_type=F32)
            h = (g * jax.nn.sigmoid(g) * u).astype(BF16)
            return jnp.dot(h, wd_ref[0], preferred_element_type=F32)

        tail = x_ref[:, d:]
        o_ref[...] = (expert(wga_ref, wua_ref, wda_ref) * _lane_pick(tail, 0)
                      + expert(wgb_ref, wub_ref, wdb_ref) * _lane_pick(tail, 1))

    @pl.when(r >= nused_ref[0])
    def _():
        o_ref[...] = jnp.zeros_like(o_ref)


def _moe_experts(xs, blk_ea, blk_eb, nused, w_gate, w_up, w_down):
    m_pad, width = xs.shape
    d = width - LANES
    f = w_gate.shape[2]
    tm = TM_MOE
    last = lambda r, nu: jnp.minimum(r, nu[0] - 1)
    slot = lambda shape, pick, bufs: pl.BlockSpec(shape, lambda r, ea, eb, nu: (pick(ea, eb)[last(r, nu)], 0, 0),
                                                  pipeline_mode=pl.Buffered(bufs))
    slot_a = lambda shape: slot(shape, lambda ea, eb: ea, 2)
    slot_b = lambda shape: slot(shape, lambda ea, eb: eb, 1)
    return pl.pallas_call(
        _moe_kernel,
        grid_spec=pltpu.PrefetchScalarGridSpec(
            num_scalar_prefetch=3,
            grid=(m_pad // tm,),
            in_specs=[pl.BlockSpec((tm, width), lambda r, ea, eb, nu: (last(r, nu), 0)),
                      slot_a((1, d, f)), slot_a((1, d, f)), slot_a((1, f, d)),
                      slot_b((1, d, f)), slot_b((1, d, f)), slot_b((1, f, d))],
            out_specs=pl.BlockSpec((tm, d), lambda r, ea, eb, nu: (r, 0))),
        out_shape=jax.ShapeDtypeStruct((m_pad, d), F32),
        compiler_params=_params(("arbitrary",), 58),
        name="moe_experts",
    )(blk_ea, blk_eb, nused, xs, w_gate, w_up, w_down, w_gate, w_up, w_down)


def _combine_kernel(n_steps, dest_ref, dnext_ref, x1_ref, mod_ref, o_ref, y_ref, g_ref, sem):
    tc = x1_ref.shape[0]
    i = pl.program_id(0)
    slot = i % 2

    def drain(which):
        pltpu.make_async_copy(o_ref.at[pl.ds(0, tc), :], g_ref.at[which], sem.at[which]).wait()

    @pl.when(i == 0)
    def _():
        def start(t, carry):
            _row_copy(o_ref, dest_ref[0, 0, t], g_ref.at[slot], t, sem.at[slot]).start()
            return carry

        lax.fori_loop(0, tc, start, 0, unroll=DMA_UNROLL)

    drain(slot)
    rows = g_ref.at[slot]
    for c0 in range(0, tc, ROW_CHUNK):
        sl = slice(c0, c0 + ROW_CHUNK)
        y_ref[sl, :] = x1_ref[sl, :] + mod_ref[0, 5:6, :] * rows[sl, :]
        for t in range(c0, c0 + ROW_CHUNK):
            _row_copy(o_ref, dnext_ref[0, 0, t], g_ref.at[1 - slot], t, sem.at[1 - slot]).start()

    @pl.when(i == n_steps - 1)
    def _():
        drain(1 - slot)


def _combine(x1, dest, mod, o_rows):
    nseq, t, d = x1.shape
    x1 = x1.reshape(nseq * t, d)
    tc = min(TC_COMBINE, t)
    per = t // tc
    ntiles = nseq * per
    tile = lambda g: (g // per, 0, g % per)
    return pl.pallas_call(
        functools.partial(_combine_kernel, ntiles),
        grid=(ntiles,),
        in_specs=[pl.BlockSpec((1, 1, tc), tile, memory_space=pltpu.SMEM),
                  pl.BlockSpec((1, 1, tc), lambda i: tile(jnp.minimum(i + 1, ntiles - 1)), memory_space=pltpu.SMEM),
                  pl.BlockSpec((tc, d), lambda i: (i, 0)),
                  pl.BlockSpec((1, 6, d), lambda i: (i // per, 0, 0)),
                  pl.BlockSpec(memory_space=pl.ANY)],
        out_specs=pl.BlockSpec((tc, d), lambda i: (i, 0)),
        scratch_shapes=[pltpu.VMEM((2, tc, d), F32), pltpu.SemaphoreType.DMA((2,))],
        out_shape=jax.ShapeDtypeStruct((nseq * t, d), F32),
        compiler_params=_params(("arbitrary",), 32),
        name="combine",
    )(dest, dest, x1, mod, o_rows)


def _rope_tables(t):
    rows = t // GRID_W
    row = jnp.repeat(jnp.arange(rows, dtype=F32), GRID_W)
    col = jnp.tile(jnp.arange(GRID_W, dtype=F32), rows)
    freqs = ROPE_THETA ** (-jnp.arange(ROPE_PAIRS, dtype=F32) / ROPE_PAIRS)
    ar = row[:, None] * freqs
    ac = col[:, None] * freqs
    cos_t = jnp.concatenate([jnp.cos(ar), jnp.cos(ar), jnp.cos(ac), jnp.cos(ac)], axis=-1)
    sin_t = jnp.concatenate([-jnp.sin(ar), jnp.sin(ar), -jnp.sin(ac), jnp.sin(ac)], axis=-1)
    return cos_t, sin_t


def _token_mixing(x, mod, p):
    b, t, d = x.shape
    cos_t, sin_t = _rope_tables(t)
    aq, ak, av_t, mq, mk_t, mv, mo, mg = _in_projection(
        x, mod, p["norm1_w"], p["w_main"], p["w_v_t"], p["w_gates"], p["q_norm_w"], p["k_norm_w"], cos_t, sin_t)
    att = _attention(aq, ak, av_t)
    mg_t = jnp.swapaxes(mg[:, :, :C_MG], 1, 2)
    h_fw, h_bw = _mlstm(mq, mk_t, mv, mg, mg_t, p["gate_bias_col"], p["gate_bias_row"])
    return _mix_and_router(x, att, h_fw, h_bw, mo, p["ml_norm_w"], p["w_out"], mod,
                           p["norm2_w"], p["wr_hi"], p["wr_lo"], p["b_router"])


def _channel_mixing(x1_list, logits_list, mod, p):
    seq_starts = tuple(int(v) for v in np.cumsum([0] + [x1.shape[0] for x1 in x1_list]))
    b = seq_starts[-1]
    _, t, d = x1_list[0].shape
    n = b * t
    idx, wts, counts = _route(logits_list, seq_starts)

    tm = TM_MOE
    counts = counts[:N_BUCKETS, 0]
    padded = (counts + tm - 1) // tm * tm
    pend = jnp.cumsum(padded)
    pstart = (pend - padded).astype(I32)
    nb = (n + tm - 1) // tm + N_BUCKETS
    m_pad = nb * tm
    block_row0 = jnp.arange(nb, dtype=I32) * tm
    blk_bucket = jnp.minimum(jnp.sum(pend[None, :] <= block_row0[:, None], axis=1), N_BUCKETS - 1)
    group_base = np.repeat(np.arange(N_GROUPS) * EXPERTS_PER_GROUP, N_PAIRS)
    expert_a = jnp.asarray(group_base + np.tile(PAIR_SLOT_A, N_GROUPS), I32)
    expert_b = jnp.asarray(group_base + np.tile(PAIR_SLOT_B, N_GROUPS), I32)
    blk_onehot = blk_bucket[:, None] == jnp.arange(N_BUCKETS)
    blk_ea = jnp.sum(jnp.where(blk_onehot, expert_a, 0), axis=1).astype(I32)
    blk_eb = jnp.sum(jnp.where(blk_onehot, expert_b, 0), axis=1).astype(I32)
    nused = (pend[-1:] // tm).astype(I32)
    onehot = idx[:, 0:1, :, None] == jnp.arange(N_BUCKETS, dtype=I32)
    dest = jnp.sum(jnp.where(onehot, pstart, 0), axis=-1) + idx[:, 1:2, :]

    pad_lo = (pstart + counts).astype(I32)
    pad_hi = jnp.concatenate([pstart[1:], jnp.full((1,), m_pad, I32)])
    w_tail = jnp.zeros((n, LANES), F32).at[:, :2].set(jnp.swapaxes(wts[:, :2, :], 1, 2).reshape(n, 2))
    xs = _dispatch(x1_list, seq_starts, dest, w_tail, p["norm2_w"], mod, pad_lo, pad_hi, m_pad)
    o_rows = _moe_experts(xs, blk_ea, blk_eb, nused, p["w_gate"], p["w_up"], p["w_down"])
    return [_combine(x1, dest[s0:s0 + x1.shape[0]], mod[s0:s0 + x1.shape[0]], o_rows).reshape(x1.shape)
            for x1, s0 in zip(x1_list, seq_starts)]


def kernel(x_prompt, x_sample, c_prompt, c_sample, norm1_w, norm2_w, w_ada, b_ada, w_in, q_norm_w, k_norm_w, b_igate, b_fgate, ml_norm_w, w_out, w_gr, b_gr, w_er, b_er, w_gate, w_up, w_down):
    assert x_prompt.shape[1:] == x_sample.shape[1:], "the request groups share one token buffer per sequence length"
    depth = norm1_w.shape[0]
    d = x_prompt.shape[-1]
    bp = x_prompt.shape[0]
    bs = x_sample.shape[0]
    rows = -(-(bp + bs) // 8) * 8
    y_prompt, y_sample = x_prompt, x_sample
    for l in range(depth):
        c_pad = jnp.zeros((rows, d), F32).at[:bp].set(c_prompt).at[bp:bp + bs].set(c_sample)
        mod = _ada_modulation(c_pad, w_ada[l], b_ada[l]).reshape(rows, 6, d)
        gate_bias = jnp.concatenate([b_igate[l].reshape(-1), b_fgate[l].reshape(-1)])
        w_router = jnp.concatenate([w_gr[l], w_er[l]], axis=1).T
        w_router = jnp.zeros((ROUTER_ROWS, d), F32).at[:N_GROUPS + N_EXPERTS].set(w_router)
        wr_hi = w_router.astype(BF16)
        b_router = jnp.zeros((ROUTER_ROWS,), F32).at[:N_GROUPS + N_EXPERTS].set(jnp.concatenate([b_gr[l], b_er[l]]))
        p = {
            "norm1_w": norm1_w[l].reshape(1, d),
            "norm2_w": norm2_w[l].reshape(1, d),
            "w_main": w_in[l][:, :C_MAIN].astype(BF16),
            "w_v_t": jnp.concatenate([w_in[l][:, C_AQ + C_AK:C_AQ + C_AK + C_AV],
                                      w_in[l][:, C_AQ + C_AK + C_AV + C_MQ:C_AQ + C_AK + C_AV + C_MQ + C_MK]],
                                     axis=1).T.astype(BF16),
            "w_gates": jnp.zeros((d, LANES), BF16).at[:, :C_MG].set(w_in[l][:, C_MAIN:].astype(BF16)),
            "q_norm_w": q_norm_w[l].reshape(1, HEAD_DIM),
            "k_norm_w": k_norm_w[l].reshape(1, HEAD_DIM),
            "gate_bias_col": jnp.zeros((1, LANES), F32).at[0, :C_MG].set(gate_bias),
            "gate_bias_row": jnp.broadcast_to(gate_bias[:, None], (C_MG, ML_CHUNK)),
            "ml_norm_w": ml_norm_w[l].reshape(1, C_MV),
            "w_out": w_out[l].astype(BF16),
            "wr_hi": wr_hi,
            "wr_lo": (w_router - wr_hi.astype(F32)).astype(BF16),
            "b_router": jnp.broadcast_to(b_router[:, None], (ROUTER_ROWS, LANES)),
            "w_gate": w_gate[l].astype(BF16),
            "w_up": w_up[l].astype(BF16),
            "w_down": w_down[l].astype(BF16),
        }
        x1_p, logits_p = _token_mixing(y_prompt, mod[:bp], p)
        x1_s, logits_s = _token_mixing(y_sample, mod[bp:bp + bs], p)
        y_prompt, y_sample = _channel_mixing([x1_p, x1_s], [logits_p, logits_s], mod, p)
    return (y_prompt, y_sample)
```

```python
import functools

import jax
import jax.numpy as jnp
import numpy as np
from jax import lax
from jax.experimental import pallas as pl
from jax.experimental.pallas import tpu as pltpu

F32 = jnp.float32
BF16 = jnp.bfloat16
I32 = jnp.int32

GRID_W = 64
HEAD_DIM = 128
ATT_HEADS = 8
ATT_KV_HEADS = 2
ATT_GROUP = ATT_HEADS // ATT_KV_HEADS
ROPE_THETA = 10000.0
ROPE_PAIRS = HEAD_DIM // 4
ML_HEADS = 4
ML_DV = 256
ML_DQK = 128
ML_CHUNK = 128
N_GROUPS = 4
EXPERTS_PER_GROUP = 4
N_EXPERTS = N_GROUPS * EXPERTS_PER_GROUP
EPS = 1e-6
Q_SCALE = HEAD_DIM ** -0.5 * float(np.log2(np.e))

C_AQ = ATT_HEADS * HEAD_DIM
C_AK = ATT_KV_HEADS * HEAD_DIM
C_AV = ATT_KV_HEADS * HEAD_DIM
C_MQ = ML_HEADS * ML_DQK
C_MK = ML_HEADS * ML_DQK
C_MV = ML_HEADS * ML_DV
C_MO = ML_HEADS * ML_DV
C_MG = 4 * ML_HEADS
C_MAIN = C_AQ + C_AK + C_AV + C_MQ + C_MK + C_MV + C_MO

N_PAIRS = 6
PAIR_SLOT_A = (0, 1, 2, 2, 0, 0)
PAIR_SLOT_B = (3, 3, 3, 1, 1, 2)
N_BUCKETS = N_GROUPS * N_PAIRS
BUCKET_ROWS = 32

LANES = 128
ROUTER_ROWS = 32
MIB = 1024 * 1024

TM_PROJ = 256
TQ_ATT = 256
TM_MIX = 512
TL_ROUTE = 512
TD_DISPATCH = 512
TM_MOE = 256
TC_COMBINE = 512
ML_SEQS = 4
ATT_SOFTMAX_LAG = 2
ATT_VALUES_LAG = 1
ATT_SLOTS = 4
ROW_CHUNK = 32
DMA_UNROLL = 8


def _params(semantics, vmem_mib):
    return pltpu.CompilerParams(dimension_semantics=semantics, vmem_limit_bytes=vmem_mib * MIB)


def _resident(shape):
    nd = len(shape)
    return pl.BlockSpec(shape, lambda *_: (0,) * nd, pipeline_mode=pl.Buffered(1))


def _ada_kernel(c_ref, w_ref, b_ref, o_ref):
    c = c_ref[...]
    s = (c * jax.nn.sigmoid(c)).astype(BF16)
    o_ref[...] = jnp.dot(s, w_ref[...].astype(BF16), preferred_element_type=F32) + b_ref[...]


def _ada_modulation(c_pad, w_ada, b_ada):
    rows, d = c_pad.shape
    n = w_ada.shape[1]
    tn = 1024
    return pl.pallas_call(
        _ada_kernel,
        grid=(n // tn,),
        in_specs=[pl.BlockSpec((rows, d), lambda j: (0, 0)),
                  pl.BlockSpec((d, tn), lambda j: (0, j)),
                  pl.BlockSpec((1, tn), lambda j: (0, j))],
        out_specs=pl.BlockSpec((rows, tn), lambda j: (0, j)),
        out_shape=jax.ShapeDtypeStruct((rows, n), F32),
        compiler_params=_params(("arbitrary",), 40),
        name="ada_modulation",
    )(c_pad, w_ada, b_ada.reshape(1, n))


def _rms(x, w):
    return x * lax.rsqrt(jnp.mean(x * x, axis=-1, keepdims=True) + EPS) * w


def _inproj_kernel(x_ref, mod_ref, n1_ref, w_ref, wvt_ref, wg_ref, qn_ref, kn_ref, cos_ref, sin_ref,
                   aq_ref, ak_ref, avt_ref, mq_ref, mkt_ref, mv_ref, mo_ref, mg_ref):
    x = x_ref[0]
    h = _rms(x, n1_ref[...]) * (1.0 + mod_ref[0, 1:2, :]) + mod_ref[0, 0:1, :]
    hb = h.astype(BF16)

    def proj(c0, width):
        return jnp.dot(hb, w_ref[:, c0:c0 + width], preferred_element_type=F32)

    cos = cos_ref[...]
    sin = sin_ref[...]
    lane = lax.broadcasted_iota(I32, (1, HEAD_DIM), 1)
    first = (lane % (2 * ROPE_PAIRS)) < ROPE_PAIRS

    def norm_rope(p, w):
        pn = _rms(p, w)
        partner = jnp.where(first, pltpu.roll(pn, HEAD_DIM - ROPE_PAIRS, 1), pltpu.roll(pn, ROPE_PAIRS, 1))
        return pn * cos + partner * sin

    c0 = 0
    for half in range(2):
        p = proj(c0, C_AQ // 2)
        for hh in range(ATT_HEADS // 2):
            col = half * (C_AQ // 2) + hh * HEAD_DIM
            qh = norm_rope(p[:, hh * HEAD_DIM:(hh + 1) * HEAD_DIM], qn_ref[...])
            aq_ref[0, :, col:col + HEAD_DIM] = (qh * Q_SCALE).astype(BF16)
        c0 += C_AQ // 2
    p = proj(c0, C_AK)
    for hh in range(ATT_KV_HEADS):
        ak_ref[0, :, hh * HEAD_DIM:(hh + 1) * HEAD_DIM] = norm_rope(p[:, hh * HEAD_DIM:(hh + 1) * HEAD_DIM], kn_ref[...]).astype(BF16)
    nt = (((1,), (1,)), ((), ()))
    avt_ref[0] = lax.dot_general(wvt_ref[0:C_AV, :], hb, nt, preferred_element_type=F32).astype(BF16)
    mkt_ref[0] = lax.dot_general(wvt_ref[C_AV:, :], hb, nt, preferred_element_type=F32).astype(BF16)
    c0 += C_AK + C_AV
    mq_ref[0] = (proj(c0, C_MQ) * (ML_DQK ** -0.5)).astype(BF16)
    c0 += C_MQ + C_MK
    for half in range(2):
        mv_ref[0, :, half * 512:(half + 1) * 512] = proj(c0, 512).astype(BF16)
        c0 += 512
    for half in range(2):
        mo_ref[0, :, half * 512:(half + 1) * 512] = proj(c0, 512)
        c0 += 512
    mg_ref[0] = jnp.dot(hb, wg_ref[...], preferred_element_type=F32)


def _in_projection(x, mod, norm1_w, w_main, w_v_t, w_gates, q_norm_w, k_norm_w, cos_t, sin_t):
    b, t, d = x.shape
    tm = min(TM_PROJ, t)
    row = lambda width: pl.BlockSpec((1, tm, width), lambda bi, i: (bi, i, 0))
    outs = ((C_AQ, False), (C_AK, False), (C_AV, True), (C_MQ, False), (C_MK, True), (C_MV, False),
            (C_MO, False), (LANES, False))
    dtypes = (BF16, BF16, BF16, BF16, BF16, BF16, F32, F32)
    col = lambda width: pl.BlockSpec((1, width, tm), lambda bi, i: (bi, 0, i))
    return pl.pallas_call(
        _inproj_kernel,
        grid=(b, t // tm),
        in_specs=[row(d),
                  pl.BlockSpec((1, 6, d), lambda bi, i: (bi, 0, 0)),
                  _resident((1, d)),
                  _resident((d, C_MAIN)),
                  _resident((C_AV + C_MK, d)),
                  _resident((d, LANES)),
                  _resident((1, HEAD_DIM)),
                  _resident((1, HEAD_DIM)),
                  pl.BlockSpec((tm, HEAD_DIM), lambda bi, i: (i, 0)),
                  pl.BlockSpec((tm, HEAD_DIM), lambda bi, i: (i, 0))],
        out_specs=[col(w) if tr else row(w) for w, tr in outs],
        out_shape=[jax.ShapeDtypeStruct((b, w, t) if tr else (b, t, w), dt) for (w, tr), dt in zip(outs, dtypes)],
        compiler_params=_params(("parallel", "arbitrary"), 48),
        name="in_projection",
    )(x, mod, norm1_w, w_main, w_v_t, w_gates, q_norm_w, k_norm_w, cos_t, sin_t)


def _attention_kernel(q_ref, qn_ref, k_ref, vt_ref, o_ref, s_ref, p_ref, l_ref):
    def scores(g, src_ref=q_ref):
        kv = g // ATT_GROUP
        q = src_ref[0, :, g * HEAD_DIM:(g + 1) * HEAD_DIM]
        k = k_ref[0, :, kv * HEAD_DIM:(kv + 1) * HEAD_DIM]
        s_ref[g % ATT_SLOTS] = lax.dot_general(k, q, (((1,), (1,)), ((), ())), preferred_element_type=F32)

    def softmax(g):
        s = s_ref[g % ATT_SLOTS]
        m = jnp.max(s, axis=0, keepdims=True)
        p = jnp.exp2(s - m)
        l_ref[g] = jnp.broadcast_to(jnp.sum(p, axis=0, keepdims=True), l_ref.shape[1:])
        p_ref[g % ATT_SLOTS] = p.astype(BF16)

    def values(g):
        kv = g // ATT_GROUP
        vt = vt_ref[0, kv * HEAD_DIM:(kv + 1) * HEAD_DIM, :]
        o_t = jnp.dot(vt, p_ref[g % ATT_SLOTS], preferred_element_type=F32)
        o_ref[0, :, g * HEAD_DIM:(g + 1) * HEAD_DIM] = (o_t * (1.0 / l_ref[g][0:1, :])).T.astype(BF16)

    n = ATT_HEADS
    lag = ATT_SOFTMAX_LAG

    @pl.when(pl.program_id(1) == 0)
    def _():
        for g in range(lag):
            scores(g)

    for g in range(n):
        if g + lag < n:
            scores(g + lag)
        else:
            scores(g + lag - n, qn_ref)
        softmax(g)
        if g >= ATT_VALUES_LAG:
            values(g - ATT_VALUES_LAG)
    for g in range(n - ATT_VALUES_LAG, n):
        values(g)


def _attention(aq, ak, av_t):
    b, t, _ = aq.shape
    tq = min(TQ_ATT, t)
    nt = t // tq
    return pl.pallas_call(
        _attention_kernel,
        grid=(b, nt),
        in_specs=[pl.BlockSpec((1, tq, C_AQ), lambda bi, i: (bi, i, 0)),
                  pl.BlockSpec((1, tq, C_AQ), lambda bi, i: (bi, jnp.minimum(i + 1, nt - 1), 0)),
                  pl.BlockSpec((1, t, C_AK), lambda bi, i: (bi, 0, 0)),
                  pl.BlockSpec((1, C_AV, t), lambda bi, i: (bi, 0, 0))],
        out_specs=pl.BlockSpec((1, tq, C_AQ), lambda bi, i: (bi, i, 0)),
        out_shape=jax.ShapeDtypeStruct((b, t, C_AQ), BF16),
        scratch_shapes=[pltpu.VMEM((ATT_SLOTS, t, tq), F32), pltpu.VMEM((ATT_SLOTS, t, tq), BF16),
                        pltpu.VMEM((ATT_HEADS, 8, tq), F32)],
        compiler_params=_params(("parallel", "arbitrary"), 48),
        name="attention",
    )(aq, aq, ak, av_t)


def _log_sigmoid(x):
    return jnp.minimum(x, 0.0) - jnp.log1p(jnp.exp(-jnp.abs(x)))


def _split3(a):
    a1 = a.astype(BF16)
    r1 = a - a1.astype(F32)
    a2 = r1.astype(BF16)
    a3 = (r1 - a2.astype(F32)).astype(BF16)
    return a1, a2, a3


def _lane_pick(a, idx):
    lane = lax.broadcasted_iota(I32, a.shape, 1)
    return jnp.sum(jnp.where(lane == idx, a, 0.0), axis=-1, keepdims=True)


def _mlstm_kernel(qf, kf, vf, gcf, grf, qb, kb, vb, gcb, grb, bc_ref, br_ref, hf_ref, hb_ref,
                  ct_ref, m_ref):
    @pl.when(pl.program_id(1) == 0)
    def _():
        ct_ref[...] = jnp.zeros_like(ct_ref)
        m_ref[...] = jnp.zeros_like(m_ref)

    L = ML_CHUNK
    row = lax.broadcasted_iota(I32, (L, L), 0)
    col = lax.broadcasted_iota(I32, (L, L), 1)
    lower = (col <= row)
    upper = (col >= row)
    lower_b = lower.astype(BF16)
    upper_b = upper.astype(BF16)

    gates = {}
    for bb in range(qf.shape[0]):
        for d, (gc_ref, gr_ref) in enumerate(((gcf, grf), (gcb, grb))):
            cum_cols = upper_b if d else lower_b
            cum_rows = lower_b if d else upper_b
            pre_c = gc_ref[bb] + bc_ref[...]
            pre_r = gr_ref[bb] + br_ref[...]
            b_cols = sum(jnp.dot(cum_cols, part, preferred_element_type=F32) for part in _split3(_log_sigmoid(pre_c)))
            b_rows = sum(jnp.dot(part, cum_rows, preferred_element_type=F32) for part in _split3(_log_sigmoid(pre_r)))
            gates[bb, d] = (pre_r, b_cols, b_rows)

    chains = [(bb, d, hh) for bb in range(qf.shape[0]) for d in range(2) for hh in range(ML_HEADS)]
    refs = ((qf, kf, vf, hf_ref), (qb, kb, vb, hb_ref))
    ones_block = jnp.ones((L, LANES), BF16)


    st = {}
    for (bb, d, hh) in chains:
        q_ref, kt_ref, v_ref, _ = refs[d]
        pre_r, b_cols, b_rows = gates[bb, d]
        ci = d * ML_HEADS + hh
        cf = 2 * ML_HEADS + d * ML_HEADS + hh
        sidx = (bb * 2 + d) * ML_HEADS + hh
        i_row = pre_r[ci:ci + 1, :]
        b_row = b_rows[cf:cf + 1, :]
        m = m_ref[sidx][0:1, 0:1]
        c_vis = jnp.where(upper if d else lower, i_row - b_row, -jnp.inf)
        shift = -jnp.maximum(m, jnp.max(c_vis, axis=-1, keepdims=True))
        dec = jnp.exp(c_vis + shift)
        a_inter = jnp.exp(m + shift)
        floor = jnp.exp(shift - _lane_pick(b_cols, cf))
        q = q_ref[bb, :, hh * ML_DQK:(hh + 1) * ML_DQK]
        kt = kt_ref[bb, hh * ML_DQK:(hh + 1) * ML_DQK, :]
        v_aug = jnp.concatenate([v_ref[bb, :, hh * ML_DV:(hh + 1) * ML_DV], ones_block], axis=1)
        qk = jnp.dot(q, kt, preferred_element_type=F32)
        ct = ct_ref[sidx]
        inter = jnp.dot(q, ct.astype(BF16), preferred_element_type=F32)
        b_end = _lane_pick(b_row, 0 if d else L - 1)
        g_row = b_end - b_row + i_row
        m_new = jnp.maximum(b_end + m, jnp.max(g_row, axis=-1, keepdims=True))
        decay = jnp.exp(b_end + m - m_new)
        w_row = jnp.exp(g_row - m_new)
        upd = jnp.dot((kt.astype(F32) * w_row).astype(BF16), v_aug, preferred_element_type=F32)
        st[bb, d, hh] = (sidx, dec, a_inter, floor, v_aug, qk, ct, inter, m_new, decay, upd)

    sv = {}
    for key in chains:
        sidx, dec, a_inter, floor, v_aug, qk, ct, inter, m_new, decay, upd = st[key]
        sv[key] = jnp.dot((qk * dec).astype(BF16), v_aug, preferred_element_type=F32)

    for key in chains:
        bb, d, hh = key
        sidx, dec, a_inter, floor, v_aug, qk, ct, inter, m_new, decay, upd = st[key]
        both = a_inter * inter + sv[key]
        scale = 1.0 / jnp.maximum(jnp.abs(both[:, ML_DV:]), floor)
        h_ref = refs[d][3]
        for blk in range(ML_DV // LANES):
            lo = hh * ML_DV + blk * LANES
            h_ref[bb, :, lo:lo + LANES] = both[:, blk * LANES:(blk + 1) * LANES] * scale
        ct_ref[sidx] = decay * ct + upd
        m_ref[sidx] = jnp.broadcast_to(m_new, (8, LANES))


def _mlstm(mq, mk_t, mv, mg, mg_t, bias_col, bias_row):
    b, t, _ = mq.shape
    L = ML_CHUNK
    nc = t // L
    sb = ML_SEQS if b % ML_SEQS == 0 else 1
    fw = lambda width: pl.BlockSpec((sb, L, width), lambda bi, c: (bi, c, 0))
    bw = lambda width: pl.BlockSpec((sb, L, width), lambda bi, c: (bi, nc - 1 - c, 0))
    fw_t = lambda rows: pl.BlockSpec((sb, rows, L), lambda bi, c: (bi, 0, c))
    bw_t = lambda rows: pl.BlockSpec((sb, rows, L), lambda bi, c: (bi, 0, nc - 1 - c))
    ns = sb * 2 * ML_HEADS
    return pl.pallas_call(
        _mlstm_kernel,
        grid=(b // sb, nc),
        in_specs=[fw(C_MQ), fw_t(C_MK), fw(C_MV), fw(LANES), fw_t(C_MG),
                  bw(C_MQ), bw_t(C_MK), bw(C_MV), bw(LANES), bw_t(C_MG),
                  _resident((1, LANES)), _resident((C_MG, L))],
        out_specs=[fw(C_MV), bw(C_MV)],
        out_shape=[jax.ShapeDtypeStruct((b, t, C_MV), F32)] * 2,
        scratch_shapes=[pltpu.VMEM((ns, ML_DQK, ML_DV + LANES), F32),
                        pltpu.VMEM((ns, 8, LANES), F32)],
        compiler_params=_params(("parallel", "arbitrary"), 48),
        name="mlstm",
    )(mq, mk_t, mv, mg, mg_t, mq, mk_t, mv, mg, mg_t, bias_col, bias_row)


def _moe_input(x1, n2_ref, mod_ref):
    return _rms(x1, n2_ref[...]) * (1.0 + mod_ref[0, 4:5, :]) + mod_ref[0, 3:4, :]


def _mix_kernel(x_ref, att_ref, hf_ref, hb_ref, mo_ref, mlw_ref, wo_ref, mod_ref, n2_ref,
                wrh_ref, wrl_ref, br_ref, x1_ref, lg_ref):
    ml = hf_ref[0] + hb_ref[0]
    gate = jax.nn.sigmoid(mo_ref[0])
    mix = jnp.dot(att_ref[0], wo_ref[0:C_AQ, :], preferred_element_type=F32)
    for hh in range(ML_HEADS):
        sl = slice(hh * ML_DV, (hh + 1) * ML_DV)
        seg = (_rms(ml[:, sl], mlw_ref[:, sl]) * gate[:, sl]).astype(BF16)
        mix += jnp.dot(seg, wo_ref[C_AQ + hh * ML_DV:C_AQ + (hh + 1) * ML_DV, :], preferred_element_type=F32)
    x1 = x_ref[0] + mod_ref[0, 2:3, :] * mix
    x1_ref[0] = x1
    h2 = _moe_input(x1, n2_ref, mod_ref)
    h_hi = h2.astype(BF16)
    h_lo = (h2 - h_hi.astype(F32)).astype(BF16)
    nt = (((1,), (1,)), ((), ()))
    both = lax.dot_general(jnp.concatenate([wrh_ref[...], wrl_ref[...]], axis=0), h_hi, nt, preferred_element_type=F32)
    lg = both[:ROUTER_ROWS] + both[ROUTER_ROWS:] + lax.dot_general(wrh_ref[...], h_lo, nt, preferred_element_type=F32)
    lg_ref[0] = lg + br_ref[:, 0:1]


def _mix_and_router(x, att, h_fw, h_bw, mo, ml_norm_w, w_out, mod, norm2_w, wr_hi, wr_lo, b_router):
    b, t, d = x.shape
    tm = min(TM_MIX, t)
    row = lambda width: pl.BlockSpec((1, tm, width), lambda bi, i: (bi, i, 0))
    return pl.pallas_call(
        _mix_kernel,
        grid=(b, t // tm),
        in_specs=[row(d), row(C_AQ), row(C_MV), row(C_MV), row(C_MO),
                  _resident((1, C_MV)), _resident((d, d)),
                  pl.BlockSpec((1, 6, d), lambda bi, i: (bi, 0, 0)),
                  _resident((1, d)), _resident((ROUTER_ROWS, d)), _resident((ROUTER_ROWS, d)),
                  _resident((ROUTER_ROWS, LANES))],
        out_specs=[row(d), pl.BlockSpec((1, ROUTER_ROWS, tm), lambda bi, i: (bi, 0, i))],
        out_shape=[jax.ShapeDtypeStruct((b, t, d), F32), jax.ShapeDtypeStruct((b, ROUTER_ROWS, t), F32)],
        compiler_params=_params(("parallel", "arbitrary"), 48),
        name="mix_router",
    )(x, att, h_fw, h_bw, mo, ml_norm_w, w_out, mod, norm2_w, wr_hi, wr_lo, b_router)


def _first_argmax(rows):
    best = rows[0]
    idx = jnp.zeros_like(best)
    for j in range(1, len(rows)):
        better = rows[j] > best
        best = jnp.where(better, rows[j], best)
        idx = jnp.where(better, float(j), idx)
    return best, idx


def _softmax_rows(rows):
    mx = functools.reduce(jnp.maximum, rows)
    ex = [jnp.exp(r - mx) for r in rows]
    tot = functools.reduce(lambda a, c: a + c, ex)
    return [e / tot for e in ex]


def _group_of(seq, seq_starts):
    out = []
    for gi, lo in enumerate(seq_starts[:-1]):
        hi = seq_starts[gi + 1]
        out.append(((seq >= lo) & (seq < hi), jnp.clip(seq - lo, 0, hi - lo - 1)))
    return out


def _select_group(seq, seq_starts, values):
    picked = values[-1]
    for (active, _), v in list(zip(_group_of(seq, seq_starts), values))[-2::-1]:
        picked = jnp.where(active, v, picked)
    return picked


def _route_kernel(seq_starts, *refs):
    lg_refs = refs[:len(seq_starts) - 1]
    idx_ref, wt_ref, cnt_ref, run_ref = refs[len(seq_starts) - 1:]

    @pl.when((pl.program_id(0) == 0) & (pl.program_id(1) == 0))
    def _():
        run_ref[...] = jnp.zeros_like(run_ref)

    tl = idx_ref.shape[2]
    lg = _select_group(pl.program_id(0), seq_starts, [r[0] for r in lg_refs])
    p_grp = _softmax_rows([lg[g:g + 1, :] for g in range(N_GROUPS)])
    p_g, g_idx = _first_argmax(p_grp)
    el = []
    for j in range(EXPERTS_PER_GROUP):
        sel = lg[N_GROUPS + (N_GROUPS - 1) * EXPERTS_PER_GROUP + j:N_GROUPS + (N_GROUPS - 1) * EXPERTS_PER_GROUP + j + 1, :]
        for g in range(N_GROUPS - 2, -1, -1):
            r = N_GROUPS + g * EXPERTS_PER_GROUP + j
            sel = jnp.where(g_idx == float(g), lg[r:r + 1, :], sel)
        el.append(sel)
    pe = _softmax_rows(el)
    w1, i1 = _first_argmax(pe)
    rest = [jnp.where(i1 == float(j), -1.0, pe[j]) for j in range(EXPERTS_PER_GROUP)]
    w2, i2 = _first_argmax(rest)
    tot = w1 + w2
    wt1 = w1 / tot * p_g
    wt2 = w2 / tot * p_g
    lo = jnp.minimum(i1, i2)
    hi = jnp.maximum(i1, i2)
    pair = jnp.where(hi == 3.0, lo, jnp.where(lo == 1.0, 3.0, jnp.where(hi == 1.0, 4.0, 5.0)))
    slot_a = functools.reduce(lambda acc, p: jnp.where(pair == float(p), float(PAIR_SLOT_A[p]), acc),
                              range(1, N_PAIRS), jnp.full_like(pair, float(PAIR_SLOT_A[0])))
    first_in_a = slot_a == i1
    wa = jnp.where(first_in_a, wt1, wt2)
    wb = jnp.where(first_in_a, wt2, wt1)
    bucket = (g_idx * float(N_PAIRS) + pair).astype(I32)

    brow = lax.broadcasted_iota(I32, (BUCKET_ROWS, tl), 0)
    hit = brow == bucket
    onehot = hit.astype(F32)
    before = lax.broadcasted_iota(I32, (tl, tl), 0) <= lax.broadcasted_iota(I32, (tl, tl), 1)
    incl = jnp.dot(onehot.astype(BF16), before.astype(BF16), preferred_element_type=F32)
    base = run_ref[:, 0:1] + incl - onehot
    rank = jnp.sum(jnp.where(hit, base, 0.0), axis=0, keepdims=True).astype(I32)
    total = run_ref[...] + jnp.sum(onehot, axis=-1, keepdims=True)
    run_ref[...] = total
    cnt_ref[...] = total.astype(I32)

    idx_ref[0] = jnp.concatenate([bucket, rank, jnp.zeros((6, tl), I32)], axis=0)
    wt_ref[0] = jnp.concatenate([wa, wb, jnp.zeros((6, tl), F32)], axis=0)


def _group_spec(block, seq_starts, gi, seq_of, inner_of, place):
    lo, hi = seq_starts[gi], seq_starts[gi + 1]

    def index_map(*ids):
        seq, inner, n_inner = seq_of(*ids), inner_of(*ids), place[1]
        local = jnp.clip(seq - lo, 0, hi - lo - 1)
        inner = jnp.where(seq < lo, 0, jnp.where(seq >= hi, n_inner - 1, inner))
        return place[0](local, inner)

    return pl.BlockSpec(block, index_map)


def _route(logits_list, seq_starts):
    b = seq_starts[-1]
    t = logits_list[0].shape[2]
    tl = min(TL_ROUTE, t)
    nt = t // tl
    blk = lambda rows: pl.BlockSpec((1, rows, tl), lambda bi, i: (bi, 0, i))
    lg_specs = [_group_spec((1, ROUTER_ROWS, tl), seq_starts, gi, lambda bi, i: bi, lambda bi, i: i,
                            (lambda s, i: (s, 0, i), nt)) for gi in range(len(logits_list))]
    return pl.pallas_call(
        functools.partial(_route_kernel, seq_starts),
        grid=(b, nt),
        in_specs=lg_specs,
        out_specs=[blk(8), blk(8), pl.BlockSpec((BUCKET_ROWS, LANES), lambda bi, i: (0, 0))],
        out_shape=[jax.ShapeDtypeStruct((b, 8, t), I32), jax.ShapeDtypeStruct((b, 8, t), F32),
                   jax.ShapeDtypeStruct((BUCKET_ROWS, LANES), I32)],
        scratch_shapes=[pltpu.VMEM((BUCKET_ROWS, LANES), F32)],
        compiler_params=_params(("arbitrary", "arbitrary"), 32),
        name="route",
    )(*logits_list)


def _row_copy(src_ref, src_row, dst_ref, dst_row, sem):
    return pltpu.make_async_copy(src_ref.at[pl.ds(src_row, 1), :], dst_ref.at[pl.ds(dst_row, 1), :], sem)


def _dispatch_kernel(seq_starts, tiles_per_seq, padlo_ref, padhi_ref, dest_ref, *refs):
    x1_refs = refs[:len(seq_starts) - 1]
    wtail_ref, n2_ref, mod_ref, xs_ref, h_ref, zero_ref, sem = refs[len(seq_starts) - 1:]
    _, td, d = x1_refs[0].shape
    i = pl.program_id(0)
    seq = i // tiles_per_seq
    slot = i % 2
    rows = h_ref.at[slot]
    for c0 in range(0, td, ROW_CHUNK):
        sl = slice(c0, c0 + ROW_CHUNK)
        x1 = _select_group(seq, seq_starts, [r[0, sl, :] for r in x1_refs])
        rows[sl, 0:d] = _moe_input(x1, n2_ref, mod_ref)
        rows[sl, d:] = wtail_ref[sl, :]
        for t in range(c0, c0 + ROW_CHUNK):
            _row_copy(rows, t, xs_ref, dest_ref[0, 0, t], sem.at[slot]).start()

    def drain(which):
        pltpu.make_async_copy(h_ref.at[which], xs_ref.at[pl.ds(0, td), :], sem.at[which]).wait()

    @pl.when(i > 0)
    def _():
        drain(1 - slot)

    @pl.when(i == seq_starts[-1] * tiles_per_seq - 1)
    def _():
        drain(slot)
        zero_ref[...] = jnp.zeros_like(zero_ref)
        zsem = sem.at[0]

        sub = zero_ref.shape[0]

        def one_row(r):
            return _row_copy(zero_ref, 0, xs_ref, r, zsem)

        def aligned_rows(r):
            return pltpu.make_async_copy(zero_ref, xs_ref.at[pl.ds(pl.multiple_of(r, sub), sub), :], zsem)

        def spans(e):
            lo, hi = padlo_ref[e], padhi_ref[e]
            mid = jnp.minimum(hi, (lo + sub - 1) // sub * sub)
            return lo, mid, (hi - mid) // sub

        for wait in (False, True):
            for e in range(N_BUCKETS):
                lo, mid, nblk = spans(e)

                def head(r, carry):
                    one_row(r).wait() if wait else one_row(r).start()
                    return carry

                def body(c, carry):
                    cp = aligned_rows(mid + c * sub)
                    cp.wait() if wait else cp.start()
                    return carry

                lax.fori_loop(lo, mid, head, 0)
                lax.fori_loop(0, nblk, body, 0)


def _dispatch(x1_list, seq_starts, dest, w_tail, norm2_w, mod, pad_lo, pad_hi, m_pad):
    d = x1_list[0].shape[2]
    b, _, t = dest.shape
    n = b * t
    td = min(TD_DISPATCH, t)
    per = t // td
    width = d + LANES
    x1_specs = [_group_spec((1, td, d), seq_starts, gi, lambda i, *_: i // per, lambda i, *_: i % per,
                            (lambda s, j: (s, j, 0), per)) for gi in range(len(x1_list))]
    return pl.pallas_call(
        functools.partial(_dispatch_kernel, seq_starts, per),
        grid_spec=pltpu.PrefetchScalarGridSpec(
            num_scalar_prefetch=2,
            grid=(n // td,),
            in_specs=[pl.BlockSpec((1, 1, td), lambda i, *_: (i // per, 0, i % per), memory_space=pltpu.SMEM),
                      *x1_specs,
                      pl.BlockSpec((td, LANES), lambda i, *_: (i, 0)),
                      pl.BlockSpec((1, d), lambda i, *_: (0, 0)),
                      pl.BlockSpec((1, 6, d), lambda i, *_: (i // per, 0, 0))],
            out_specs=pl.BlockSpec(memory_space=pl.ANY),
            scratch_shapes=[pltpu.VMEM((2, td, width), F32), pltpu.VMEM((8, width), F32),
                            pltpu.SemaphoreType.DMA((2,))]),
        out_shape=jax.ShapeDtypeStruct((m_pad, width), F32),
        compiler_params=_params(("arbitrary",), 32),
        name="dispatch",
    )(pad_lo, pad_hi, dest, *x1_list, w_tail, norm2_w, mod)


def _moe_kernel(ea_ref, eb_ref, nused_ref, x_ref, wga_ref, wua_ref, wda_ref, wgb_ref, wub_ref, wdb_ref, o_ref):
    r = pl.program_id(0)
    d = o_ref.shape[1]

    @pl.when(r < nused_ref[0])
    def _():
        xb = x_ref[:, 0:d].astype(BF16)

        def expert(wg_ref, wu_ref, wd_ref):
            g = jnp.dot(xb, wg_ref[0], preferred_element_type=F32)
            u = jnp.dot(xb, wu_ref[0], preferred_element_type=F32)
            h = (g * jax.nn.sigmoid(g) * u).astype(BF16)
            return jnp.dot(h, wd_ref[0], preferred_element_type=F32)

        tail = x_ref[:, d:]
        o_ref[...] = (expert(wga_ref, wua_ref, wda_ref) * _lane_pick(tail, 0)
                      + expert(wgb_ref, wub_ref, wdb_ref) * _lane_pick(tail, 1))

    @pl.when(r >= nused_ref[0])
    def _():
        o_ref[...] = jnp.zeros_like(o_ref)


def _moe_experts(xs, blk_ea, blk_eb, nused, w_gate, w_up, w_down):
    m_pad, width = xs.shape
    d = width - LANES
    f = w_gate.shape[2]
    tm = TM_MOE
    last = lambda r, nu: jnp.minimum(r, nu[0] - 1)
    slot = lambda shape, pick, bufs: pl.BlockSpec(shape, lambda r, ea, eb, nu: (pick(ea, eb)[last(r, nu)], 0, 0),
                                                  pipeline_mode=pl.Buffered(bufs))
    slot_a = lambda shape: slot(shape, lambda ea, eb: ea, 2)
    slot_b = lambda shape: slot(shape, lambda ea, eb: eb, 1)
    return pl.pallas_call(
        _moe_kernel,
        grid_spec=pltpu.PrefetchScalarGridSpec(
            num_scalar_prefetch=3,
            grid=(m_pad // tm,),
            in_specs=[pl.BlockSpec((tm, width), lambda r, ea, eb, nu: (last(r, nu), 0)),
                      slot_a((1, d, f)), slot_a((1, d, f)), slot_a((1, f, d)),
                      slot_b((1, d, f)), slot_b((1, d, f)), slot_b((1, f, d))],
            out_specs=pl.BlockSpec((tm, d), lambda r, ea, eb, nu: (r, 0))),
        out_shape=jax.ShapeDtypeStruct((m_pad, d), F32),
        compiler_params=_params(("arbitrary",), 58),
        name="moe_experts",
    )(blk_ea, blk_eb, nused, xs, w_gate, w_up, w_down, w_gate, w_up, w_down)


def _combine_kernel(n_steps, dest_ref, dnext_ref, x1_ref, mod_ref, o_ref, y_ref, g_ref, sem):
    tc = x1_ref.shape[0]
    i = pl.program_id(0)
    slot = i % 2

    def drain(which):
        pltpu.make_async_copy(o_ref.at[pl.ds(0, tc), :], g_ref.at[which], sem.at[which]).wait()

    @pl.when(i == 0)
    def _():
        def start(t, carry):
            _row_copy(o_ref, dest_ref[0, 0, t], g_ref.at[slot], t, sem.at[slot]).start()
            return carry

        lax.fori_loop(0, tc, start, 0, unroll=DMA_UNROLL)

    drain(slot)
    rows = g_ref.at[slot]
    for c0 in range(0, tc, ROW_CHUNK):
        sl = slice(c0, c0 + ROW_CHUNK)
        y_ref[sl, :] = x1_ref[sl, :] + mod_ref[0, 5:6, :] * rows[sl, :]
        for t in range(c0, c0 + ROW_CHUNK):
            _row_copy(o_ref, dnext_ref[0, 0, t], g_ref.at[1 - slot], t, sem.at[1 - slot]).start()

    @pl.when(i == n_steps - 1)
    def _():
        drain(1 - slot)


def _combine(x1, dest, mod, o_rows):
    nseq, t, d = x1.shape
    x1 = x1.reshape(nseq * t, d)
    tc = min(TC_COMBINE, t)
    per = t // tc
    ntiles = nseq * per
    tile = lambda g: (g // per, 0, g % per)
    return pl.pallas_call(
        functools.partial(_combine_kernel, ntiles),
        grid=(ntiles,),
        in_specs=[pl.BlockSpec((1, 1, tc), tile, memory_space=pltpu.SMEM),
                  pl.BlockSpec((1, 1, tc), lambda i: tile(jnp.minimum(i + 1, ntiles - 1)), memory_space=pltpu.SMEM),
                  pl.BlockSpec((tc, d), lambda i: (i, 0)),
                  pl.BlockSpec((1, 6, d), lambda i: (i // per, 0, 0)),
                  pl.BlockSpec(memory_space=pl.ANY)],
        out_specs=pl.BlockSpec((tc, d), lambda i: (i, 0)),
        scratch_shapes=[pltpu.VMEM((2, tc, d), F32), pltpu.SemaphoreType.DMA((2,))],
        out_shape=jax.ShapeDtypeStruct((nseq * t, d), F32),
        compiler_params=_params(("arbitrary",), 32),
        name="combine",
    )(dest, dest, x1, mod, o_rows)


def _rope_tables(t):
    rows = t // GRID_W
    row = jnp.repeat(jnp.arange(rows, dtype=F32), GRID_W)
    col = jnp.tile(jnp.arange(GRID_W, dtype=F32), rows)
    freqs = ROPE_THETA ** (-jnp.arange(ROPE_PAIRS, dtype=F32) / ROPE_PAIRS)
    ar = row[:, None] * freqs
    ac = col[:, None] * freqs
    cos_t = jnp.concatenate([jnp.cos(ar), jnp.cos(ar), jnp.cos(ac), jnp.cos(ac)], axis=-1)
    sin_t = jnp.concatenate([-jnp.sin(ar), jnp.sin(ar), -jnp.sin(ac), jnp.sin(ac)], axis=-1)
    return cos_t, sin_t


def _token_mixing(x, mod, p):
    b, t, d = x.shape
    cos_t, sin_t = _rope_tables(t)
    aq, ak, av_t, mq, mk_t, mv, mo, mg = _in_projection(
        x, mod, p["norm1_w"], p["w_main"], p["w_v_t"], p["w_gates"], p["q_norm_w"], p["k_norm_w"], cos_t, sin_t)
    att = _attention(aq, ak, av_t)
    mg_t = jnp.swapaxes(mg[:, :, :C_MG], 1, 2)
    h_fw, h_bw = _mlstm(mq, mk_t, mv, mg, mg_t, p["gate_bias_col"], p["gate_bias_row"])
    return _mix_and_router(x, att, h_fw, h_bw, mo, p["ml_norm_w"], p["w_out"], mod,
                           p["norm2_w"], p["wr_hi"], p["wr_lo"], p["b_router"])


def _channel_mixing(x1_list, logits_list, mod, p):
    seq_starts = tuple(int(v) for v in np.cumsum([0] + [x1.shape[0] for x1 in x1_list]))
    b = seq_starts[-1]
    _, t, d = x1_list[0].shape
    n = b * t
    idx, wts, counts = _route(logits_list, seq_starts)

    tm = TM_MOE
    counts = counts[:N_BUCKETS, 0]
    padded = (counts + tm - 1) // tm * tm
    pend = jnp.cumsum(padded)
    pstart = (pend - padded).astype(I32)
    nb = (n + tm - 1) // tm + N_BUCKETS
    m_pad = nb * tm
    block_row0 = jnp.arange(nb, dtype=I32) * tm
    blk_bucket = jnp.minimum(jnp.sum(pend[None, :] <= block_row0[:, None], axis=1), N_BUCKETS - 1)
    group_base = np.repeat(np.arange(N_GROUPS) * EXPERTS_PER_GROUP, N_PAIRS)
    expert_a = jnp.asarray(group_base + np.tile(PAIR_SLOT_A, N_GROUPS), I32)
    expert_b = jnp.asarray(group_base + np.tile(PAIR_SLOT_B, N_GROUPS), I32)
    blk_onehot = blk_bucket[:, None] == jnp.arange(N_BUCKETS)
    blk_ea = jnp.sum(jnp.where(blk_onehot, expert_a, 0), axis=1).astype(I32)
    blk_eb = jnp.sum(jnp.where(blk_onehot, expert_b, 0), axis=1).astype(I32)
    nused = (pend[-1:] // tm).astype(I32)
    onehot = idx[:, 0:1, :, None] == jnp.arange(N_BUCKETS, dtype=I32)
    dest = jnp.sum(jnp.where(onehot, pstart, 0), axis=-1) + idx[:, 1:2, :]

    pad_lo = (pstart + counts).astype(I32)
    pad_hi = jnp.concatenate([pstart[1:], jnp.full((1,), m_pad, I32)])
    w_tail = jnp.zeros((n, LANES), F32).at[:, :2].set(jnp.swapaxes(wts[:, :2, :], 1, 2).reshape(n, 2))
    xs = _dispatch(x1_list, seq_starts, dest, w_tail, p["norm2_w"], mod, pad_lo, pad_hi, m_pad)
    o_rows = _moe_experts(xs, blk_ea, blk_eb, nused, p["w_gate"], p["w_up"], p["w_down"])
    return [_combine(x1, dest[s0:s0 + x1.shape[0]], mod[s0:s0 + x1.shape[0]], o_rows).reshape(x1.shape)
            for x1, s0 in zip(x1_list, seq_starts)]


def kernel(x_prompt, x_sample, c_prompt, c_sample, norm1_w, norm2_w, w_ada, b_ada, w_in, q_norm_w, k_norm_w, b_igate, b_fgate, ml_norm_w, w_out, w_gr, b_gr, w_er, b_er, w_gate, w_up, w_down):
    assert x_prompt.shape[1:] == x_sample.shape[1:], "the request groups share one token buffer per sequence length"
    depth = norm1_w.shape[0]
    d = x_prompt.shape[-1]
    bp = x_prompt.shape[0]
    bs = x_sample.shape[0]
    rows = -(-(bp + bs) // 8) * 8
    y_prompt, y_sample = x_prompt, x_sample
    for l in range(depth):
        c_pad = jnp.zeros((rows, d), F32).at[:bp].set(c_prompt).at[bp:bp + bs].set(c_sample)
        mod = _ada_modulation(c_pad, w_ada[l], b_ada[l]).reshape(rows, 6, d)
        gate_bias = jnp.concatenate([b_igate[l].reshape(-1), b_fgate[l].reshape(-1)])
        w_router = jnp.concatenate([w_gr[l], w_er[l]], axis=1).T
        w_router = jnp.zeros((ROUTER_ROWS, d), F32).at[:N_GROUPS + N_EXPERTS].set(w_router)
        wr_hi = w_router.astype(BF16)
        b_router = jnp.zeros((ROUTER_ROWS,), F32).at[:N_GROUPS + N_EXPERTS].set(jnp.concatenate([b_gr[l], b_er[l]]))
        p = {
            "norm1_w": norm1_w[l].reshape(1, d),
            "norm2_w": norm2_w[l].reshape(1, d),
            "w_main": w_in[l][:, :C_MAIN].astype(BF16),
            "w_v_t": jnp.concatenate([w_in[l][:, C_AQ + C_AK:C_AQ + C_AK + C_AV],
                                      w_in[l][:, C_AQ + C_AK + C_AV + C_MQ:C_AQ + C_AK + C_AV + C_MQ + C_MK]],
                                     axis=1).T.astype(BF16),
            "w_gates": jnp.zeros((d, LANES), BF16).at[:, :C_MG].set(w_in[l][:, C_MAIN:].astype(BF16)),
            "q_norm_w": q_norm_w[l].reshape(1, HEAD_DIM),
            "k_norm_w": k_norm_w[l].reshape(1, HEAD_DIM),
            "gate_bias_col": jnp.zeros((1, LANES), F32).at[0, :C_MG].set(gate_bias),
            "gate_bias_row": jnp.broadcast_to(gate_bias[:, None], (C_MG, ML_CHUNK)),
            "ml_norm_w": ml_norm_w[l].reshape(1, C_MV),
            "w_out": w_out[l].astype(BF16),
            "wr_hi": wr_hi,
            "wr_lo": (w_router - wr_hi.astype(F32)).astype(BF16),
            "b_router": jnp.broadcast_to(b_router[:, None], (ROUTER_ROWS, LANES)),
            "w_gate": w_gate[l].astype(BF16),
            "w_up": w_up[l].astype(BF16),
            "w_down": w_down[l].astype(BF16),
        }
        x1_p, logits_p = _token_mixing(y_prompt, mod[:bp], p)
        x1_s, logits_s = _token_mixing(y_sample, mod[bp:bp + bs], p)
        y_prompt, y_sample = _channel_mixing([x1_p, x1_s], [logits_p, logits_s], mod, p)
    return (y_prompt, y_sample)
```

```python
import functools

import jax
import jax.numpy as jnp
import numpy as np
from jax import lax
from jax.experimental import pallas as pl
from jax.experimental.pallas import tpu as pltpu

F32 = jnp.float32
BF16 = jnp.bfloat16
I32 = jnp.int32

GRID_W = 64
HEAD_DIM = 128
ATT_HEADS = 8
ATT_KV_HEADS = 2
ATT_GROUP = ATT_HEADS // ATT_KV_HEADS
ROPE_THETA = 10000.0
ROPE_PAIRS = HEAD_DIM // 4
ML_HEADS = 4
ML_DV = 256
ML_DQK = 128
ML_CHUNK = 128
N_GROUPS = 4
EXPERTS_PER_GROUP = 4
N_EXPERTS = N_GROUPS * EXPERTS_PER_GROUP
EPS = 1e-6
Q_SCALE = HEAD_DIM ** -0.5 * float(np.log2(np.e))

C_AQ = ATT_HEADS * HEAD_DIM
C_AK = ATT_KV_HEADS * HEAD_DIM
C_AV = ATT_KV_HEADS * HEAD_DIM
C_MQ = ML_HEADS * ML_DQK
C_MK = ML_HEADS * ML_DQK
C_MV = ML_HEADS * ML_DV
C_MO = ML_HEADS * ML_DV
C_MG = 4 * ML_HEADS
C_MAIN = C_AQ + C_AK + C_AV + C_MQ + C_MK + C_MV + C_MO

N_PAIRS = 6
PAIR_SLOT_A = (0, 1, 2, 2, 0, 0)
PAIR_SLOT_B = (3, 3, 3, 1, 1, 2)
N_BUCKETS = N_GROUPS * N_PAIRS
BUCKET_ROWS = 32

LANES = 128
ROUTER_ROWS = 32
MIB = 1024 * 1024

TM_PROJ = 512
TQ_ATT = 256
TM_MIX = 512
TL_ROUTE = 512
TD_DISPATCH = 512
TM_MOE = 256
TC_COMBINE = 512
ML_SEQS = 4
ATT_SOFTMAX_LAG = 2
ATT_VALUES_LAG = 1
ATT_SLOTS = 4
ROW_CHUNK = 32
DMA_UNROLL = 8


def _params(semantics, vmem_mib):
    return pltpu.CompilerParams(dimension_semantics=semantics, vmem_limit_bytes=vmem_mib * MIB)


def _resident(shape):
    nd = len(shape)
    return pl.BlockSpec(shape, lambda *_: (0,) * nd, pipeline_mode=pl.Buffered(1))


def _ada_kernel(c_ref, w_ref, b_ref, o_ref):
    c = c_ref[...]
    s = (c * jax.nn.sigmoid(c)).astype(BF16)
    o_ref[...] = jnp.dot(s, w_ref[...].astype(BF16), preferred_element_type=F32) + b_ref[...]


def _ada_modulation(c_pad, w_ada, b_ada):
    rows, d = c_pad.shape
    n = w_ada.shape[1]
    tn = 1024
    return pl.pallas_call(
        _ada_kernel,
        grid=(n // tn,),
        in_specs=[pl.BlockSpec((rows, d), lambda j: (0, 0)),
                  pl.BlockSpec((d, tn), lambda j: (0, j)),
                  pl.BlockSpec((1, tn), lambda j: (0, j))],
        out_specs=pl.BlockSpec((rows, tn), lambda j: (0, j)),
        out_shape=jax.ShapeDtypeStruct((rows, n), F32),
        compiler_params=_params(("arbitrary",), 40),
        name="ada_modulation",
    )(c_pad, w_ada, b_ada.reshape(1, n))


def _rms(x, w):
    return x * lax.rsqrt(jnp.mean(x * x, axis=-1, keepdims=True) + EPS) * w


def _inproj_kernel(x_ref, mod_ref, n1_ref, w_ref, wvt_ref, wg_ref, qn_ref, kn_ref, cos_ref, sin_ref,
                   aq_ref, ak_ref, avt_ref, mq_ref, mkt_ref, mv_ref, mo_ref, mg_ref):
    x = x_ref[0]
    h = _rms(x, n1_ref[...]) * (1.0 + mod_ref[0, 1:2, :]) + mod_ref[0, 0:1, :]
    hb = h.astype(BF16)

    def proj(c0, width):
        return jnp.dot(hb, w_ref[:, c0:c0 + width], preferred_element_type=F32)

    cos = cos_ref[...]
    sin = sin_ref[...]
    lane = lax.broadcasted_iota(I32, (1, HEAD_DIM), 1)
    first = (lane % (2 * ROPE_PAIRS)) < ROPE_PAIRS

    def norm_rope(p, w):
        pn = _rms(p, w)
        partner = jnp.where(first, pltpu.roll(pn, HEAD_DIM - ROPE_PAIRS, 1), pltpu.roll(pn, ROPE_PAIRS, 1))
        return pn * cos + partner * sin

    c0 = 0
    for half in range(2):
        p = proj(c0, C_AQ // 2)
        for hh in range(ATT_HEADS // 2):
            col = half * (C_AQ // 2) + hh * HEAD_DIM
            qh = norm_rope(p[:, hh * HEAD_DIM:(hh + 1) * HEAD_DIM], qn_ref[...])
            aq_ref[0, :, col:col + HEAD_DIM] = (qh * Q_SCALE).astype(BF16)
        c0 += C_AQ // 2
    p = proj(c0, C_AK)
    for hh in range(ATT_KV_HEADS):
        ak_ref[0, :, hh * HEAD_DIM:(hh + 1) * HEAD_DIM] = norm_rope(p[:, hh * HEAD_DIM:(hh + 1) * HEAD_DIM], kn_ref[...]).astype(BF16)
    nt = (((1,), (1,)), ((), ()))
    avt_ref[0] = lax.dot_general(wvt_ref[0:C_AV, :], hb, nt, preferred_element_type=F32).astype(BF16)
    mkt_ref[0] = lax.dot_general(wvt_ref[C_AV:, :], hb, nt, preferred_element_type=F32).astype(BF16)
    c0 += C_AK + C_AV
    mq_ref[0] = (proj(c0, C_MQ) * (ML_DQK ** -0.5)).astype(BF16)
    c0 += C_MQ + C_MK
    for half in range(2):
        mv_ref[0, :, half * 512:(half + 1) * 512] = proj(c0, 512).astype(BF16)
        c0 += 512
    for half in range(2):
        mo_ref[0, :, half * 512:(half + 1) * 512] = proj(c0, 512)
        c0 += 512
    mg_ref[0] = jnp.dot(hb, wg_ref[...], preferred_element_type=F32)


def _in_projection(x, mod, norm1_w, w_main, w_v_t, w_gates, q_norm_w, k_norm_w, cos_t, sin_t):
    b, t, d = x.shape
    tm = min(TM_PROJ, t)
    row = lambda width: pl.BlockSpec((1, tm, width), lambda bi, i: (bi, i, 0))
    outs = ((C_AQ, False), (C_AK, False), (C_AV, True), (C_MQ, False), (C_MK, True), (C_MV, False),
            (C_MO, False), (LANES, False))
    dtypes = (BF16, BF16, BF16, BF16, BF16, BF16, F32, F32)
    col = lambda width: pl.BlockSpec((1, width, tm), lambda bi, i: (bi, 0, i))
    return pl.pallas_call(
        _inproj_kernel,
        grid=(b, t // tm),
        in_specs=[row(d),
                  pl.BlockSpec((1, 6, d), lambda bi, i: (bi, 0, 0)),
                  _resident((1, d)),
                  _resident((d, C_MAIN)),
                  _resident((C_AV + C_MK, d)),
                  _resident((d, LANES)),
                  _resident((1, HEAD_DIM)),
                  _resident((1, HEAD_DIM)),
                  pl.BlockSpec((tm, HEAD_DIM), lambda bi, i: (i, 0)),
                  pl.BlockSpec((tm, HEAD_DIM), lambda bi, i: (i, 0))],
        out_specs=[col(w) if tr else row(w) for w, tr in outs],
        out_shape=[jax.ShapeDtypeStruct((b, w, t) if tr else (b, t, w), dt) for (w, tr), dt in zip(outs, dtypes)],
        compiler_params=_params(("parallel", "arbitrary"), 48),
        name="in_projection",
    )(x, mod, norm1_w, w_main, w_v_t, w_gates, q_norm_w, k_norm_w, cos_t, sin_t)


def _attention_kernel(q_ref, qn_ref, k_ref, vt_ref, o_ref, s_ref, p_ref, l_ref):
    def scores(g, src_ref=q_ref):
        kv = g // ATT_GROUP
        q = src_ref[0, :, g * HEAD_DIM:(g + 1) * HEAD_DIM]
        k = k_ref[0, :, kv * HEAD_DIM:(kv + 1) * HEAD_DIM]
        s_ref[g % ATT_SLOTS] = lax.dot_general(k, q, (((1,), (1,)), ((), ())), preferred_element_type=F32)

    def softmax(g):
        s = s_ref[g % ATT_SLOTS]
        m = jnp.max(s, axis=0, keepdims=True)
        p = jnp.exp2(s - m)
        l_ref[g] = jnp.broadcast_to(jnp.sum(p, axis=0, keepdims=True), l_ref.shape[1:])
        p_ref[g % ATT_SLOTS] = p.astype(BF16)

    def values(g):
        kv = g // ATT_GROUP
        vt = vt_ref[0, kv * HEAD_DIM:(kv + 1) * HEAD_DIM, :]
        o_t = jnp.dot(vt, p_ref[g % ATT_SLOTS], preferred_element_type=F32)
        o_ref[0, :, g * HEAD_DIM:(g + 1) * HEAD_DIM] = (o_t * (1.0 / l_ref[g][0:1, :])).T.astype(BF16)

    n = ATT_HEADS
    lag = ATT_SOFTMAX_LAG

    @pl.when(pl.program_id(1) == 0)
    def _():
        for g in range(lag):
            scores(g)

    for g in range(n):
        if g + lag < n:
            scores(g + lag)
        else:
            scores(g + lag - n, qn_ref)
        softmax(g)
        if g >= ATT_VALUES_LAG:
            values(g - ATT_VALUES_LAG)
    for g in range(n - ATT_VALUES_LAG, n):
        values(g)


def _attention(aq, ak, av_t):
    b, t, _ = aq.shape
    tq = min(TQ_ATT, t)
    nt = t // tq
    return pl.pallas_call(
        _attention_kernel,
        grid=(b, nt),
        in_specs=[pl.BlockSpec((1, tq, C_AQ), lambda bi, i: (bi, i, 0)),
                  pl.BlockSpec((1, tq, C_AQ), lambda bi, i: (bi, jnp.minimum(i + 1, nt - 1), 0)),
                  pl.BlockSpec((1, t, C_AK), lambda bi, i: (bi, 0, 0)),
                  pl.BlockSpec((1, C_AV, t), lambda bi, i: (bi, 0, 0))],
        out_specs=pl.BlockSpec((1, tq, C_AQ), lambda bi, i: (bi, i, 0)),
        out_shape=jax.ShapeDtypeStruct((b, t, C_AQ), BF16),
        scratch_shapes=[pltpu.VMEM((ATT_SLOTS, t, tq), F32), pltpu.VMEM((ATT_SLOTS, t, tq), BF16),
                        pltpu.VMEM((ATT_HEADS, 8, tq), F32)],
        compiler_params=_params(("parallel", "arbitrary"), 48),
        name="attention",
    )(aq, aq, ak, av_t)


def _log_sigmoid(x):
    return jnp.minimum(x, 0.0) - jnp.log1p(jnp.exp(-jnp.abs(x)))


def _split3(a):
    a1 = a.astype(BF16)
    r1 = a - a1.astype(F32)
    a2 = r1.astype(BF16)
    a3 = (r1 - a2.astype(F32)).astype(BF16)
    return a1, a2, a3


def _lane_pick(a, idx):
    lane = lax.broadcasted_iota(I32, a.shape, 1)
    return jnp.sum(jnp.where(lane == idx, a, 0.0), axis=-1, keepdims=True)


def _mlstm_kernel(qf, kf, vf, gcf, grf, qb, kb, vb, gcb, grb, bc_ref, br_ref, hf_ref, hb_ref,
                  ct_ref, m_ref):
    @pl.when(pl.program_id(1) == 0)
    def _():
        ct_ref[...] = jnp.zeros_like(ct_ref)
        m_ref[...] = jnp.zeros_like(m_ref)

    L = ML_CHUNK
    row = lax.broadcasted_iota(I32, (L, L), 0)
    col = lax.broadcasted_iota(I32, (L, L), 1)
    lower = (col <= row)
    upper = (col >= row)
    lower_b = lower.astype(BF16)
    upper_b = upper.astype(BF16)

    gates = {}
    for bb in range(qf.shape[0]):
        for d, (gc_ref, gr_ref) in enumerate(((gcf, grf), (gcb, grb))):
            cum_cols = upper_b if d else lower_b
            cum_rows = lower_b if d else upper_b
            pre_c = gc_ref[bb] + bc_ref[...]
            pre_r = gr_ref[bb] + br_ref[...]
            b_cols = sum(jnp.dot(cum_cols, part, preferred_element_type=F32) for part in _split3(_log_sigmoid(pre_c)))
            b_rows = sum(jnp.dot(part, cum_rows, preferred_element_type=F32) for part in _split3(_log_sigmoid(pre_r)))
            gates[bb, d] = (pre_r, b_cols, b_rows)

    chains = [(bb, d, hh) for bb in range(qf.shape[0]) for d in range(2) for hh in range(ML_HEADS)]
    refs = ((qf, kf, vf, hf_ref), (qb, kb, vb, hb_ref))
    ones_block = jnp.ones((L, LANES), BF16)


    st = {}
    for (bb, d, hh) in chains:
        q_ref, kt_ref, v_ref, _ = refs[d]
        pre_r, b_cols, b_rows = gates[bb, d]
        ci = d * ML_HEADS + hh
        cf = 2 * ML_HEADS + d * ML_HEADS + hh
        sidx = (bb * 2 + d) * ML_HEADS + hh
        i_row = pre_r[ci:ci + 1, :]
        b_row = b_rows[cf:cf + 1, :]
        m = m_ref[sidx][0:1, 0:1]
        c_vis = jnp.where(upper if d else lower, i_row - b_row, -jnp.inf)
        shift = -jnp.maximum(m, jnp.max(c_vis, axis=-1, keepdims=True))
        dec = jnp.exp(c_vis + shift)
        a_inter = jnp.exp(m + shift)
        floor = jnp.exp(shift - _lane_pick(b_cols, cf))
        q = q_ref[bb, :, hh * ML_DQK:(hh + 1) * ML_DQK]
        kt = kt_ref[bb, hh * ML_DQK:(hh + 1) * ML_DQK, :]
        v_aug = jnp.concatenate([v_ref[bb, :, hh * ML_DV:(hh + 1) * ML_DV], ones_block], axis=1)
        qk = jnp.dot(q, kt, preferred_element_type=F32)
        ct = ct_ref[sidx]
        inter = jnp.dot(q, ct.astype(BF16), preferred_element_type=F32)
        b_end = _lane_pick(b_row, 0 if d else L - 1)
        g_row = b_end - b_row + i_row
        m_new = jnp.maximum(b_end + m, jnp.max(g_row, axis=-1, keepdims=True))
        decay = jnp.exp(b_end + m - m_new)
        w_row = jnp.exp(g_row - m_new)
        upd = jnp.dot((kt.astype(F32) * w_row).astype(BF16), v_aug, preferred_element_type=F32)
        st[bb, d, hh] = (sidx, dec, a_inter, floor, v_aug, qk, ct, inter, m_new, decay, upd)

    sv = {}
    for key in chains:
        sidx, dec, a_inter, floor, v_aug, qk, ct, inter, m_new, decay, upd = st[key]
        sv[key] = jnp.dot((qk * dec).astype(BF16), v_aug, preferred_element_type=F32)

    for key in chains:
        bb, d, hh = key
        sidx, dec, a_inter, floor, v_aug, qk, ct, inter, m_new, decay, upd = st[key]
        both = a_inter * inter + sv[key]
        scale = 1.0 / jnp.maximum(jnp.abs(both[:, ML_DV:]), floor)
        h_ref = refs[d][3]
        for blk in range(ML_DV // LANES):
            lo = hh * ML_DV + blk * LANES
            h_ref[bb, :, lo:lo + LANES] = both[:, blk * LANES:(blk + 1) * LANES] * scale
        ct_ref[sidx] = decay * ct + upd
        m_ref[sidx] = jnp.broadcast_to(m_new, (8, LANES))


def _mlstm(mq, mk_t, mv, mg, mg_t, bias_col, bias_row):
    b, t, _ = mq.shape
    L = ML_CHUNK
    nc = t // L
    sb = ML_SEQS if b % ML_SEQS == 0 else 1
    fw = lambda width: pl.BlockSpec((sb, L, width), lambda bi, c: (bi, c, 0))
    bw = lambda width: pl.BlockSpec((sb, L, width), lambda bi, c: (bi, nc - 1 - c, 0))
    fw_t = lambda rows: pl.BlockSpec((sb, rows, L), lambda bi, c: (bi, 0, c))
    bw_t = lambda rows: pl.BlockSpec((sb, rows, L), lambda bi, c: (bi, 0, nc - 1 - c))
    ns = sb * 2 * ML_HEADS
    return pl.pallas_call(
        _mlstm_kernel,
        grid=(b // sb, nc),
        in_specs=[fw(C_MQ), fw_t(C_MK), fw(C_MV), fw(LANES), fw_t(C_MG),
                  bw(C_MQ), bw_t(C_MK), bw(C_MV), bw(LANES), bw_t(C_MG),
                  _resident((1, LANES)), _resident((C_MG, L))],
        out_specs=[fw(C_MV), bw(C_MV)],
        out_shape=[jax.ShapeDtypeStruct((b, t, C_MV), F32)] * 2,
        scratch_shapes=[pltpu.VMEM((ns, ML_DQK, ML_DV + LANES), F32),
                        pltpu.VMEM((ns, 8, LANES), F32)],
        compiler_params=_params(("parallel", "arbitrary"), 48),
        name="mlstm",
    )(mq, mk_t, mv, mg, mg_t, mq, mk_t, mv, mg, mg_t, bias_col, bias_row)


def _moe_input(x1, n2_ref, mod_ref):
    return _rms(x1, n2_ref[...]) * (1.0 + mod_ref[0, 4:5, :]) + mod_ref[0, 3:4, :]


def _mix_kernel(x_ref, att_ref, hf_ref, hb_ref, mo_ref, mlw_ref, wo_ref, mod_ref, n2_ref,
                wrh_ref, wrl_ref, br_ref, x1_ref, lg_ref):
    ml = hf_ref[0] + hb_ref[0]
    gate = jax.nn.sigmoid(mo_ref[0])
    mix = jnp.dot(att_ref[0], wo_ref[0:C_AQ, :], preferred_element_type=F32)
    for hh in range(ML_HEADS):
        sl = slice(hh * ML_DV, (hh + 1) * ML_DV)
        seg = (_rms(ml[:, sl], mlw_ref[:, sl]) * gate[:, sl]).astype(BF16)
        mix += jnp.dot(seg, wo_ref[C_AQ + hh * ML_DV:C_AQ + (hh + 1) * ML_DV, :], preferred_element_type=F32)
    x1 = x_ref[0] + mod_ref[0, 2:3, :] * mix
    x1_ref[0] = x1
    h2 = _moe_input(x1, n2_ref, mod_ref)
    h_hi = h2.astype(BF16)
    h_lo = (h2 - h_hi.astype(F32)).astype(BF16)
    nt = (((1,), (1,)), ((), ()))
    both = lax.dot_general(jnp.concatenate([wrh_ref[...], wrl_ref[...]], axis=0), h_hi, nt, preferred_element_type=F32)
    lg = both[:ROUTER_ROWS] + both[ROUTER_ROWS:] + lax.dot_general(wrh_ref[...], h_lo, nt, preferred_element_type=F32)
    lg_ref[0] = lg + br_ref[:, 0:1]


def _mix_and_router(x, att, h_fw, h_bw, mo, ml_norm_w, w_out, mod, norm2_w, wr_hi, wr_lo, b_router):
    b, t, d = x.shape
    tm = min(TM_MIX, t)
    row = lambda width: pl.BlockSpec((1, tm, width), lambda bi, i: (bi, i, 0))
    return pl.pallas_call(
        _mix_kernel,
        grid=(b, t // tm),
        in_specs=[row(d), row(C_AQ), row(C_MV), row(C_MV), row(C_MO),
                  _resident((1, C_MV)), _resident((d, d)),
                  pl.BlockSpec((1, 6, d), lambda bi, i: (bi, 0, 0)),
                  _resident((1, d)), _resident((ROUTER_ROWS, d)), _resident((ROUTER_ROWS, d)),
                  _resident((ROUTER_ROWS, LANES))],
        out_specs=[row(d), pl.BlockSpec((1, ROUTER_ROWS, tm), lambda bi, i: (bi, 0, i))],
        out_shape=[jax.ShapeDtypeStruct((b, t, d), F32), jax.ShapeDtypeStruct((b, ROUTER_ROWS, t), F32)],
        compiler_params=_params(("parallel", "arbitrary"), 48),
        name="mix_router",
    )(x, att, h_fw, h_bw, mo, ml_norm_w, w_out, mod, norm2_w, wr_hi, wr_lo, b_router)


def _first_argmax(rows):
    best = rows[0]
    idx = jnp.zeros_like(best)
    for j in range(1, len(rows)):
        better = rows[j] > best
        best = jnp.where(better, rows[j], best)
        idx = jnp.where(better, float(j), idx)
    return best, idx


def _softmax_rows(rows):
    mx = functools.reduce(jnp.maximum, rows)
    ex = [jnp.exp(r - mx) for r in rows]
    tot = functools.reduce(lambda a, c: a + c, ex)
    return [e / tot for e in ex]


def _group_of(seq, seq_starts):
    out = []
    for gi, lo in enumerate(seq_starts[:-1]):
        hi = seq_starts[gi + 1]
        out.append(((seq >= lo) & (seq < hi), jnp.clip(seq - lo, 0, hi - lo - 1)))
    return out


def _select_group(seq, seq_starts, values):
    picked = values[-1]
    for (active, _), v in list(zip(_group_of(seq, seq_starts), values))[-2::-1]:
        picked = jnp.where(active, v, picked)
    return picked


def _route_kernel(seq_starts, *refs):
    lg_refs = refs[:len(seq_starts) - 1]
    idx_ref, wt_ref, cnt_ref, run_ref = refs[len(seq_starts) - 1:]

    @pl.when((pl.program_id(0) == 0) & (pl.program_id(1) == 0))
    def _():
        run_ref[...] = jnp.zeros_like(run_ref)

    tl = idx_ref.shape[2]
    lg = _select_group(pl.program_id(0), seq_starts, [r[0] for r in lg_refs])
    p_grp = _softmax_rows([lg[g:g + 1, :] for g in range(N_GROUPS)])
    p_g, g_idx = _first_argmax(p_grp)
    el = []
    for j in range(EXPERTS_PER_GROUP):
        sel = lg[N_GROUPS + (N_GROUPS - 1) * EXPERTS_PER_GROUP + j:N_GROUPS + (N_GROUPS - 1) * EXPERTS_PER_GROUP + j + 1, :]
        for g in range(N_GROUPS - 2, -1, -1):
            r = N_GROUPS + g * EXPERTS_PER_GROUP + j
            sel = jnp.where(g_idx == float(g), lg[r:r + 1, :], sel)
        el.append(sel)
    pe = _softmax_rows(el)
    w1, i1 = _first_argmax(pe)
    rest = [jnp.where(i1 == float(j), -1.0, pe[j]) for j in range(EXPERTS_PER_GROUP)]
    w2, i2 = _first_argmax(rest)
    tot = w1 + w2
    wt1 = w1 / tot * p_g
    wt2 = w2 / tot * p_g
    lo = jnp.minimum(i1, i2)
    hi = jnp.maximum(i1, i2)
    pair = jnp.where(hi == 3.0, lo, jnp.where(lo == 1.0, 3.0, jnp.where(hi == 1.0, 4.0, 5.0)))
    slot_a = functools.reduce(lambda acc, p: jnp.where(pair == float(p), float(PAIR_SLOT_A[p]), acc),
                              range(1, N_PAIRS), jnp.full_like(pair, float(PAIR_SLOT_A[0])))
    first_in_a = slot_a == i1
    wa = jnp.where(first_in_a, wt1, wt2)
    wb = jnp.where(first_in_a, wt2, wt1)
    bucket = (g_idx * float(N_PAIRS) + pair).astype(I32)

    brow = lax.broadcasted_iota(I32, (BUCKET_ROWS, tl), 0)
    hit = brow == bucket
    onehot = hit.astype(F32)
    before = lax.broadcasted_iota(I32, (tl, tl), 0) <= lax.broadcasted_iota(I32, (tl, tl), 1)
    incl = jnp.dot(onehot.astype(BF16), before.astype(BF16), preferred_element_type=F32)
    base = run_ref[:, 0:1] + incl - onehot
    rank = jnp.sum(jnp.where(hit, base, 0.0), axis=0, keepdims=True).astype(I32)
    total = run_ref[...] + jnp.sum(onehot, axis=-1, keepdims=True)
    run_ref[...] = total
    cnt_ref[...] = total.astype(I32)

    idx_ref[0] = jnp.concatenate([bucket, rank, jnp.zeros((6, tl), I32)], axis=0)
    wt_ref[0] = jnp.concatenate([wa, wb, jnp.zeros((6, tl), F32)], axis=0)


def _group_spec(block, seq_starts, gi, seq_of, inner_of, place):
    lo, hi = seq_starts[gi], seq_starts[gi + 1]

    def index_map(*ids):
        seq, inner, n_inner = seq_of(*ids), inner_of(*ids), place[1]
        local = jnp.clip(seq - lo, 0, hi - lo - 1)
        inner = jnp.where(seq < lo, 0, jnp.where(seq >= hi, n_inner - 1, inner))
        return place[0](local, inner)

    return pl.BlockSpec(block, index_map)


def _route(logits_list, seq_starts):
    b = seq_starts[-1]
    t = logits_list[0].shape[2]
    tl = min(TL_ROUTE, t)
    nt = t // tl
    blk = lambda rows: pl.BlockSpec((1, rows, tl), lambda bi, i: (bi, 0, i))
    lg_specs = [_group_spec((1, ROUTER_ROWS, tl), seq_starts, gi, lambda bi, i: bi, lambda bi, i: i,
                            (lambda s, i: (s, 0, i), nt)) for gi in range(len(logits_list))]
    return pl.pallas_call(
        functools.partial(_route_kernel, seq_starts),
        grid=(b, nt),
        in_specs=lg_specs,
        out_specs=[blk(8), blk(8), pl.BlockSpec((BUCKET_ROWS, LANES), lambda bi, i: (0, 0))],
        out_shape=[jax.ShapeDtypeStruct((b, 8, t), I32), jax.ShapeDtypeStruct((b, 8, t), F32),
                   jax.ShapeDtypeStruct((BUCKET_ROWS, LANES), I32)],
        scratch_shapes=[pltpu.VMEM((BUCKET_ROWS, LANES), F32)],
        compiler_params=_params(("arbitrary", "arbitrary"), 32),
        name="route",
    )(*logits_list)


def _row_copy(src_ref, src_row, dst_ref, dst_row, sem):
    return pltpu.make_async_copy(src_ref.at[pl.ds(src_row, 1), :], dst_ref.at[pl.ds(dst_row, 1), :], sem)


def _dispatch_kernel(seq_starts, tiles_per_seq, padlo_ref, padhi_ref, dest_ref, *refs):
    x1_refs = refs[:len(seq_starts) - 1]
    wtail_ref, n2_ref, mod_ref, xs_ref, h_ref, zero_ref, sem = refs[len(seq_starts) - 1:]
    _, td, d = x1_refs[0].shape
    i = pl.program_id(0)
    seq = i // tiles_per_seq
    slot = i % 2
    rows = h_ref.at[slot]
    for c0 in range(0, td, ROW_CHUNK):
        sl = slice(c0, c0 + ROW_CHUNK)
        x1 = _select_group(seq, seq_starts, [r[0, sl, :] for r in x1_refs])
        rows[sl, 0:d] = _moe_input(x1, n2_ref, mod_ref)
        rows[sl, d:] = wtail_ref[sl, :]
        for t in range(c0, c0 + ROW_CHUNK):
            _row_copy(rows, t, xs_ref, dest_ref[0, 0, t], sem.at[slot]).start()

    def drain(which):
        pltpu.make_async_copy(h_ref.at[which], xs_ref.at[pl.ds(0, td), :], sem.at[which]).wait()

    @pl.when(i > 0)
    def _():
        drain(1 - slot)

    @pl.when(i == seq_starts[-1] * tiles_per_seq - 1)
    def _():
        drain(slot)
        zero_ref[...] = jnp.zeros_like(zero_ref)
        zsem = sem.at[0]

        sub = zero_ref.shape[0]

        def one_row(r):
            return _row_copy(zero_ref, 0, xs_ref, r, zsem)

        def aligned_rows(r):
            return pltpu.make_async_copy(zero_ref, xs_ref.at[pl.ds(pl.multiple_of(r, sub), sub), :], zsem)

        def spans(e):
            lo, hi = padlo_ref[e], padhi_ref[e]
            mid = jnp.minimum(hi, (lo + sub - 1) // sub * sub)
            return lo, mid, (hi - mid) // sub

        for wait in (False, True):
            for e in range(N_BUCKETS):
                lo, mid, nblk = spans(e)

                def head(r, carry):
                    one_row(r).wait() if wait else one_row(r).start()
                    return carry

                def body(c, carry):
                    cp = aligned_rows(mid + c * sub)
                    cp.wait() if wait else cp.start()
                    return carry

                lax.fori_loop(lo, mid, head, 0)
                lax.fori_loop(0, nblk, body, 0)


def _dispatch(x1_list, seq_starts, dest, w_tail, norm2_w, mod, pad_lo, pad_hi, m_pad):
    d = x1_list[0].shape[2]
    b, _, t = dest.shape
    n = b * t
    td = min(TD_DISPATCH, t)
    per = t // td
    width = d + LANES
    x1_specs = [_group_spec((1, td, d), seq_starts, gi, lambda i, *_: i // per, lambda i, *_: i % per,
                            (lambda s, j: (s, j, 0), per)) for gi in range(len(x1_list))]
    return pl.pallas_call(
        functools.partial(_dispatch_kernel, seq_starts, per),
        grid_spec=pltpu.PrefetchScalarGridSpec(
            num_scalar_prefetch=2,
            grid=(n // td,),
            in_specs=[pl.BlockSpec((1, 1, td), lambda i, *_: (i // per, 0, i % per), memory_space=pltpu.SMEM),
                      *x1_specs,
                      pl.BlockSpec((td, LANES), lambda i, *_: (i, 0)),
                      pl.BlockSpec((1, d), lambda i, *_: (0, 0)),
                      pl.BlockSpec((1, 6, d), lambda i, *_: (i // per, 0, 0))],
            out_specs=pl.BlockSpec(memory_space=pl.ANY),
            scratch_shapes=[pltpu.VMEM((2, td, width), F32), pltpu.VMEM((8, width), F32),
                            pltpu.SemaphoreType.DMA((2,))]),
        out_shape=jax.ShapeDtypeStruct((m_pad, width), F32),
        compiler_params=_params(("arbitrary",), 32),
        name="dispatch",
    )(pad_lo, pad_hi, dest, *x1_list, w_tail, norm2_w, mod)


def _moe_kernel(ea_ref, eb_ref, nused_ref, x_ref, wga_ref, wua_ref, wda_ref, wgb_ref, wub_ref, wdb_ref, o_ref):
    r = pl.program_id(0)
    d = o_ref.shape[1]

    @pl.when(r < nused_ref[0])
    def _():
        xb = x_ref[:, 0:d].astype(BF16)

        def expert(wg_ref, wu_ref, wd_ref):
            g = jnp.dot(xb, wg_ref[0], preferred_element_type=F32)
            u = jnp.dot(xb, wu_ref[0], preferred_element_type=F32)
            h = (g * jax.nn.sigmoid(g) * u).astype(BF16)
            return jnp.dot(h, wd_ref[0], preferred_element_type=F32)

        tail = x_ref[:, d:]
        o_ref[...] = (expert(wga_ref, wua_ref, wda_ref) * _lane_pick(tail, 0)
                      + expert(wgb_ref, wub_ref, wdb_ref) * _lane_pick(tail, 1))

    @pl.when(r >= nused_ref[0])
    def _():
        o_ref[...] = jnp.zeros_like(o_ref)


def _moe_experts(xs, blk_ea, blk_eb, nused, w_gate, w_up, w_down):
    m_pad, width = xs.shape
    d = width - LANES
    f = w_gate.shape[2]
    tm = TM_MOE
    last = lambda r, nu: jnp.minimum(r, nu[0] - 1)
    slot = lambda shape, pick, bufs: pl.BlockSpec(shape, lambda r, ea, eb, nu: (pick(ea, eb)[last(r, nu)], 0, 0),
                                                  pipeline_mode=pl.Buffered(bufs))
    slot_a = lambda shape: slot(shape, lambda ea, eb: ea, 2)
    slot_b = lambda shape: slot(shape, lambda ea, eb: eb, 1)
    return pl.pallas_call(
        _moe_kernel,
        grid_spec=pltpu.PrefetchScalarGridSpec(
            num_scalar_prefetch=3,
            grid=(m_pad // tm,),
            in_specs=[pl.BlockSpec((tm, width), lambda r, ea, eb, nu: (last(r, nu), 0)),
                      slot_a((1, d, f)), slot_a((1, d, f)), slot_a((1, f, d)),
                      slot_b((1, d, f)), slot_b((1, d, f)), slot_b((1, f, d))],
            out_specs=pl.BlockSpec((tm, d), lambda r, ea, eb, nu: (r, 0))),
        out_shape=jax.ShapeDtypeStruct((m_pad, d), F32),
        compiler_params=_params(("arbitrary",), 58),
        name="moe_experts",
    )(blk_ea, blk_eb, nused, xs, w_gate, w_up, w_down, w_gate, w_up, w_down)


def _combine_kernel(n_steps, dest_ref, dnext_ref, x1_ref, mod_ref, o_ref, y_ref, g_ref, sem):
    tc = x1_ref.shape[0]
    i = pl.program_id(0)
    slot = i % 2

    def drain(which):
        pltpu.make_async_copy(o_ref.at[pl.ds(0, tc), :], g_ref.at[which], sem.at[which]).wait()

    @pl.when(i == 0)
    def _():
        def start(t, carry):
            _row_copy(o_ref, dest_ref[0, 0, t], g_ref.at[slot], t, sem.at[slot]).start()
            return carry

        lax.fori_loop(0, tc, start, 0, unroll=DMA_UNROLL)

    drain(slot)
    rows = g_ref.at[slot]
    for c0 in range(0, tc, ROW_CHUNK):
        sl = slice(c0, c0 + ROW_CHUNK)
        y_ref[sl, :] = x1_ref[sl, :] + mod_ref[0, 5:6, :] * rows[sl, :]
        for t in range(c0, c0 + ROW_CHUNK):
            _row_copy(o_ref, dnext_ref[0, 0, t], g_ref.at[1 - slot], t, sem.at[1 - slot]).start()

    @pl.when(i == n_steps - 1)
    def _():
        drain(1 - slot)


def _combine(x1, dest, mod, o_rows):
    nseq, t, d = x1.shape
    x1 = x1.reshape(nseq * t, d)
    tc = min(TC_COMBINE, t)
    per = t // tc
    ntiles = nseq * per
    tile = lambda g: (g // per, 0, g % per)
    return pl.pallas_call(
        functools.partial(_combine_kernel, ntiles),
        grid=(ntiles,),
        in_specs=[pl.BlockSpec((1, 1, tc), tile, memory_space=pltpu.SMEM),
                  pl.BlockSpec((1, 1, tc), lambda i: tile(jnp.minimum(i + 1, ntiles - 1)), memory_space=pltpu.SMEM),
                  pl.BlockSpec((tc, d), lambda i: (i, 0)),
                  pl.BlockSpec((1, 6, d), lambda i: (i // per, 0, 0)),
                  pl.BlockSpec(memory_space=pl.ANY)],
        out_specs=pl.BlockSpec((tc, d), lambda i: (i, 0)),
        scratch_shapes=[pltpu.VMEM((2, tc, d), F32), pltpu.SemaphoreType.DMA((2,))],
        out_shape=jax.ShapeDtypeStruct((nseq * t, d), F32),
        compiler_params=_params(("arbitrary",), 32),
        name="combine",
    )(dest, dest, x1, mod, o_rows)


def _rope_tables(t):
    rows = t // GRID_W
    row = jnp.repeat(jnp.arange(rows, dtype=F32), GRID_W)
    col = jnp.tile(jnp.arange(GRID_W, dtype=F32), rows)
    freqs = ROPE_THETA ** (-jnp.arange(ROPE_PAIRS, dtype=F32) / ROPE_PAIRS)
    ar = row[:, None] * freqs
    ac = col[:, None] * freqs
    cos_t = jnp.concatenate([jnp.cos(ar), jnp.cos(ar), jnp.cos(ac), jnp.cos(ac)], axis=-1)
    sin_t = jnp.concatenate([-jnp.sin(ar), jnp.sin(ar), -jnp.sin(ac), jnp.sin(ac)], axis=-1)
    return cos_t, sin_t


def _token_mixing(x, mod, p):
    b, t, d = x.shape
    cos_t, sin_t = _rope_tables(t)
    aq, ak, av_t, mq, mk_t, mv, mo, mg = _in_projection(
        x, mod, p["norm1_w"], p["w_main"], p["w_v_t"], p["w_gates"], p["q_norm_w"], p["k_norm_w"], cos_t, sin_t)
    att = _attention(aq, ak, av_t)
    mg_t = jnp.swapaxes(mg[:, :, :C_MG], 1, 2)
    h_fw, h_bw = _mlstm(mq, mk_t, mv, mg, mg_t, p["gate_bias_col"], p["gate_bias_row"])
    return _mix_and_router(x, att, h_fw, h_bw, mo, p["ml_norm_w"], p["w_out"], mod,
                           p["norm2_w"], p["wr_hi"], p["wr_lo"], p["b_router"])


def _channel_mixing(x1_list, logits_list, mod, p):
    seq_starts = tuple(int(v) for v in np.cumsum([0] + [x1.shape[0] for x1 in x1_list]))
    b = seq_starts[-1]
    _, t, d = x1_list[0].shape
    n = b * t
    idx, wts, counts = _route(logits_list, seq_starts)

    tm = TM_MOE
    counts = counts[:N_BUCKETS, 0]
    padded = (counts + tm - 1) // tm * tm
    pend = jnp.cumsum(padded)
    pstart = (pend - padded).astype(I32)
    nb = (n + tm - 1) // tm + N_BUCKETS
    m_pad = nb * tm
    block_row0 = jnp.arange(nb, dtype=I32) * tm
    blk_bucket = jnp.minimum(jnp.sum(pend[None, :] <= block_row0[:, None], axis=1), N_BUCKETS - 1)
    group_base = np.repeat(np.arange(N_GROUPS) * EXPERTS_PER_GROUP, N_PAIRS)
    expert_a = jnp.asarray(group_base + np.tile(PAIR_SLOT_A, N_GROUPS), I32)
    expert_b = jnp.asarray(group_base + np.tile(PAIR_SLOT_B, N_GROUPS), I32)
    blk_onehot = blk_bucket[:, None] == jnp.arange(N_BUCKETS)
    blk_ea = jnp.sum(jnp.where(blk_onehot, expert_a, 0), axis=1).astype(I32)
    blk_eb = jnp.sum(jnp.where(blk_onehot, expert_b, 0), axis=1).astype(I32)
    nused = (pend[-1:] // tm).astype(I32)
    onehot = idx[:, 0:1, :, None] == jnp.arange(N_BUCKETS, dtype=I32)
    dest = jnp.sum(jnp.where(onehot, pstart, 0), axis=-1) + idx[:, 1:2, :]

    pad_lo = (pstart + counts).astype(I32)
    pad_hi = jnp.concatenate([pstart[1:], jnp.full((1,), m_pad, I32)])
    w_tail = jnp.zeros((n, LANES), F32).at[:, :2].set(jnp.swapaxes(wts[:, :2, :], 1, 2).reshape(n, 2))
    xs = _dispatch(x1_list, seq_starts, dest, w_tail, p["norm2_w"], mod, pad_lo, pad_hi, m_pad)
    o_rows = _moe_experts(xs, blk_ea, blk_eb, nused, p["w_gate"], p["w_up"], p["w_down"])
    return [_combine(x1, dest[s0:s0 + x1.shape[0]], mod[s0:s0 + x1.shape[0]], o_rows).reshape(x1.shape)
            for x1, s0 in zip(x1_list, seq_starts)]


def kernel(x_prompt, x_sample, c_prompt, c_sample, norm1_w, norm2_w, w_ada, b_ada, w_in, q_norm_w, k_norm_w, b_igate, b_fgate, ml_norm_w, w_out, w_gr, b_gr, w_er, b_er, w_gate, w_up, w_down):
    assert x_prompt.shape[1:] == x_sample.shape[1:], "the request groups share one token buffer per sequence length"
    depth = norm1_w.shape[0]
    d = x_prompt.shape[-1]
    bp = x_prompt.shape[0]
    bs = x_sample.shape[0]
    rows = -(-(bp + bs) // 8) * 8
    y_prompt, y_sample = x_prompt, x_sample
    for l in range(depth):
        c_pad = jnp.zeros((rows, d), F32).at[:bp].set(c_prompt).at[bp:bp + bs].set(c_sample)
        mod = _ada_modulation(c_pad, w_ada[l], b_ada[l]).reshape(rows, 6, d)
        gate_bias = jnp.concatenate([b_igate[l].reshape(-1), b_fgate[l].reshape(-1)])
        w_router = jnp.concatenate([w_gr[l], w_er[l]], axis=1).T
        w_router = jnp.zeros((ROUTER_ROWS, d), F32).at[:N_GROUPS + N_EXPERTS].set(w_router)
        wr_hi = w_router.astype(BF16)
        b_router = jnp.zeros((ROUTER_ROWS,), F32).at[:N_GROUPS + N_EXPERTS].set(jnp.concatenate([b_gr[l], b_er[l]]))
        p = {
            "norm1_w": norm1_w[l].reshape(1, d),
            "norm2_w": norm2_w[l].reshape(1, d),
            "w_main": w_in[l][:, :C_MAIN].astype(BF16),
            "w_v_t": jnp.concatenate([w_in[l][:, C_AQ + C_AK:C_AQ + C_AK + C_AV],
                                      w_in[l][:, C_AQ + C_AK + C_AV + C_MQ:C_AQ + C_AK + C_AV + C_MQ + C_MK]],
                                     axis=1).T.astype(BF16),
            "w_gates": jnp.zeros((d, LANES), BF16).at[:, :C_MG].set(w_in[l][:, C_MAIN:].astype(BF16)),
            "q_norm_w": q_norm_w[l].reshape(1, HEAD_DIM),
            "k_norm_w": k_norm_w[l].reshape(1, HEAD_DIM),
            "gate_bias_col": jnp.zeros((1, LANES), F32).at[0, :C_MG].set(gate_bias),
            "gate_bias_row": jnp.broadcast_to(gate_bias[:, None], (C_MG, ML_CHUNK)),
            "ml_norm_w": ml_norm_w[l].reshape(1, C_MV),
            "w_out": w_out[l].astype(BF16),
            "wr_hi": wr_hi,
            "wr_lo": (w_router - wr_hi.astype(F32)).astype(BF16),
            "b_router": jnp.broadcast_to(b_router[:, None], (ROUTER_ROWS, LANES)),
            "w_gate": w_gate[l].astype(BF16),
            "w_up": w_up[l].astype(BF16),
            "w_down": w_down[l].astype(BF16),
        }
        x1_p, logits_p = _token_mixing(y_prompt, mod[:bp], p)
        x1_s, logits_s = _token_mixing(y_sample, mod[bp:bp + bs], p)
        y_prompt, y_sample = _channel_mixing([x1_p, x1_s], [logits_p, logits_s], mod, p)
    return (y_prompt, y_sample)
```
